```python
import math
import jax, jax.numpy as jnp
from jax import lax
import numpy as np

D_MODEL = 2048
BATCH = 1
SEQ = 16384
DEPTH = 2

GRID_W = 64
CTX_LEN = 256
HEAD_DIM = 128
MIX_W = D_MODEL // 4
N_BRANCH = 4
A_HEADS = MIX_W // HEAD_DIM
A_KV_HEADS = 2
ROPE_THETA = 10000.0
Q_BLOCK = 128
B_HEADS = MIX_W // HEAD_DIM
NA_ROWS = 8
NA_COLS = 16
C_HEADS = 4
C_DV = MIX_W // C_HEADS
C_DK = C_DV // 2
C_GATE_RANK = 16
C_GATE_TAU = 16.0
C_CHUNK = 64
D_CH = MIX_W
HY_EMB = 33
HY_FFN = 64
HY_DECAY_MIN = math.log(1e-2) / 1.5
HY_DECAY_MAX = math.log(1e-2) / 0.3
D_FF = 11 * D_MODEL // 4
NORM_EPS = 1e-6
IN_SPLITS = (A_HEADS * HEAD_DIM, A_KV_HEADS * HEAD_DIM, A_KV_HEADS * HEAD_DIM,
             B_HEADS * HEAD_DIM, B_HEADS * HEAD_DIM, B_HEADS * HEAD_DIM,
             C_HEADS * C_DK, C_HEADS * C_DK, C_HEADS * C_DV, 2 * C_GATE_RANK, C_HEADS * C_DV,
             3 * D_CH,
             N_BRANCH * D_MODEL)
IN_OFFSETS = tuple(int(o) for o in np.cumsum(IN_SPLITS)[:-1])
IN_WIDTH = int(sum(IN_SPLITS))

kernel_name = 'hybrid_flow_backbone'


def rms_norm(x, g):
    xf = x.astype(jnp.float32)
    y = xf * lax.rsqrt(jnp.mean(xf * xf, axis=-1, keepdims=True) + NORM_EPS)
    return (y * g.astype(jnp.float32)).astype(x.dtype)


def split_heads(t, n_heads):
    return t.reshape(t.shape[:-1] + (n_heads, t.shape[-1] // n_heads))


def merge_heads(t):
    return t.reshape(t.shape[:-2] + (t.shape[-2] * t.shape[-1],))


def dwconv3(u, w, b):
    up = jnp.pad(u, ((0, 0), (1, 1), (0, 0)))
    return up[:, :-2] * w[0] + up[:, 1:-1] * w[1] + up[:, 2:] * w[2] + b


def axial_rope_tables(length):
    pos = jnp.arange(length, dtype=jnp.int32)
    axes = jnp.stack([pos // GRID_W, pos % GRID_W], axis=-1).astype(jnp.float32)
    quarter = HEAD_DIM // 4
    inv_freq = ROPE_THETA ** (-jnp.arange(quarter, dtype=jnp.float32) / quarter)
    ang = axes[:, :, None] * inv_freq
    return jnp.cos(ang), jnp.sin(ang)


def apply_axial_rope(x, cos, sin):
    xs = x.reshape(x.shape[:-1] + (2, 2, HEAD_DIM // 4))
    c = cos[None, :, None].astype(x.dtype)
    s = sin[None, :, None].astype(x.dtype)
    x1, x2 = xs[..., 0, :], xs[..., 1, :]
    return jnp.stack([x1 * c - x2 * s, x2 * c + x1 * s], axis=-2).reshape(x.shape)


def gqa_sdpa(q, k, v):
    b, lq, h, d = q.shape
    kvh = k.shape[2]
    qg = q.reshape(b, lq, kvh, h // kvh, d)
    s = jnp.einsum('bqkgd,bskd->bkgqs', qg, k).astype(jnp.float32) * (d ** -0.5)
    p = jax.nn.softmax(s, axis=-1).astype(v.dtype)
    o = jnp.einsum('bkgqs,bskd->bqkgd', p, v)
    return o.reshape(b, lq, h * d)


def mixer_gqa(q_l, k_l, v_l, q_c, k_c, v_c, qn_g, kn_g, ctx_out):
    b, length = q_l.shape[:2]
    q_l = rms_norm(split_heads(q_l, A_HEADS), qn_g)
    k_l = rms_norm(split_heads(k_l, A_KV_HEADS), kn_g)
    v_l = split_heads(v_l, A_KV_HEADS)
    q_c = rms_norm(split_heads(q_c, A_HEADS), qn_g)
    k_c = rms_norm(split_heads(k_c, A_KV_HEADS), kn_g)
    v_c = split_heads(v_c, A_KV_HEADS)
    cos, sin = axial_rope_tables(length)
    q_l = apply_axial_rope(q_l, cos, sin)
    k_l = apply_axial_rope(k_l, cos, sin)
    keys = jnp.concatenate([k_c, k_l], axis=1)
    vals = jnp.concatenate([v_c, v_l], axis=1)
    nb = length // Q_BLOCK
    qb = q_l.reshape(b, nb, Q_BLOCK, A_HEADS, HEAD_DIM).swapaxes(0, 1)
    o = lax.map(lambda qblk: gqa_sdpa(qblk, keys, vals), qb)
    y_l = o.swapaxes(0, 1).reshape(b, length, A_HEADS * HEAD_DIM)
    y_c = gqa_sdpa(q_c, k_c, v_c) if ctx_out else None
    return y_l, y_c


def neighbourhood_attention(q, k, v, k_c, v_c, rpb):
    b, length, h, d = q.shape
    rows = length // GRID_W
    wr = min(NA_ROWS, rows)
    qg = q.reshape(b, rows, GRID_W, h, d)
    kg = k.reshape(b, rows, GRID_W, h, d)
    vg = v.reshape(b, rows, GRID_W, h, d)
    col = np.arange(GRID_W)
    c_start = np.clip(col - NA_COLS // 2, 0, GRID_W - NA_COLS)
    col_idx = c_start[:, None] + np.arange(NA_COLS)[None, :]
    dc = col_idx - col[:, None] + (NA_COLS - 1)
    bias_cols = rpb[:, :, dc].astype(jnp.float32)
    scale = HEAD_DIM ** -0.5

    def row_block(r):
        rs = jnp.clip(r - wr // 2, 0, rows - wr)
        kw = lax.dynamic_slice_in_dim(kg, rs, wr, axis=1)[:, :, col_idx]
        vw = lax.dynamic_slice_in_dim(vg, rs, wr, axis=1)[:, :, col_idx]
        qr = lax.dynamic_index_in_dim(qg, r, axis=1, keepdims=False)
        dr = rs + jnp.arange(wr) - r + (NA_ROWS - 1)
        bias = jnp.take(bias_cols, dr, axis=1).transpose(0, 2, 1, 3)
        s_loc = jnp.einsum('bqhd,brqjhd->bhqrj', qr, kw).astype(jnp.float32) * scale + bias[None]
        s_ctx = jnp.einsum('bqhd,bchd->bhqc', qr, k_c).astype(jnp.float32) * scale
        s = jnp.concatenate([s_loc.reshape(b, h, GRID_W, wr * NA_COLS), s_ctx], axis=-1)
        p = jax.nn.softmax(s, axis=-1).astype(v.dtype)
        p_loc = p[..., :wr * NA_COLS].reshape(b, h, GRID_W, wr, NA_COLS)
        o = (jnp.einsum('bhqrj,brqjhd->bqhd', p_loc, vw)
             + jnp.einsum('bhqc,bchd->bqhd', p[..., wr * NA_COLS:], v_c))
        return o.reshape(b, GRID_W, h * d)

    out = lax.map(row_block, jnp.arange(rows))
    return out.transpose(1, 0, 2, 3).reshape(b, length, h * d)


def mixer_neighbourhood(q_l, k_l, v_l, q_c, k_c, v_c, rpb, ctx_out):
    k_c = split_heads(k_c, B_HEADS)
    v_c = split_heads(v_c, B_HEADS)
    y_l = neighbourhood_attention(split_heads(q_l, B_HEADS), split_heads(k_l, B_HEADS),
                                  split_heads(v_l, B_HEADS), k_c, v_c, rpb)
    y_c = gqa_sdpa(split_heads(q_c, B_HEADS), k_c, v_c) if ctx_out else None
    return y_l, y_c


def gla_chunked(q, k, v, log_a, s0):
    b, length, h, _ = q.shape
    dv = v.shape[-1]
    n = length // C_CHUNK
    mask = np.tril(np.ones((C_CHUNK, C_CHUNK), dtype=bool))[:, :, None]

    def to_chunks(t):
        return t.reshape(b, n, C_CHUNK, h, t.shape[-1]).transpose(1, 0, 3, 2, 4)

    def step(state, xs):
        qc, kc, vc, gc = xs
        bcum = jnp.cumsum(gc.astype(jnp.float32), axis=2)
        o_inter = jnp.einsum('bhcd,bhde->bhce', qc * jnp.exp(bcum), state)
        diff = bcum[:, :, :, None, :] - bcum[:, :, None, :, :]
        decay = jnp.exp(jnp.where(mask, diff, -jnp.inf))
        att = jnp.einsum('bhid,bhjd,bhijd->bhij', qc, kc, decay)
        o = o_inter + jnp.einsum('bhij,bhje->bhie', att, vc)
        blast = bcum[:, :, -1:, :]
        new_state = (jnp.exp(blast[:, :, 0, :, None]) * state
                     + jnp.einsum('bhjd,bhje->bhde', kc * jnp.exp(blast - bcum), vc))
        return new_state, o

    s_fin, o = lax.scan(step, s0, (to_chunks(q), to_chunks(k), to_chunks(v), to_chunks(log_a)))
    o = o.transpose(1, 0, 3, 2, 4).reshape(b, length, h, dv)
    return o.astype(v.dtype), s_fin


def mixer_gla(q_l, k_l, v_l, a_l, g_l, q_c, k_c, v_c, a_c, g_c, gate_w2, gate_b, norm_g, ctx_out):
    def prep(q, k, v, a):
        q = split_heads(q, C_HEADS) * (C_DK ** -0.5)
        k = split_heads(k, C_HEADS)
        v = split_heads(v, C_HEADS)
        a = a.reshape(a.shape[:-1] + (2, C_GATE_RANK))
        la = jax.nn.log_sigmoid((jnp.einsum('blnr,nrk->blnk', a, gate_w2) + gate_b).astype(jnp.float32)) / C_GATE_TAU
        la = split_heads(la, C_HEADS)
        return q, k, v, la[:, :, 0], la[:, :, 1]

    def flip(t):
        return jnp.flip(t, axis=1)

    qc, kc, vc, fwd_c, bwd_c = prep(q_c, k_c, v_c, a_c)
    ql, kl, vl, fwd_l, bwd_l = prep(q_l, k_l, v_l, a_l)
    s0 = jnp.zeros((q_l.shape[0], C_HEADS, C_DK, C_DV), jnp.float32)
    o_cf, s_f = gla_chunked(qc, kc, vc, fwd_c, s0)
    o_cb, s_b = gla_chunked(flip(qc), flip(kc), flip(vc), flip(bwd_c), s0)
    o_lf, _ = gla_chunked(ql, kl, vl, fwd_l, s_f)
    o_lb, _ = gla_chunked(flip(ql), flip(kl), flip(vl), flip(bwd_l), s_b)

    def head_out(o, g):
        return merge_heads(rms_norm(o, norm_g)) * jax.nn.silu(g)

    y_l = head_out(o_lf + flip(o_lb), g_l)
    y_c = head_out(o_cf + flip(o_cb), g_c) if ctx_out else None
    return y_l, y_c


def hyena_filter(length, w1, b1, w2, b2, w3, b3, w4, freq):
    t = jnp.linspace(0.0, 1.0, length, dtype=jnp.float32)[:, None]
    bands = (HY_EMB - 1) // 2
    w = 2.0 * math.pi * jnp.arange(length, dtype=jnp.float32)[:, None] / length
    f = jnp.linspace(1e-4, bands - 1, bands, dtype=jnp.float32)
    z = jnp.concatenate([t, jnp.cos(f * w), -jnp.sin(f * w)], axis=-1)
    hid = jnp.sin(freq * (z @ w1 + b1))
    hid = jnp.sin(freq * (hid @ w2 + b2))
    hid = jnp.sin(freq * (hid @ w3 + b3))
    h = (hid @ w4).astype(jnp.float32).reshape(length, 2, D_CH)
    deltas = jnp.abs(jnp.linspace(HY_DECAY_MIN, HY_DECAY_MAX, D_CH, dtype=jnp.float32))
    h = h * jnp.exp(-t * deltas)[:, None, :]
    return h / jnp.sum(jnp.abs(h), axis=(0, 1), keepdims=True)


def long_conv_bidir(u, h):
    length = u.shape[1]
    ch = h.shape[-1]
    k2 = jnp.concatenate([h[:, 0], jnp.zeros((1, ch), h.dtype), h[:0:-1, 1]], axis=0)
    kf = jnp.fft.rfft(k2, axis=0)
    uf = jnp.fft.rfft(u.astype(jnp.float32), n=2 * length, axis=1)
    y = jnp.fft.irfft(uf * kf, n=2 * length, axis=1)[:, :length]
    return y.astype(u.dtype)


def mixer_hyena(u_l, u_c, conv_w, conv_b, w1, b1, w2, b2, w3, b3, w4, freq, bias, ctx_out):
    def run(u):
        x0, x1, v = jnp.split(dwconv3(u, conv_w, conv_b), 3, axis=-1)
        z = v * x1
        filt = hyena_filter(u.shape[1], w1, b1, w2, b2, w3, b3, w4, freq)
        return x0 * (long_conv_bidir(z, filt) + z * bias)

    y_l = run(u_l)
    y_c = run(u_c) if ctx_out else None
    return y_l, y_c


def merge_branches(ys, gates, branch_w, w_out):
    g = jax.nn.sigmoid(gates.reshape(gates.shape[:-1] + (N_BRANCH, D_MODEL)))
    merged = g[..., 0, :] * (ys[0] @ branch_w[0])
    for i in range(1, N_BRANCH):
        merged = merged + g[..., i, :] * (ys[i] @ branch_w[i])
    return merged @ w_out


def token_mixing(p_l, p_c, a_qn_g, a_kn_g, b_rpb, c_gate_w2, c_gate_b, c_norm_g,
                 d_conv_w, d_conv_b, d_ffn_w1, d_ffn_b1, d_ffn_w2, d_ffn_b2, d_ffn_w3, d_ffn_b3,
                 d_ffn_w4, d_sin_freq, d_bias, branch_w, w_out, ctx_out):
    aq_l, ak_l, av_l, bq_l, bk_l, bv_l, cq_l, ck_l, cv_l, ca_l, cg_l, du_l, gate_l = jnp.split(p_l, IN_OFFSETS, axis=-1)
    aq_c, ak_c, av_c, bq_c, bk_c, bv_c, cq_c, ck_c, cv_c, ca_c, cg_c, du_c, gate_c = jnp.split(p_c, IN_OFFSETS, axis=-1)
    ya_l, ya_c = mixer_gqa(aq_l, ak_l, av_l, aq_c, ak_c, av_c, a_qn_g, a_kn_g, ctx_out)
    yb_l, yb_c = mixer_neighbourhood(bq_l, bk_l, bv_l, bq_c, bk_c, bv_c, b_rpb, ctx_out)
    yc_l, yc_c = mixer_gla(cq_l, ck_l, cv_l, ca_l, cg_l, cq_c, ck_c, cv_c, ca_c, cg_c,
                           c_gate_w2, c_gate_b, c_norm_g, ctx_out)
    yd_l, yd_c = mixer_hyena(du_l, du_c, d_conv_w, d_conv_b, d_ffn_w1, d_ffn_b1, d_ffn_w2, d_ffn_b2,
                             d_ffn_w3, d_ffn_b3, d_ffn_w4, d_sin_freq, d_bias, ctx_out)
    y_l = merge_branches((ya_l, yb_l, yc_l, yd_l), gate_l, branch_w, w_out)
    y_c = merge_branches((ya_c, yb_c, yc_c, yd_c), gate_c, branch_w, w_out) if ctx_out else None
    return y_l, y_c


def conv_ffn(n, w_up, conv_w, conv_b, w_down):
    u = dwconv3(n @ w_up, conv_w, conv_b)
    a, g = jnp.split(u, 2, axis=-1)
    return (jax.nn.silu(g) * a) @ w_down


def modulate(h, g, shift, scale):
    return rms_norm(h, g) * (1.0 + scale) + shift


def setup_inputs(seed: int = 0) -> dict:
    key = jax.random.key(seed)
    ks = iter(jax.random.split(key, 48))

    def nrm(shape, scale):
        return scale * jax.random.normal(next(ks), shape, jnp.float32)

    def gain(shape):
        return 1.0 + nrm(shape, 0.05)

    L = DEPTH
    return {
        'x': nrm((BATCH, SEQ, D_MODEL), 1.0),
        'c': nrm((BATCH, D_MODEL), 1.0),
        'ctx': nrm((BATCH, CTX_LEN, D_MODEL), 1.0),
        'c_ctx': nrm((D_MODEL,), 1.0),
        'ada_w': nrm((L, D_MODEL, 6 * D_MODEL), 0.5 * D_MODEL ** -0.5),
        'ada_b': nrm((L, 6 * D_MODEL), 0.01),
        'norm1_g': gain((L, D_MODEL)),
        'norm2_g': gain((L, D_MODEL)),
        'w_in': nrm((L, D_MODEL, IN_WIDTH), D_MODEL ** -0.5),
        'a_qn_g': gain((L, HEAD_DIM)),
        'a_kn_g': gain((L, HEAD_DIM)),
        'b_rpb': nrm((L, B_HEADS, 2 * NA_ROWS - 1, 2 * NA_COLS - 1), 0.1),
        'c_gate_w2': nrm((L, 2, C_GATE_RANK, C_HEADS * C_DK), C_GATE_RANK ** -0.5),
        'c_gate_b': nrm((L, 2, C_HEADS * C_DK), 0.1),
        'c_norm_g': gain((L, C_DV)),
        'd_conv_w': nrm((L, 3, 3 * D_CH), 3 ** -0.5),
        'd_conv_b': nrm((L, 3 * D_CH), 0.01),
        'd_ffn_w1': nrm((L, HY_EMB, HY_FFN), HY_EMB ** -0.5),
        'd_ffn_b1': nrm((L, HY_FFN), 0.01),
        'd_ffn_w2': nrm((L, HY_FFN, HY_FFN), HY_FFN ** -0.5),
        'd_ffn_b2': nrm((L, HY_FFN), 0.01),
        'd_ffn_w3': nrm((L, HY_FFN, HY_FFN), HY_FFN ** -0.5),
        'd_ffn_b3': nrm((L, HY_FFN), 0.01),
        'd_ffn_w4': nrm((L, HY_FFN, 2 * D_CH), HY_FFN ** -0.5),
        'd_sin_freq': gain((L, HY_FFN)),
        'd_bias': nrm((L, D_CH), 0.5),
        'branch_w': nrm((L, N_BRANCH, MIX_W, D_MODEL), MIX_W ** -0.5),
        'w_out': nrm((L, D_MODEL, D_MODEL), D_MODEL ** -0.5),
        'ffn_up': nrm((L, D_MODEL, 2 * D_FF), D_MODEL ** -0.5),
        'ffn_conv_w': nrm((L, 3, 2 * D_FF), 3 ** -0.5),
        'ffn_conv_b': nrm((L, 2 * D_FF), 0.01),
        'ffn_down': nrm((L, D_FF, D_MODEL), D_FF ** -0.5),
        'final_norm_g': gain((D_MODEL,)),
    }


def reference(x, c, ctx, c_ctx, ada_w, ada_b, norm1_g, norm2_g, w_in, a_qn_g, a_kn_g, b_rpb,
              c_gate_w2, c_gate_b, c_norm_g, d_conv_w, d_conv_b, d_ffn_w1, d_ffn_b1, d_ffn_w2,
              d_ffn_b2, d_ffn_w3, d_ffn_b3, d_ffn_w4, d_sin_freq, d_bias, branch_w, w_out,
              ffn_up, ffn_conv_w, ffn_conv_b, ffn_down, final_norm_g):
    h_l, h_c = x, ctx
    for i in range(DEPTH):
        ctx_out = i < DEPTH - 1
        mod_l = (jax.nn.silu(c) @ ada_w[i] + ada_b[i])[:, None, :]
        mod_c = jax.nn.silu(c_ctx) @ ada_w[i] + ada_b[i]
        sh1_l, sc1_l, g1_l, sh2_l, sc2_l, g2_l = jnp.split(mod_l, 6, axis=-1)
        sh1_c, sc1_c, g1_c, sh2_c, sc2_c, g2_c = jnp.split(mod_c, 6, axis=-1)
        p_l = modulate(h_l, norm1_g[i], sh1_l, sc1_l) @ w_in[i]
        p_c = modulate(h_c, norm1_g[i], sh1_c, sc1_c) @ w_in[i]
        mix_l, mix_c = token_mixing(p_l, p_c, a_qn_g[i], a_kn_g[i], b_rpb[i], c_gate_w2[i], c_gate_b[i],
                                    c_norm_g[i], d_conv_w[i], d_conv_b[i], d_ffn_w1[i], d_ffn_b1[i],
                                    d_ffn_w2[i], d_ffn_b2[i], d_ffn_w3[i], d_ffn_b3[i], d_ffn_w4[i],
                                    d_sin_freq[i], d_bias[i], branch_w[i], w_out[i], ctx_out)
        h_l = h_l + g1_l * mix_l
        h_l = h_l + g2_l * conv_ffn(modulate(h_l, norm2_g[i], sh2_l, sc2_l),
                                    ffn_up[i], ffn_conv_w[i], ffn_conv_b[i], ffn_down[i])
        if ctx_out:
            h_c = h_c + g1_c * mix_c
            h_c = h_c + g2_c * conv_ffn(modulate(h_c, norm2_g[i], sh2_c, sc2_c),
                                        ffn_up[i], ffn_conv_w[i], ffn_conv_b[i], ffn_down[i])
    return rms_norm(h_l, final_norm_g)
```

```python
import functools
import math

import numpy as np
import jax
import jax.numpy as jnp
from jax import lax
from jax.experimental import pallas as pl
from jax.experimental.pallas import tpu as pltpu

F32 = jnp.float32
BF16 = jnp.bfloat16

GRID_W = 64
HEAD_DIM = 128
A_HEADS = 4
A_KV_HEADS = 2
ROPE_THETA = 10000.0
B_HEADS = 4
NA_ROWS = 8
NA_COLS = 16
C_HEADS = 4
C_DK = 64
C_DV = 128
C_GATE_RANK = 16
C_GATE_TAU = 16.0
C_CHUNK = 64
MIX_W = 512
N_BRANCH = 4
HY_EMB = 33
HY_DECAY_MIN = math.log(1e-2) / 1.5
HY_DECAY_MAX = math.log(1e-2) / 0.3
NORM_EPS = 1e-6
NEG_BIG = -1e30

P_AQ, P_AK, P_AV = 0, 512, 768
P_BQ, P_BK, P_BV = 1024, 1536, 2048
P_CQ, P_CK, P_CV, P_CG = 2560, 2816, 3072, 3584
P_DU = 4096
P_WIDTH = 5632

VMEM_LIMIT = 56 * 1024 * 1024


def _cparams(*sem):
    return pltpu.CompilerParams(dimension_semantics=sem, vmem_limit_bytes=VMEM_LIMIT)


def _pick(n, candidates):
    for c in candidates:
        if n % c == 0:
            return c
    return n


def _modnorm_kernel(h_ref, g_ref, sh_ref, sc_ref, o_ref):
    x = h_ref[...]
    y = x * lax.rsqrt(jnp.mean(x * x, axis=-1, keepdims=True) + NORM_EPS) * g_ref[...]
    o_ref[...] = (y * (1.0 + sc_ref[...]) + sh_ref[...]).astype(o_ref.dtype)


def mod_norm(h, g, shift2, scale2, n_latent, out_dtype, n_rows=None):
    d = h.shape[1]
    tb = _pick(math.gcd(n_latent, h.shape[0]), (256, 128, 64, 8))
    m = n_rows or h.shape[0]
    nl = n_latent // tb
    sel = lambda i: (jnp.where(i >= nl, 1, 0), 0, 0)
    return pl.pallas_call(
        _modnorm_kernel,
        grid=(m // tb,),
        in_specs=[pl.BlockSpec((tb, d), lambda i: (i, 0)),
                  pl.BlockSpec((1, d), lambda i: (0, 0)),
                  pl.BlockSpec((None, 1, d), sel),
                  pl.BlockSpec((None, 1, d), sel)],
        out_specs=pl.BlockSpec((tb, d), lambda i: (i, 0)),
        out_shape=jax.ShapeDtypeStruct((m, d), out_dtype),
        compiler_params=_cparams("parallel"),
        name="mod_norm",
    )(h, g.reshape(1, d), shift2.reshape(2, 1, d), scale2.reshape(2, 1, d))


def _mm_kernel(a_ref, w_ref, o_ref):
    o_ref[...] = jnp.dot(a_ref[...], w_ref[...], preferred_element_type=F32).astype(o_ref.dtype)


def _mm_bias_kernel(a_ref, w_ref, b_ref, o_ref):
    o_ref[...] = (jnp.dot(a_ref[...], w_ref[...], preferred_element_type=F32) + b_ref[...]).astype(o_ref.dtype)


def matmul(a, w, out_dtype, bias=None, tm=None, tn=None):
    m, k = a.shape
    n = w.shape[1]
    tm = tm or _pick(m, (1280, 1024, 512, 256, 128, 8))
    tn = tn or _pick(n, (512, 256, 128))
    in_specs = [pl.BlockSpec((tm, k), lambda i, j: (i, 0)),
                pl.BlockSpec((k, tn), lambda i, j: (0, j))]
    args = [a, w]
    kern = _mm_kernel
    if bias is not None:
        in_specs.append(pl.BlockSpec((1, tn), lambda i, j: (0, j)))
        args.append(bias.reshape(1, n))
        kern = _mm_bias_kernel
    return pl.pallas_call(
        kern,
        grid=(m // tm, n // tn),
        in_specs=in_specs,
        out_specs=pl.BlockSpec((tm, tn), lambda i, j: (i, j)),
        out_shape=jax.ShapeDtypeStruct((m, n), out_dtype),
        compiler_params=_cparams("parallel", "parallel"),
        name="matmul",
    )(*args)


def _mm_res_kernel(a_ref, w_ref, h_ref, g_ref, o_ref, acc_ref, *, n_latent, tm, nk):
    kk = pl.program_id(2)

    @pl.when(kk == 0)
    def _():
        acc_ref[...] = jnp.zeros_like(acc_ref)

    acc_ref[...] += jnp.dot(a_ref[...], w_ref[...], preferred_element_type=F32)

    @pl.when(kk == nk - 1)
    def _():
        row = pl.program_id(0) * tm + lax.broadcasted_iota(jnp.int32, (tm, 1), 0)
        gate = jnp.where(row < n_latent, g_ref[0], g_ref[1])
        o_ref[...] = h_ref[...] + gate * acc_ref[...]


def matmul_residual(a, w, h, gate2, n_latent, tk=None):
    m, k = a.shape
    n = w.shape[1]
    tm = _pick(m, (1280, 1024, 512, 256, 128))
    tn = _pick(n, (512, 256, 128))
    tk = tk or _pick(k, (2048, 1408, 1024, 512, 256, 128))
    nk = k // tk
    return pl.pallas_call(
        functools.partial(_mm_res_kernel, n_latent=n_latent, tm=tm, nk=nk),
        grid=(m // tm, n // tn, nk),
        in_specs=[pl.BlockSpec((tm, tk), lambda i, j, kk: (i, kk)),
                  pl.BlockSpec((tk, tn), lambda i, j, kk: (kk, j)),
                  pl.BlockSpec((tm, tn), lambda i, j, kk: (i, j)),
                  pl.BlockSpec((2, 1, tn), lambda i, j, kk: (0, 0, j))],
        out_specs=pl.BlockSpec((tm, tn), lambda i, j, kk: (i, j)),
        out_shape=jax.ShapeDtypeStruct((m, n), F32),
        scratch_shapes=[pltpu.VMEM((tm, tn), F32)],
        compiler_params=_cparams("parallel", "parallel", "arbitrary"),
        name="matmul_residual",
    )(a, w, h, gate2.reshape(2, 1, n))


def _swap_halves(y):
    lane = lax.broadcasted_iota(jnp.int32, y.shape, 1)
    return jnp.where((lane % 64) < 32, pltpu.roll(y, 96, 1), pltpu.roll(y, 32, 1))


def _aprep_kernel(p_ref, cos_ref, sin_ref, qg_ref, kg_ref, q_ref, k_ref, v_ref):
    c = cos_ref[...]
    s = sin_ref[...]

    def norm_rope(x, g):
        y = x * lax.rsqrt(jnp.mean(x * x, axis=-1, keepdims=True) + NORM_EPS) * g
        return y * c + _swap_halves(y) * s

    for hd in range(A_HEADS):
        x = p_ref[:, hd * HEAD_DIM:(hd + 1) * HEAD_DIM]
        q_ref[:, hd * HEAD_DIM:(hd + 1) * HEAD_DIM] = (
            norm_rope(x, qg_ref[...]) * (HEAD_DIM ** -0.5)).astype(q_ref.dtype)
    for hd in range(A_KV_HEADS):
        x = p_ref[:, P_AK + hd * HEAD_DIM:P_AK + (hd + 1) * HEAD_DIM]
        k_ref[:, hd * HEAD_DIM:(hd + 1) * HEAD_DIM] = norm_rope(x, kg_ref[...]).astype(k_ref.dtype)
    v_ref[...] = p_ref[:, P_AV:P_AV + A_KV_HEADS * HEAD_DIM].astype(v_ref.dtype)


def rope_tables(n_latent, n_ctx):
    pos = np.arange(n_latent)
    axes = np.stack([pos // GRID_W, pos % GRID_W], axis=-1).astype(np.float32)
    quarter = HEAD_DIM // 4
    inv_freq = (np.float32(ROPE_THETA) ** (-np.arange(quarter, dtype=np.float32) / quarter)).astype(np.float32)
    ang = (axes[:, :, None] * inv_freq).astype(np.float32)
    cos, sin = np.cos(ang), np.sin(ang)
    ctab = np.concatenate([cos[:, 0], cos[:, 0], cos[:, 1], cos[:, 1]], axis=-1)
    stab = np.concatenate([-sin[:, 0], sin[:, 0], -sin[:, 1], sin[:, 1]], axis=-1)
    ctab = np.concatenate([ctab, np.ones((n_ctx, HEAD_DIM), np.float32)], axis=0)
    stab = np.concatenate([stab, np.zeros((n_ctx, HEAD_DIM), np.float32)], axis=0)
    return jnp.asarray(ctab, F32), jnp.asarray(stab, F32)


def a_prep(p1, ctab, stab, qn_g, kn_g):
    m = p1.shape[0]
    tb = _pick(m, (256, 128, 64, 8))
    wa = 1024
    return pl.pallas_call(
        _aprep_kernel,
        grid=(m // tb,),
        in_specs=[pl.BlockSpec((tb, wa), lambda i: (i, 0)),
                  pl.BlockSpec((tb, HEAD_DIM), lambda i: (i, 0)),
                  pl.BlockSpec((tb, HEAD_DIM), lambda i: (i, 0)),
                  pl.BlockSpec((1, HEAD_DIM), lambda i: (0, 0)),
                  pl.BlockSpec((1, HEAD_DIM), lambda i: (0, 0))],
        out_specs=[pl.BlockSpec((tb, 512), lambda i: (i, 0)),
                   pl.BlockSpec((tb, 256), lambda i: (i, 0)),
                   pl.BlockSpec((tb, 256), lambda i: (i, 0))],
        out_shape=[jax.ShapeDtypeStruct((m, 512), BF16),
                   jax.ShapeDtypeStruct((m, 256), BF16),
                   jax.ShapeDtypeStruct((m, 256), BF16)],
        compiler_params=_cparams("parallel"),
        name="a_prep",
    )(p1, ctab, stab, qn_g.reshape(1, HEAD_DIM), kn_g.reshape(1, HEAD_DIM))


def _flash_kernel(q_ref, k_ref, v_ref, o_ref, m_ref, l_ref, acc_ref, *, group, nk):
    j = pl.program_id(2)

    @pl.when(j == 0)
    def _():
        m_ref[...] = jnp.full_like(m_ref, NEG_BIG)
        l_ref[...] = jnp.zeros_like(l_ref)
        acc_ref[...] = jnp.zeros_like(acc_ref)

    k = k_ref[...]
    v = v_ref[...]
    for g in range(group):
        q = q_ref[:, g * HEAD_DIM:(g + 1) * HEAD_DIM]
        s = lax.dot_general(q, k, (((1,), (1,)), ((), ())), preferred_element_type=F32)
        m_prev = m_ref[g]
        m_new = jnp.maximum(m_prev, jnp.max(s, axis=-1, keepdims=True))
        p = jnp.exp(s - m_new)
        alpha = jnp.exp(m_prev - m_new)
        l_ref[g] = alpha * l_ref[g] + jnp.sum(p, axis=-1, keepdims=True)
        acc_ref[g] = alpha * acc_ref[g] + jnp.dot(p.astype(BF16), v, preferred_element_type=F32)
        m_ref[g] = m_new

    @pl.when(j == nk - 1)
    def _():
        for g in range(group):
            o_ref[:, g * HEAD_DIM:(g + 1) * HEAD_DIM] = (acc_ref[g] / l_ref[g]).astype(o_ref.dtype)


def flash_attention(q, k, v, kv_heads, tq=None, tk=None):
    lq, hq = q.shape
    lk = k.shape[0]
    group = hq // (kv_heads * HEAD_DIM)
    tq = tq or _pick(lq, (512, 256, 128))
    tk = tk or _pick(lk, (1280, 1024, 512, 256, 128))
    nk = lk // tk
    gw = group * HEAD_DIM
    return pl.pallas_call(
        functools.partial(_flash_kernel, group=group, nk=nk),
        grid=(kv_heads, lq // tq, nk),
        in_specs=[pl.BlockSpec((tq, gw), lambda h, i, j: (i, h)),
                  pl.BlockSpec((tk, HEAD_DIM), lambda h, i, j: (j, h)),
                  pl.BlockSpec((tk, HEAD_DIM), lambda h, i, j: (j, h))],
        out_specs=pl.BlockSpec((tq, gw), lambda h, i, j: (i, h)),
        out_shape=jax.ShapeDtypeStruct((lq, hq), BF16),
        scratch_shapes=[pltpu.VMEM((group, tq, 1), F32),
                        pltpu.VMEM((group, tq, 1), F32),
                        pltpu.VMEM((group, tq, HEAD_DIM), F32)],
        compiler_params=_cparams("parallel", "parallel", "arbitrary"),
        name="flash_attention",
    )(q, k, v)


def na_bias_table(rpb):
    col = np.arange(GRID_W)
    c_start = np.clip(col - NA_COLS // 2, 0, GRID_W - NA_COLS)
    kc = np.arange(GRID_W)
    inside = (kc[None, :] >= c_start[:, None]) & (kc[None, :] < c_start[:, None] + NA_COLS)
    dc = np.clip(kc[None, :] - col[:, None] + (NA_COLS - 1), 0, 2 * NA_COLS - 2)
    off = np.arange(NA_ROWS)
    i = np.arange(NA_ROWS)
    dr = np.clip(i[None, :] - off[:, None] + (NA_ROWS - 1), 0, 2 * NA_ROWS - 2)
    tab = rpb.astype(F32)[:, dr][:, :, :, dc]
    tab = jnp.where(jnp.asarray(inside)[None, None, None], tab, NEG_BIG)
    tab = tab.transpose(0, 1, 3, 2, 4).reshape(rpb.shape[0], NA_ROWS, GRID_W, NA_ROWS * GRID_W)
    return tab


def _na_kernel(*refs):
    q_ref = refs[0]
    k_refs = refs[1:1 + NA_ROWS]
    v_refs = refs[1 + NA_ROWS:1 + 2 * NA_ROWS]
    kc_ref, vc_ref, b_ref, o_ref = refs[1 + 2 * NA_ROWS:]
    scale = HEAD_DIM ** -0.5
    k_all = jnp.concatenate([r[...].astype(BF16) for r in k_refs], axis=0)
    v_all = jnp.concatenate([r[...].astype(BF16) for r in v_refs], axis=0)
    k_c = kc_ref[...].astype(BF16)
    v_c = vc_ref[...].astype(BF16)
    nt = (((1,), (1,)), ((), ()))
    for hd in range(B_HEADS):
        sl = slice(hd * HEAD_DIM, (hd + 1) * HEAD_DIM)
        q = (q_ref[:, sl] * scale).astype(BF16)
        s_loc = lax.dot_general(q, k_all[:, sl], nt, preferred_element_type=F32) + b_ref[hd]
        s_ctx = lax.dot_general(q, k_c[:, sl], nt, preferred_element_type=F32)
        mx = jnp.maximum(jnp.max(s_loc, axis=-1, keepdims=True), jnp.max(s_ctx, axis=-1, keepdims=True))
        p_loc = jnp.exp(s_loc - mx)
        p_ctx = jnp.exp(s_ctx - mx)
        den = jnp.sum(p_loc, axis=-1, keepdims=True) + jnp.sum(p_ctx, axis=-1, keepdims=True)
        o = (jnp.dot(p_loc.astype(BF16), v_all[:, sl], preferred_element_type=F32)
             + jnp.dot(p_ctx.astype(BF16), v_c[:, sl], preferred_element_type=F32))
        o_ref[:, sl] = (o / den).astype(o_ref.dtype)


def neighbourhood_attention(p1, bias_tab, n_latent, n_ctx):
    rows = n_latent // GRID_W
    wr = NA_ROWS
    assert rows >= wr
    w = B_HEADS * HEAD_DIM
    cq, ck, cv = P_BQ // w, P_BK // w, P_BV // w

    def rstart(r):
        return jnp.clip(r - wr // 2, 0, rows - wr)

    k_specs = [pl.BlockSpec((GRID_W, w), functools.partial(lambda r, i: (rstart(r) + i, ck), i=i))
               for i in range(wr)]
    v_specs = [pl.BlockSpec((GRID_W, w), functools.partial(lambda r, i: (rstart(r) + i, cv), i=i))
               for i in range(wr)]
    cblk = n_latent // n_ctx
    return pl.pallas_call(
        _na_kernel,
        grid=(rows,),
        in_specs=[pl.BlockSpec((GRID_W, w), lambda r: (r, cq))] + k_specs + v_specs + [
            pl.BlockSpec((n_ctx, w), lambda r: (cblk, ck)),
            pl.BlockSpec((n_ctx, w), lambda r: (cblk, cv)),
            pl.BlockSpec((B_HEADS, None, GRID_W, wr * GRID_W), lambda r: (0, r - rstart(r), 0, 0))],
        out_specs=pl.BlockSpec((GRID_W, w), lambda r: (r, 0)),
        out_shape=jax.ShapeDtypeStruct((n_latent, w), BF16),
        compiler_params=_cparams("parallel"),
        name="neighbourhood_attention",
    )(*([p1] * (1 + 2 * wr + 2)), bias_tab)


def _gla_chunk(q_ref, k_ref, v_ref, a_ref, w2_ref, gb_ref, st_ref, reverse):
    c = C_CHUNK
    x = jnp.dot(a_ref[...].astype(BF16), w2_ref[...], preferred_element_type=F32) + gb_ref[...]
    la = (jnp.minimum(x, 0.0) - jnp.log(1.0 + jnp.exp(-jnp.abs(x)))) * (1.0 / C_GATE_TAU)
    row = lax.broadcasted_iota(jnp.int32, (c, C_HEADS * C_DK), 0)
    b = la
    sh = 1
    while sh < c:
        if reverse:
            b = b + jnp.where(row < c - sh, pltpu.roll(b, c - sh, 0), 0.0)
        else:
            b = b + jnp.where(row >= sh, pltpu.roll(b, sh, 0), 0.0)
        sh *= 2
    b_mid = b[c // 2:c // 2 + 1, :]
    b_last = b[0:1, :] if reverse else b[c - 1:c, :]
    qs = q_ref[...] * (C_DK ** -0.5)
    kk = k_ref[...]
    q_in = (qs * jnp.exp(b)).astype(BF16)
    q_t = (qs * jnp.exp(b - b_mid)).astype(BF16)
    k_t = (kk * jnp.exp(b_mid - b)).astype(BF16)
    k_d = (kk * jnp.exp(b_last - b)).astype(BF16)
    e_last = jnp.exp(b_last)
    ii = lax.broadcasted_iota(jnp.int32, (c, c), 0)
    jj = lax.broadcasted_iota(jnp.int32, (c, c), 1)
    mask = (jj >= ii) if reverse else (jj <= ii)
    nt = (((1,), (1,)), ((), ()))
    tn = (((0,), (0,)), ((), ()))
    outs = []
    for hd in range(C_HEADS):
        ks = slice(hd * C_DK, (hd + 1) * C_DK)
        vs = slice(hd * C_DV, (hd + 1) * C_DV)
        v = v_ref[:, vs].astype(BF16)
        st = st_ref[hd]
        att = lax.dot_general(q_t[:, ks], k_t[:, ks], nt, preferred_element_type=F32)
        att = jnp.where(mask, att, 0.0).astype(BF16)
        o = (lax.dot_general(q_in[:, ks], st.astype(BF16), nt, preferred_element_type=F32)
             + jnp.dot(att, v, preferred_element_type=F32))
        st_ref[hd] = e_last[:, ks] * st + lax.dot_general(v, k_d[:, ks], tn, preferred_element_type=F32)
        outs.append(o)
    return jnp.concatenate(outs, axis=-1)


def _gla_fwd_kernel(q_ref, k_ref, v_ref, a_ref, w2_ref, gb_ref, o_ref, st_ref):
    @pl.when(pl.program_id(0) == 0)
    def _():
        st_ref[...] = jnp.zeros_like(st_ref)

    o_ref[...] = _gla_chunk(q_ref, k_ref, v_ref, a_ref, w2_ref, gb_ref, st_ref, False)


def _gla_bwd_kernel(q_ref, k_ref, v_ref, a_ref, w2_ref, gb_ref, of_ref, g_ref, ng_ref, y_ref, st_ref):
    @pl.when(pl.program_id(0) == 0)
    def _():
        st_ref[...] = jnp.zeros_like(st_ref)

    o = of_ref[...] + _gla_chunk(q_ref, k_ref, v_ref, a_ref, w2_ref, gb_ref, st_ref, True)
    g = g_ref[...]
    silu = g / (1.0 + jnp.exp(-g))
    for hd in range(C_HEADS):
        vs = slice(hd * C_DV, (hd + 1) * C_DV)
        x = o[:, vs]
        y = x * lax.rsqrt(jnp.mean(x * x, axis=-1, keepdims=True) + NORM_EPS) * ng_ref[...]
        y_ref[:, vs] = (y * silu[:, vs]).astype(y_ref.dtype)


def gla_mixer(p1, ca, gate_w2, gate_b, norm_g, n_latent, n_ctx):
    m = p1.shape[0]
    c = C_CHUNK
    nl, nc = n_latent // c, n_ctx // c
    kw = C_HEADS * C_DK
    vw = C_HEADS * C_DV

    def blk_f(s):
        return jnp.where(s < nc, nl + s, s - nc)

    def blk_r(s):
        return nl + nc - 1 - s

    w2 = jnp.zeros((2, 128, kw), F32)
    w2 = w2.at[0, 0:C_GATE_RANK].set(gate_w2[0]).at[1, C_GATE_RANK:2 * C_GATE_RANK].set(gate_w2[1]).astype(BF16)
    gb = gate_b.reshape(2, 1, kw).astype(F32)

    def common_specs(blk):
        return [pl.BlockSpec((c, kw), lambda s: (blk(s), P_CQ // kw)),
                pl.BlockSpec((c, kw), lambda s: (blk(s), P_CK // kw)),
                pl.BlockSpec((c, vw), lambda s: (blk(s), P_CV // vw)),
                pl.BlockSpec((c, 128), lambda s: (blk(s), 0))]

    st_scratch = [pltpu.VMEM((C_HEADS, C_DV, C_DK), F32)]
    o_f = pl.pallas_call(
        _gla_fwd_kernel,
        grid=(nl + nc,),
        in_specs=common_specs(blk_f) + [pl.BlockSpec((None, 128, kw), lambda s: (0, 0, 0)),
                                        pl.BlockSpec((None, 1, kw), lambda s: (0, 0, 0))],
        out_specs=pl.BlockSpec((c, vw), lambda s: (blk_f(s), 0)),
        out_shape=jax.ShapeDtypeStruct((m, vw), F32),
        scratch_shapes=st_scratch,
        compiler_params=_cparams("arbitrary"),
        name="gla_forward",
    )(p1, p1, p1, ca, w2, gb)
    y = pl.pallas_call(
        _gla_bwd_kernel,
        grid=(nl + nc,),
        in_specs=common_specs(blk_r) + [pl.BlockSpec((None, 128, kw), lambda s: (1, 0, 0)),
                                        pl.BlockSpec((None, 1, kw), lambda s: (1, 0, 0)),
                                        pl.BlockSpec((c, vw), lambda s: (blk_r(s), 0)),
                                        pl.BlockSpec((c, vw), lambda s: (blk_r(s), P_CG // vw)),
                                        pl.BlockSpec((1, C_DV), lambda s: (0, 0))],
        out_specs=pl.BlockSpec((c, vw), lambda s: (blk_r(s), 0)),
        out_shape=jax.ShapeDtypeStruct((m, vw), BF16),
        scratch_shapes=st_scratch,
        compiler_params=_cparams("arbitrary"),
        name="gla_reverse",
    )(p1, p1, p1, ca, w2, gb, o_f, p1, norm_g.reshape(1, C_DV))
    return y


def _conv3_rows(x, prev_row, next_row, w_ref, b_ref):
    t = x.shape[0]
    row = lax.broadcasted_iota(jnp.int32, x.shape, 0)
    up = jnp.where(row == 0, prev_row, pltpu.roll(x, 1, 0))
    dn = jnp.where(row == t - 1, next_row, pltpu.roll(x, t - 1, 0))
    return up * w_ref[0:1, :] + x * w_ref[1:2, :] + dn * w_ref[2:3, :] + b_ref[...]


def _hy_pre_kernel(*refs, tb, seq_starts, seq_ends):
    mains, prevs, nexts = refs[0:3], refs[3:6], refs[6:9]
    w_refs, b_refs = refs[9:12], refs[12:15]
    x0_ref, z_ref = refs[15:17]
    r0 = pl.program_id(0) * tb
    pvalid = jnp.where(functools.reduce(jnp.logical_or, [r0 == s for s in seq_starts]), 0.0, 1.0)
    nvalid = jnp.where(functools.reduce(jnp.logical_or, [r0 + tb == e for e in seq_ends]), 0.0, 1.0)
    res = []
    for t in range(3):
        res.append(_conv3_rows(mains[t][...], prevs[t][7:8, :] * pvalid, nexts[t][0:1, :] * nvalid,
                               w_refs[t], b_refs[t]))
    x0_ref[...] = res[0]
    z_ref[...] = res[2] * res[1]


def hyena_pre(p1, conv_w, conv_b, n_latent):
    m = p1.shape[0]
    w = MIX_W
    tb = _pick(math.gcd(n_latent, m), (256, 128, 64, 8))
    nb8 = m // 8
    c0 = P_DU // w
    mains = [pl.BlockSpec((tb, w), functools.partial(lambda i, t: (i, c0 + t), t=t)) for t in range(3)]
    prevs = [pl.BlockSpec((8, w), functools.partial(lambda i, t: (jnp.maximum(i * (tb // 8) - 1, 0), c0 + t), t=t))
             for t in range(3)]
    nexts = [pl.BlockSpec((8, w), functools.partial(lambda i, t: (jnp.minimum((i + 1) * (tb // 8), nb8 - 1), c0 + t), t=t))
             for t in range(3)]
    wspecs = [pl.BlockSpec((3, w), functools.partial(lambda i, t: (0, t), t=t)) for t in range(3)]
    bspecs = [pl.BlockSpec((1, w), functools.partial(lambda i, t: (0, t), t=t)) for t in range(3)]
    return pl.pallas_call(
        functools.partial(_hy_pre_kernel, tb=tb, seq_starts=(0, n_latent), seq_ends=(n_latent, m)),
        grid=(m // tb,),
        in_specs=mains + prevs + nexts + wspecs + bspecs,
        out_specs=[pl.BlockSpec((tb, w), lambda i: (i, 0))] * 2,
        out_shape=[jax.ShapeDtypeStruct((m, w), F32)] * 2,
        compiler_params=_cparams("parallel"),
        name="hyena_pre",
    )(*([p1] * 9), *([conv_w] * 3), *([conv_b.reshape(1, -1)] * 3))


def _hy_filter_kernel(z_ref, w1_ref, b1_ref, w2_ref, b2_ref, w3_ref, b3_ref, w4_ref, fr_ref, dl_ref,
                      h_ref, s_ref):
    hi = lax.Precision.HIGHEST
    z = z_ref[...]
    fr = fr_ref[...]
    hid = jnp.sin(fr * (jnp.dot(z, w1_ref[...], precision=hi, preferred_element_type=F32) + b1_ref[...]))
    hid = jnp.sin(fr * (jnp.dot(hid, w2_ref[...], precision=hi, preferred_element_type=F32) + b2_ref[...]))
    hid = jnp.sin(fr * (jnp.dot(hid, w3_ref[...], precision=hi, preferred_element_type=F32) + b3_ref[...]))
    h = jnp.dot(hid, w4_ref[...], precision=hi, preferred_element_type=F32)
    h = h * jnp.exp(-z[:, 0:1] * dl_ref[...])
    h_ref[...] = h

    @pl.when(pl.program_id(0) == 0)
    def _():
        s_ref[...] = jnp.zeros_like(s_ref)

    s_ref[...] += jnp.sum(jnp.abs(h), axis=0, keepdims=True)


def hyena_filter(length, w1, b1, w2, b2, w3, b3, w4, freq):
    bands = (HY_EMB - 1) // 2
    t = np.linspace(0.0, 1.0, length, dtype=np.float32)[:, None]
    wv = (np.float32(2.0 * math.pi) * np.arange(length, dtype=np.float32)[:, None] / np.float32(length)).astype(np.float32)
    f = np.linspace(1e-4, bands - 1, bands, dtype=np.float32)
    zt = np.concatenate([t, np.cos(f * wv), -np.sin(f * wv)], axis=-1).astype(np.float32)
    z = np.zeros((length, 128), np.float32)
    z[:, :HY_EMB] = zt
    deltas = np.abs(np.linspace(HY_DECAY_MIN, HY_DECAY_MAX, MIX_W, dtype=np.float32))
    dl = np.concatenate([deltas, deltas])[None, :].astype(np.float32)

    def pad2(a, r, c):
        return jnp.zeros((r, c), F32).at[:a.shape[0], :a.shape[1]].set(a.astype(F32))

    hf = w1.shape[1]
    tb = _pick(length, (512, 256, 128, 8))
    full = lambda i: (0, 0)
    return pl.pallas_call(
        _hy_filter_kernel,
        grid=(length // tb,),
        in_specs=[pl.BlockSpec((tb, 128), lambda i: (i, 0)),
                  pl.BlockSpec((128, 128), full), pl.BlockSpec((1, 128), full),
                  pl.BlockSpec((128, 128), full), pl.BlockSpec((1, 128), full),
                  pl.BlockSpec((128, 128), full), pl.BlockSpec((1, 128), full),
                  pl.BlockSpec((128, 2 * MIX_W), full), pl.BlockSpec((1, 128), full),
                  pl.BlockSpec((1, 2 * MIX_W), full)],
        out_specs=[pl.BlockSpec((tb, 2 * MIX_W), lambda i: (i, 0)),
                   pl.BlockSpec((1, 2 * MIX_W), full)],
        out_shape=[jax.ShapeDtypeStruct((length, 2 * MIX_W), F32),
                   jax.ShapeDtypeStruct((1, 2 * MIX_W), F32)],
        compiler_params=_cparams("arbitrary"),
        name="hyena_filter",
    )(jnp.asarray(z), pad2(w1, 128, 128), pad2(b1[None], 1, 128), pad2(w2, 128, 128), pad2(b2[None], 1, 128),
      pad2(w3, 128, 128), pad2(b3[None], 1, 128), pad2(w4, 128, 2 * MIX_W), pad2(freq[None], 1, 128),
      jnp.asarray(dl))


def _split_hi_lo(m):
    m = np.asarray(m, np.float32)
    hi = jnp.asarray(m, F32).astype(BF16)
    lo = (jnp.asarray(m, F32) - hi.astype(F32)).astype(BF16)
    return hi, lo


def _dft_consts(n1, n2):
    n = n1 * n2
    k = np.arange(n1, dtype=np.int64)
    ang1 = 2.0 * np.pi * ((k[:, None] * k[None, :]) % n1) / n1
    c1, s1 = np.cos(ang1), np.sin(ang1)
    hf = n1 // 2
    fwd_half = np.concatenate([c1[:, :hf], -s1[:, :hf]], axis=0)
    fwd_full = np.concatenate([c1, -s1], axis=0)
    inv_half = np.concatenate([c1[:hf, :], -s1[:hf, :]], axis=1)
    k2 = np.arange(n2, dtype=np.int64)
    ang2 = 2.0 * np.pi * ((k2[:, None] * k2[None, :]) % n2) / n2
    c2, s2 = np.cos(ang2), np.sin(ang2)
    m2 = np.block([[c2, s2], [-s2, c2]])
    angt = 2.0 * np.pi * ((k[:, None] * k2[None, :]) % n) / n
    twr = np.cos(angt).astype(np.float32)[:, :, None]
    twi = (-np.sin(angt)).astype(np.float32)[:, :, None]
    return dict(fwd_half=_split_hi_lo(fwd_half), fwd_full=_split_hi_lo(fwd_full), inv_half=_split_hi_lo(inv_half),
                m2=_split_hi_lo(m2), m2inv=_split_hi_lo(m2.T), twr=jnp.asarray(twr), twi=jnp.asarray(twi))


def _dot3(mh, ml, x):
    xh = x.astype(BF16)
    xl = (x - xh.astype(F32)).astype(BF16)
    return (jnp.dot(mh, xh, preferred_element_type=F32) + jnp.dot(ml, xh, preferred_element_type=F32)
            + jnp.dot(mh, xl, preferred_element_type=F32))


def _dft_left_kernel(mh_ref, ml_ref, x_ref, o_ref):
    o_ref[...] = _dot3(mh_ref[...], ml_ref[...], x_ref[...])


def _dft_left_mul_kernel(mh_ref, ml_ref, x_ref, k_ref, o_ref):
    a = _dot3(mh_ref[...], ml_ref[...], x_ref[...])
    r = a.shape[0] // 2
    ar, ai = a[:r], a[r:]
    kr, ki = k_ref[0:r, :], k_ref[r:, :]
    o_ref[0:r, :] = ar * kr - ai * ki
    o_ref[r:, :] = ar * ki + ai * kr


def dft_left(mats, x, spectrum=None):
    mh, ml = mats
    r, k = mh.shape
    w = x.shape[1]
    tc = _pick(w, (2048, 1024, 512))
    in_specs = [pl.BlockSpec((r, k), lambda j: (0, 0)), pl.BlockSpec((r, k), lambda j: (0, 0)),
                pl.BlockSpec((k, tc), lambda j: (0, j))]
    args = [mh, ml, x]
    kern = _dft_left_kernel
    if spectrum is not None:
        in_specs.append(pl.BlockSpec((r, tc), lambda j: (0, j)))
        args.append(spectrum)
        kern = _dft_left_mul_kernel
    return pl.pallas_call(
        kern,
        grid=(w // tc,),
        in_specs=in_specs,
        out_specs=pl.BlockSpec((r, tc), lambda j: (0, j)),
        out_shape=jax.ShapeDtypeStruct((r, w), F32),
        compiler_params=_cparams("parallel"),
        name="dft_left",
    )(*args)


def _dft_inv_post_kernel(mh_ref, ml_ref, b_ref, x0_ref, z_ref, bias_ref, scale_ref, o_ref):
    conv = _dot3(mh_ref[...], ml_ref[...], b_ref[...]) * scale_ref[...]
    o_ref[...] = (x0_ref[...] * (conv + z_ref[...] * bias_ref[...])).astype(o_ref.dtype)


def dft_inv_post(mats, b, x0, z, bias_t, scale_t):
    mh, ml = mats
    r, k = mh.shape
    w = b.shape[1]
    tc = _pick(w, (2048, 1024, 512))
    return pl.pallas_call(
        _dft_inv_post_kernel,
        grid=(w // tc,),
        in_specs=[pl.BlockSpec((r, k), lambda j: (0, 0)), pl.BlockSpec((r, k), lambda j: (0, 0)),
                  pl.BlockSpec((k, tc), lambda j: (0, j)),
                  pl.BlockSpec((r, tc), lambda j: (0, j)), pl.BlockSpec((r, tc), lambda j: (0, j)),
                  pl.BlockSpec((1, tc), lambda j: (0, j)), pl.BlockSpec((1, tc), lambda j: (0, j))],
        out_specs=pl.BlockSpec((r, tc), lambda j: (0, j)),
        out_shape=jax.ShapeDtypeStruct((r, w), BF16),
        compiler_params=_cparams("parallel"),
        name="dft_inv_post",
    )(mh, ml, b, x0, z, bias_t, scale_t)


def _fft_mid_kernel(*refs, n2, with_filter):
    if with_filter:
        a_ref, twr_ref, twi_ref, mh_ref, ml_ref, ih_ref, il_ref, k_ref, o_ref = refs
    else:
        a_ref, twr_ref, twi_ref, mh_ref, ml_ref, o_ref = refs
    twr, twi = twr_ref[...], twi_ref[...]
    ar, ai = a_ref[0], a_ref[1]
    t = jnp.concatenate([twr * ar - twi * ai, twr * ai + twi * ar], axis=0)
    x = _dot3(mh_ref[...], ml_ref[...], t)
    xr, xi = x[:n2], x[n2:]
    if not with_filter:
        o_ref[0] = xr
        o_ref[1] = xi
        return
    kr, ki = k_ref[0], k_ref[1]
    y = jnp.concatenate([xr * kr - xi * ki, xr * ki + xi * kr], axis=0)
    b = _dot3(ih_ref[...], il_ref[...], y)
    br, bi = b[:n2], b[n2:]
    o_ref[0] = twr * br + twi * bi
    o_ref[1] = twr * bi - twi * br


def fft_mid(consts, a, spectrum=None):
    _, n1, n2, c = a.shape
    slab = pl.BlockSpec((2, None, n2, c), lambda k: (0, k, 0, 0))
    tw = pl.BlockSpec((None, n2, 1), lambda k: (k, 0, 0))
    mat = pl.BlockSpec((2 * n2, 2 * n2), lambda k: (0, 0))
    in_specs = [slab, tw, tw, mat, mat]
    args = [a, consts["twr"], consts["twi"], *consts["m2"]]
    if spectrum is not None:
        in_specs += [mat, mat, slab]
        args += [*consts["m2inv"], spectrum]
    return pl.pallas_call(
        functools.partial(_fft_mid_kernel, n2=n2, with_filter=spectrum is not None),
        grid=(n1,),
        in_specs=in_specs,
        out_specs=slab,
        out_shape=jax.ShapeDtypeStruct(a.shape, F32),
        compiler_params=_cparams("parallel"),
        name="fft_mid",
    )(*args)


def hyena_long_conv(x0, z, h, abs_sum, bias, length):
    c = MIX_W
    n = 2 * length
    n2 = 128 if length >= 1024 else 1
    n1 = n // n2
    consts = _dft_consts(n1, n2)
    k2 = jnp.concatenate([h[:, :c], jnp.zeros((1, c), F32), jnp.flip(h[1:, c:], axis=0)], axis=0)
    inv_scale = 1.0 / ((abs_sum[:, :c] + abs_sum[:, c:]) * n)
    spec = dft_left(consts["fwd_full"], k2.reshape(n1, n2 * c))
    xin = z.reshape(n1 // 2, n2 * c)
    if n2 > 1:
        spec = fft_mid(consts, spec.reshape(2, n1, n2, c))
        a = dft_left(consts["fwd_half"], xin)
        b = fft_mid(consts, a.reshape(2, n1, n2, c), spec).reshape(2 * n1, n2 * c)
    else:
        b = dft_left(consts["fwd_half"], xin, spec)
    y = dft_inv_post(consts["inv_half"], b, x0.reshape(n1 // 2, n2 * c), xin,
                     jnp.tile(bias.reshape(1, c), (1, n2)), jnp.tile(inv_scale, (1, n2)))
    return y.reshape(length, c)


def _merge_kernel(*refs):
    ys, gs, ws = refs[0:N_BRANCH], refs[N_BRANCH:2 * N_BRANCH], refs[2 * N_BRANCH:3 * N_BRANCH]
    o_ref = refs[3 * N_BRANCH]
    acc = None
    for i in range(N_BRANCH):
        g = gs[i][...]
        t = jnp.dot(ys[i][...], ws[i][...], preferred_element_type=F32) / (1.0 + jnp.exp(-g))
        acc = t if acc is None else acc + t
    o_ref[...] = acc.astype(o_ref.dtype)


def merge_branches(ys, gates, branch_w):
    m = ys[0].shape[0]
    d = branch_w.shape[2]
    tm = _pick(m, (1280, 1024, 512, 256, 128))
    tn = _pick(d, (512, 256, 128))
    nj = d // tn
    y_specs = [pl.BlockSpec((tm, MIX_W), lambda i, j: (i, 0))] * N_BRANCH
    g_specs = [pl.BlockSpec((tm, tn), functools.partial(lambda i, j, b: (i, b * nj + j), b=b)) for b in range(N_BRANCH)]
    w_specs = [pl.BlockSpec((None, MIX_W, tn), functools.partial(lambda i, j, b: (b, 0, j), b=b)) for b in range(N_BRANCH)]
    return pl.pallas_call(
        _merge_kernel,
        grid=(m // tm, nj),
        in_specs=y_specs + g_specs + w_specs,
        out_specs=pl.BlockSpec((tm, tn), lambda i, j: (i, j)),
        out_shape=jax.ShapeDtypeStruct((m, d), BF16),
        compiler_params=_cparams("parallel", "parallel"),
        name="merge_branches",
    )(*ys, *([gates] * N_BRANCH), *([branch_w] * N_BRANCH))


FFN_HALO = 16


def _ffn_up_kernel(a_ref, wa_ref, wg_ref, cwa_ref, cwg_ref, cba_ref, cbg_ref, o_ref, *, tm, seq_starts, seq_ends):
    a = a_ref[...]
    te = tm + 2 * FFN_HALO
    row = pl.program_id(0) * tm + lax.broadcasted_iota(jnp.int32, (tm, 1), 0)
    no_prev = functools.reduce(jnp.logical_or, [row == s for s in seq_starts])
    no_next = functools.reduce(jnp.logical_or, [row == e - 1 for e in seq_ends])

    def conv(w_ref, cw_ref, cb_ref):
        u = jnp.dot(a, w_ref[...], preferred_element_type=F32)
        mid = u[FFN_HALO:FFN_HALO + tm]
        up = pltpu.roll(u, 1, 0)[FFN_HALO:FFN_HALO + tm]
        dn = pltpu.roll(u, te - 1, 0)[FFN_HALO:FFN_HALO + tm]
        up = jnp.where(no_prev, 0.0, up)
        dn = jnp.where(no_next, 0.0, dn)
        return up * cw_ref[0:1, :] + mid * cw_ref[1:2, :] + dn * cw_ref[2:3, :] + cb_ref[...]

    ua = conv(wa_ref, cwa_ref, cba_ref)
    ug = conv(wg_ref, cwg_ref, cbg_ref)
    o_ref[...] = (ug / (1.0 + jnp.exp(-ug)) * ua).astype(o_ref.dtype)


def ffn_up(n, w_up, conv_w, conv_b, n_latent):
    m, d = n.shape
    f = w_up.shape[1] // 2
    tm = _pick(m, (1280, 1024, 512, 256, 128))
    tn = _pick(f, (512, 256, 128))
    nb, nj = m // tm, f // tn
    te = tm + 2 * FFN_HALO
    padded = jnp.pad(n, ((FFN_HALO, FFN_HALO), (0, 0)))
    idx = (np.arange(nb)[:, None] * tm + np.arange(te)[None, :]).reshape(-1)
    a_ext = jnp.take(padded, jnp.asarray(idx, jnp.int32), axis=0)
    return pl.pallas_call(
        functools.partial(_ffn_up_kernel, tm=tm, seq_starts=(0, n_latent), seq_ends=(n_latent, m)),
        grid=(nb, nj),
        in_specs=[pl.BlockSpec((te, d), lambda i, j: (i, 0)),
                  pl.BlockSpec((d, tn), lambda i, j: (0, j)),
                  pl.BlockSpec((d, tn), lambda i, j: (0, nj + j)),
                  pl.BlockSpec((3, tn), lambda i, j: (0, j)),
                  pl.BlockSpec((3, tn), lambda i, j: (0, nj + j)),
                  pl.BlockSpec((1, tn), lambda i, j: (0, j)),
                  pl.BlockSpec((1, tn), lambda i, j: (0, nj + j))],
        out_specs=pl.BlockSpec((tm, tn), lambda i, j: (i, j)),
        out_shape=jax.ShapeDtypeStruct((m, f), BF16),
        compiler_params=_cparams("parallel", "parallel"),
        name="ffn_up",
    )(a_ext, w_up, w_up, conv_w, conv_w, conv_b.reshape(1, -1), conv_b.reshape(1, -1))


IN_CA = 3584
IN_GATE = 5664


def kernel(x, c, ctx, c_ctx, ada_w, ada_b, norm1_g, norm2_g, w_in, a_qn_g, a_kn_g, b_rpb,
           c_gate_w2, c_gate_b, c_norm_g, d_conv_w, d_conv_b, d_ffn_w1, d_ffn_b1, d_ffn_w2,
           d_ffn_b2, d_ffn_w3, d_ffn_b3, d_ffn_w4, d_sin_freq, d_bias, branch_w, w_out,
           ffn_up_w, ffn_conv_w, ffn_conv_b, ffn_down, final_norm_g):
    depth = ada_w.shape[0]
    n_lat, d = x.shape[1], x.shape[2]
    n_ctx = ctx.shape[1]
    h = jnp.concatenate([x[0], ctx[0]], axis=0)
    cvec = jnp.concatenate([c[0:1], c_ctx[None, :], jnp.zeros((14, d), F32)], axis=0)
    cvec = (cvec * jax.nn.sigmoid(cvec)).astype(BF16)
    ctab, stab = rope_tables(n_lat, n_ctx)
    zeros_c = jnp.zeros((n_ctx, MIX_W), BF16)
    qscale = HEAD_DIM ** -0.5

    for i in range(depth):
        ctx_out = i < depth - 1
        mod = matmul(cvec, ada_w[i].astype(BF16), F32, bias=ada_b[i])[0:2]
        sh1, sc1, g1, sh2, sc2, g2 = jnp.split(mod, 6, axis=-1)
        n1 = mod_norm(h, norm1_g[i], sh1, sc1, n_lat, BF16)
        w = w_in[i]
        w_main = jnp.concatenate([w[:, :IN_CA], w[:, IN_CA + 2 * C_GATE_RANK:IN_GATE]], axis=1).astype(BF16)
        w_gate = w[:, IN_GATE:].astype(BF16)
        w_ca = jnp.pad(w[:, IN_CA:IN_CA + 2 * C_GATE_RANK], ((0, 0), (0, 128 - 2 * C_GATE_RANK))).astype(BF16)
        p1 = matmul(n1, w_main, F32)
        gates = matmul(n1, w_gate, F32)
        ca = matmul(n1, w_ca, F32)

        q, k, v = a_prep(p1, ctab, stab, a_qn_g[i], a_kn_g[i])
        ya_l = flash_attention(q[:n_lat], k, v, A_KV_HEADS)
        yb_l = neighbourhood_attention(p1, na_bias_table(b_rpb[i]), n_lat, n_ctx)
        yc = gla_mixer(p1, ca, c_gate_w2[i], c_gate_b[i], c_norm_g[i], n_lat, n_ctx)
        x0, z = hyena_pre(p1, d_conv_w[i], d_conv_b[i], n_lat)
        filt = (d_ffn_w1[i], d_ffn_b1[i], d_ffn_w2[i], d_ffn_b2[i], d_ffn_w3[i], d_ffn_b3[i], d_ffn_w4[i],
                d_sin_freq[i])
        hl, sl = hyena_filter(n_lat, *filt)
        yd_l = hyena_long_conv(x0[:n_lat], z[:n_lat], hl, sl, d_bias[i], n_lat)
        if ctx_out:
            ya_c = flash_attention(q[n_lat:], k[n_lat:], v[n_lat:], A_KV_HEADS)
            pc = p1[n_lat:]
            yb_c = flash_attention((pc[:, P_BQ:P_BK] * qscale).astype(BF16), pc[:, P_BK:P_BV].astype(BF16),
                                   pc[:, P_BV:P_CQ].astype(BF16), B_HEADS)
            hc, sc = hyena_filter(n_ctx, *filt)
            yd_c = hyena_long_conv(x0[n_lat:], z[n_lat:], hc, sc, d_bias[i], n_ctx)
        else:
            ya_c = yb_c = yd_c = zeros_c
        ys = [jnp.concatenate([ya_l, ya_c], axis=0), jnp.concatenate([yb_l, yb_c], axis=0), yc,
              jnp.concatenate([yd_l, yd_c], axis=0)]
        merged = merge_branches(ys, gates, branch_w[i].astype(BF16))
        h = matmul_residual(merged, w_out[i].astype(BF16), h, g1, n_lat)
        n2 = mod_norm(h, norm2_g[i], sh2, sc2, n_lat, BF16)
        act = ffn_up(n2, ffn_up_w[i].astype(BF16), ffn_conv_w[i], ffn_conv_b[i], n_lat)
        h = matmul_residual(act, ffn_down[i].astype(BF16), h, g2, n_lat)

    zero2 = jnp.zeros((2, d), F32)
    out = mod_norm(h, final_norm_g, zero2, zero2, n_lat, F32, n_rows=n_lat)
    return out[None]
```

```python
import functools
import math

import numpy as np
import jax
import jax.numpy as jnp
from jax import lax
from jax.experimental import pallas as pl
from jax.experimental.pallas import tpu as pltpu

F32 = jnp.float32
BF16 = jnp.bfloat16

GRID_W = 64
HEAD_DIM = 128
A_HEADS = 4
A_KV_HEADS = 2
ROPE_THETA = 10000.0
B_HEADS = 4
NA_ROWS = 8
NA_COLS = 16
C_HEADS = 4
C_DK = 64
C_DV = 128
C_GATE_RANK = 16
C_GATE_TAU = 16.0
C_CHUNK = 64
MIX_W = 512
N_BRANCH = 4
HY_EMB = 33
HY_DECAY_MIN = math.log(1e-2) / 1.5
HY_DECAY_MAX = math.log(1e-2) / 0.3
NORM_EPS = 1e-6
NEG_BIG = -1e30

P_AQ, P_AK, P_AV = 0, 512, 768
P_BQ, P_BK, P_BV = 1024, 1536, 2048
P_CQ, P_CK, P_CV, P_CG = 2560, 2816, 3072, 3584
P_DU = 4096
P_WIDTH = 5632

VMEM_LIMIT = 56 * 1024 * 1024


def _cparams(*sem):
    return pltpu.CompilerParams(dimension_semantics=sem, vmem_limit_bytes=VMEM_LIMIT)


def _pick(n, candidates):
    for c in candidates:
        if n % c == 0:
            return c
    return n


def _modnorm_kernel(h_ref, g_ref, sh_ref, sc_ref, o_ref):
    x = h_ref[...]
    y = x * lax.rsqrt(jnp.mean(x * x, axis=-1, keepdims=True) + NORM_EPS) * g_ref[...]
    o_ref[...] = (y * (1.0 + sc_ref[...]) + sh_ref[...]).astype(o_ref.dtype)


def mod_norm(h, g, shift2, scale2, n_latent, out_dtype, n_rows=None):
    d = h.shape[1]
    tb = _pick(math.gcd(n_latent, h.shape[0]), (256, 128, 64, 8))
    m = n_rows or h.shape[0]
    nl = n_latent // tb
    sel = lambda i: (jnp.where(i >= nl, 1, 0), 0, 0)
    return pl.pallas_call(
        _modnorm_kernel,
        grid=(m // tb,),
        in_specs=[pl.BlockSpec((tb, d), lambda i: (i, 0)),
                  pl.BlockSpec((1, d), lambda i: (0, 0)),
                  pl.BlockSpec((None, 1, d), sel),
                  pl.BlockSpec((None, 1, d), sel)],
        out_specs=pl.BlockSpec((tb, d), lambda i: (i, 0)),
        out_shape=jax.ShapeDtypeStruct((m, d), out_dtype),
        compiler_params=_cparams("parallel"),
        name="mod_norm",
    )(h, g.reshape(1, d), shift2.reshape(2, 1, d), scale2.reshape(2, 1, d))


def _mm_kernel(a_ref, w_ref, o_ref):
    o_ref[...] = jnp.dot(a_ref[...], w_ref[...], preferred_element_type=F32).astype(o_ref.dtype)


def _mm_bias_kernel(a_ref, w_ref, b_ref, o_ref):
    o_ref[...] = (jnp.dot(a_ref[...], w_ref[...], preferred_element_type=F32) + b_ref[...]).astype(o_ref.dtype)


def matmul(a, w, out_dtype, bias=None, tm=None, tn=None):
    m, k = a.shape
    n = w.shape[1]
    tm = tm or _pick(m, (1280, 1024, 512, 256, 128, 8))
    tn = tn or _pick(n, (512, 256, 128))
    in_specs = [pl.BlockSpec((tm, k), lambda i, j: (i, 0)),
                pl.BlockSpec((k, tn), lambda i, j: (0, j))]
    args = [a, w]
    kern = _mm_kernel
    if bias is not None:
        in_specs.append(pl.BlockSpec((1, tn), lambda i, j: (0, j)))
        args.append(bias.reshape(1, n))
        kern = _mm_bias_kernel
    return pl.pallas_call(
        kern,
        grid=(m // tm, n // tn),
        in_specs=in_specs,
        out_specs=pl.BlockSpec((tm, tn), lambda i, j: (i, j)),
        out_shape=jax.ShapeDtypeStruct((m, n), out_dtype),
        compiler_params=_cparams("parallel", "parallel"),
        name="matmul",
    )(*args)


def _mm_res_kernel(a_ref, w_ref, h_ref, g_ref, o_ref, acc_ref, *, n_latent, tm, nk):
    kk = pl.program_id(2)

    @pl.when(kk == 0)
    def _():
        acc_ref[...] = jnp.zeros_like(acc_ref)

    acc_ref[...] += jnp.dot(a_ref[...], w_ref[...], preferred_element_type=F32)

    @pl.when(kk == nk - 1)
    def _():
        row = pl.program_id(0) * tm + lax.broadcasted_iota(jnp.int32, (tm, 1), 0)
        gate = jnp.where(row < n_latent, g_ref[0], g_ref[1])
        o_ref[...] = h_ref[...] + gate * acc_ref[...]


def matmul_residual(a, w, h, gate2, n_latent, tk=None):
    m, k = a.shape
    n = w.shape[1]
    tm = _pick(m, (1280, 1024, 512, 256, 128))
    tn = _pick(n, (512, 256, 128))
    tk = tk or _pick(k, (2048, 1408, 1024, 512, 256, 128))
    nk = k // tk
    return pl.pallas_call(
        functools.partial(_mm_res_kernel, n_latent=n_latent, tm=tm, nk=nk),
        grid=(m // tm, n // tn, nk),
        in_specs=[pl.BlockSpec((tm, tk), lambda i, j, kk: (i, kk)),
                  pl.BlockSpec((tk, tn), lambda i, j, kk: (kk, j)),
                  pl.BlockSpec((tm, tn), lambda i, j, kk: (i, j)),
                  pl.BlockSpec((2, 1, tn), lambda i, j, kk: (0, 0, j))],
        out_specs=pl.BlockSpec((tm, tn), lambda i, j, kk: (i, j)),
        out_shape=jax.ShapeDtypeStruct((m, n), F32),
        scratch_shapes=[pltpu.VMEM((tm, tn), F32)],
        compiler_params=_cparams("parallel", "parallel", "arbitrary"),
        name="matmul_residual",
    )(a, w, h, gate2.reshape(2, 1, n))


def _swap_halves(y):
    lane = lax.broadcasted_iota(jnp.int32, y.shape, 1)
    return jnp.where((lane % 64) < 32, pltpu.roll(y, 96, 1), pltpu.roll(y, 32, 1))


def _aprep_kernel(p_ref, cos_ref, sin_ref, qg_ref, kg_ref, q_ref, k_ref, v_ref):
    c = cos_ref[...]
    s = sin_ref[...]

    def norm_rope(x, g):
        y = x * lax.rsqrt(jnp.mean(x * x, axis=-1, keepdims=True) + NORM_EPS) * g
        return y * c + _swap_halves(y) * s

    for hd in range(A_HEADS):
        x = p_ref[:, hd * HEAD_DIM:(hd + 1) * HEAD_DIM]
        q_ref[:, hd * HEAD_DIM:(hd + 1) * HEAD_DIM] = (
            norm_rope(x, qg_ref[...]) * QK_SCALE).astype(q_ref.dtype)
    for hd in range(A_KV_HEADS):
        x = p_ref[:, P_AK + hd * HEAD_DIM:P_AK + (hd + 1) * HEAD_DIM]
        k_ref[:, hd * HEAD_DIM:(hd + 1) * HEAD_DIM] = norm_rope(x, kg_ref[...]).astype(k_ref.dtype)
        v = p_ref[:, P_AV + hd * HEAD_DIM:P_AV + (hd + 1) * HEAD_DIM]
        v_ref[:, 2 * hd * HEAD_DIM:(2 * hd + 1) * HEAD_DIM] = v.astype(v_ref.dtype)
        v_ref[:, (2 * hd + 1) * HEAD_DIM:(2 * hd + 2) * HEAD_DIM] = jnp.ones_like(v).astype(v_ref.dtype)


def rope_tables(n_latent, n_ctx):
    pos = np.arange(n_latent)
    axes = np.stack([pos // GRID_W, pos % GRID_W], axis=-1).astype(np.float32)
    quarter = HEAD_DIM // 4
    inv_freq = (np.float32(ROPE_THETA) ** (-np.arange(quarter, dtype=np.float32) / quarter)).astype(np.float32)
    ang = (axes[:, :, None] * inv_freq).astype(np.float32)
    cos, sin = np.cos(ang), np.sin(ang)
    ctab = np.concatenate([cos[:, 0], cos[:, 0], cos[:, 1], cos[:, 1]], axis=-1)
    stab = np.concatenate([-sin[:, 0], sin[:, 0], -sin[:, 1], sin[:, 1]], axis=-1)
    ctab = np.concatenate([ctab, np.ones((n_ctx, HEAD_DIM), np.float32)], axis=0)
    stab = np.concatenate([stab, np.zeros((n_ctx, HEAD_DIM), np.float32)], axis=0)
    return jnp.asarray(ctab, F32), jnp.asarray(stab, F32)


def a_prep(p1, ctab, stab, qn_g, kn_g):
    m = p1.shape[0]
    tb = _pick(m, (256, 128, 64, 8))
    wa = 1024
    return pl.pallas_call(
        _aprep_kernel,
        grid=(m // tb,),
        in_specs=[pl.BlockSpec((tb, wa), lambda i: (i, 0)),
                  pl.BlockSpec((tb, HEAD_DIM), lambda i: (i, 0)),
                  pl.BlockSpec((tb, HEAD_DIM), lambda i: (i, 0)),
                  pl.BlockSpec((1, HEAD_DIM), lambda i: (0, 0)),
                  pl.BlockSpec((1, HEAD_DIM), lambda i: (0, 0))],
        out_specs=[pl.BlockSpec((tb, 512), lambda i: (i, 0)),
                   pl.BlockSpec((tb, 256), lambda i: (i, 0)),
                   pl.BlockSpec((tb, 512), lambda i: (i, 0))],
        out_shape=[jax.ShapeDtypeStruct((m, 512), BF16),
                   jax.ShapeDtypeStruct((m, 256), BF16),
                   jax.ShapeDtypeStruct((m, 512), BF16)],
        compiler_params=_cparams("parallel"),
        name="a_prep",
    )(p1, ctab, stab, qn_g.reshape(1, HEAD_DIM), kn_g.reshape(1, HEAD_DIM))


LOG2E = 1.4426950408889634
QK_SCALE = HEAD_DIM ** -0.5 * LOG2E


def _flash_kernel(q_ref, k_ref, v_ref, o_ref, m_ref, acc_ref, sa_ref, sb_ref, *, group, tk, nk):
    m_ref[...] = jnp.full_like(m_ref, NEG_BIG)
    acc_ref[...] = jnp.zeros_like(acc_ref)

    def scores(j, s_ref):
        k = k_ref[pl.ds(pl.multiple_of(j * tk, tk), tk), :]
        for g in range(group):
            q = q_ref[:, g * HEAD_DIM:(g + 1) * HEAD_DIM]
            s_ref[g] = lax.dot_general(q, k, (((1,), (1,)), ((), ())), preferred_element_type=F32)

    def update(j, s_ref):
        v = v_ref[pl.ds(pl.multiple_of(j * tk, tk), tk), :]
        for g in range(group):
            s = s_ref[g]
            m_prev = m_ref[g]
            m_new = jnp.maximum(m_prev, jnp.max(s, axis=-1, keepdims=True))
            p = jnp.exp2(s - m_new).astype(BF16)
            acc_ref[g] = jnp.exp2(m_prev - m_new) * acc_ref[g] + jnp.dot(p, v, preferred_element_type=F32)
            m_ref[g] = m_new

    scores(0, sa_ref)
    pairs = (nk - 1) // 2

    def body(jj, carry):
        j = 2 * jj
        scores(j + 1, sb_ref)
        update(j, sa_ref)
        scores(j + 2, sa_ref)
        update(j + 1, sb_ref)
        return carry

    lax.fori_loop(0, pairs, body, 0)
    if (nk - 1) % 2 == 1:
        scores(nk - 1, sb_ref)
        update(nk - 2, sa_ref)
        update(nk - 1, sb_ref)
    else:
        update(nk - 1, sa_ref)
    for g in range(group):
        acc = acc_ref[g]
        o_ref[:, g * HEAD_DIM:(g + 1) * HEAD_DIM] = (acc[:, :HEAD_DIM] / acc[:, HEAD_DIM:]).astype(o_ref.dtype)


def with_ones(v, heads):
    n = v.shape[0]
    v3 = v.reshape(n, heads, HEAD_DIM)
    return jnp.concatenate([v3, jnp.ones_like(v3)], axis=-1).reshape(n, heads * 2 * HEAD_DIM)


def flash_attention(q, k, v1, kv_heads, tq=None, tk=None):
    lq, hq = q.shape
    lk = k.shape[0]
    group = hq // (kv_heads * HEAD_DIM)
    tq = tq or _pick(lq, (512, 256, 128))
    tk = tk or _pick(lk, (1280, 1024, 512, 256, 128))
    nk = lk // tk
    gw = group * HEAD_DIM
    return pl.pallas_call(
        functools.partial(_flash_kernel, group=group, tk=tk, nk=nk),
        grid=(kv_heads, lq // tq),
        in_specs=[pl.BlockSpec((tq, gw), lambda h, i: (i, h)),
                  pl.BlockSpec((lk, HEAD_DIM), lambda h, i: (0, h)),
                  pl.BlockSpec((lk, 2 * HEAD_DIM), lambda h, i: (0, h))],
        out_specs=pl.BlockSpec((tq, gw), lambda h, i: (i, h)),
        out_shape=jax.ShapeDtypeStruct((lq, hq), BF16),
        scratch_shapes=[pltpu.VMEM((group, tq, 1), F32),
                        pltpu.VMEM((group, tq, 2 * HEAD_DIM), F32),
                        pltpu.VMEM((group, tq, tk), F32),
                        pltpu.VMEM((group, tq, tk), F32)],
        compiler_params=_cparams("parallel", "parallel"),
        name="flash_attention",
    )(q, k, v1)


NA_GROUP = 4


def na_bias_table(rpb):
    col = np.arange(GRID_W)
    c_start = np.clip(col - NA_COLS // 2, 0, GRID_W - NA_COLS)
    kc = np.arange(GRID_W)
    col_ok = (kc[None, :] >= c_start[:, None]) & (kc[None, :] < c_start[:, None] + NA_COLS)
    dc = np.clip(kc[None, :] - col[:, None] + (NA_COLS - 1), 0, 2 * NA_COLS - 2)
    i = np.arange(NA_GROUP)
    j = np.arange(3 * NA_GROUP)
    dr = np.clip(j[None, :] - i[:, None] + NA_ROWS // 2 - 1, 0, 2 * NA_ROWS - 2)
    first = np.broadcast_to((j >= NA_GROUP)[None, :], dr.shape)
    inner = (j[None, :] >= i[:, None]) & (j[None, :] < i[:, None] + NA_ROWS)
    last = np.broadcast_to((j < NA_ROWS)[None, :], dr.shape)
    bias = rpb.astype(F32)[:, dr][:, :, :, dc]
    tabs = []
    for row_ok in (first, inner, last):
        ok = row_ok[:, :, None, None] & col_ok[None, None, :, :]
        t = jnp.where(jnp.asarray(ok)[None], bias * LOG2E, NEG_BIG)
        tabs.append(t.transpose(0, 1, 3, 2, 4).reshape(rpb.shape[0], NA_GROUP * GRID_W, 3 * NA_GROUP * GRID_W))
    return jnp.stack(tabs)


def _na_kernel(q_ref, k0_ref, k1_ref, k2_ref, v0_ref, v1_ref, v2_ref, kc_ref, vc_ref, b_ref, o_ref):
    k_u = jnp.concatenate([r[...].astype(BF16) for r in (k0_ref, k1_ref, k2_ref)], axis=0)
    v_u = jnp.concatenate([r[...].astype(BF16) for r in (v0_ref, v1_ref, v2_ref)], axis=0)
    k_c = kc_ref[...].astype(BF16)
    v_c = vc_ref[...].astype(BF16)
    nt = (((1,), (1,)), ((), ()))
    for hd in range(B_HEADS):
        sl = slice(hd * HEAD_DIM, (hd + 1) * HEAD_DIM)
        q = (q_ref[:, sl] * QK_SCALE).astype(BF16)
        s_loc = lax.dot_general(q, k_u[:, sl], nt, preferred_element_type=F32) + b_ref[hd]
        s_ctx = lax.dot_general(q, k_c[:, sl], nt, preferred_element_type=F32)
        mx = jnp.maximum(jnp.max(s_loc, axis=-1, keepdims=True), jnp.max(s_ctx, axis=-1, keepdims=True))
        p_loc = jnp.exp2(s_loc - mx)
        p_ctx = jnp.exp2(s_ctx - mx)
        den = jnp.sum(p_loc, axis=-1, keepdims=True) + jnp.sum(p_ctx, axis=-1, keepdims=True)
        o = (jnp.dot(p_loc.astype(BF16), v_u[:, sl], preferred_element_type=F32)
             + jnp.dot(p_ctx.astype(BF16), v_c[:, sl], preferred_element_type=F32))
        o_ref[:, sl] = (o / den).astype(o_ref.dtype)


def neighbourhood_attention(p1, bias_tab, n_latent, n_ctx):
    rows = n_latent // GRID_W
    ng = rows // NA_GROUP
    assert rows % NA_GROUP == 0 and rows >= 2 * NA_ROWS and NA_ROWS == 2 * NA_GROUP
    w = B_HEADS * HEAD_DIM
    t = NA_GROUP * GRID_W
    cq, ck, cv = P_BQ // w, P_BK // w, P_BV // w

    def kv_specs(col):
        return [pl.BlockSpec((t, w), functools.partial(lambda g, d, col: (jnp.clip(g + d, 0, ng - 1), col), d=d, col=col))
                for d in (-1, 0, 1)]

    cblk = n_latent // n_ctx
    return pl.pallas_call(
        _na_kernel,
        grid=(ng,),
        in_specs=[pl.BlockSpec((t, w), lambda g: (g, cq))] + kv_specs(ck) + kv_specs(cv) + [
            pl.BlockSpec((n_ctx, w), lambda g: (cblk, ck)),
            pl.BlockSpec((n_ctx, w), lambda g: (cblk, cv)),
            pl.BlockSpec((None,) + bias_tab.shape[1:],
                         lambda g: (jnp.where(g == 0, 0, jnp.where(g == ng - 1, 2, 1)), 0, 0, 0))],
        out_specs=pl.BlockSpec((t, w), lambda g: (g, 0)),
        out_shape=jax.ShapeDtypeStruct((n_latent, w), BF16),
        compiler_params=_cparams("parallel"),
        name="neighbourhood_attention",
    )(*([p1] * 9), bias_tab)


GLA_BLOCK = 4 * C_CHUNK


def _gla_block(q_ref, k_ref, v_ref, a_ref, w2_ref, gb_ref, st_ref, reverse, emit):
    c = C_CHUNK
    nch = GLA_BLOCK // c
    x = jnp.dot(a_ref[...].astype(BF16), w2_ref[...], preferred_element_type=F32) + gb_ref[...]
    la = (jnp.minimum(x, 0.0) - jnp.log(1.0 + jnp.exp(-jnp.abs(x)))) * (1.0 / C_GATE_TAU)
    pos = lax.broadcasted_iota(jnp.int32, la.shape, 0) % c
    b = la
    sh = 1
    while sh < c:
        if reverse:
            b = b + jnp.where(pos < c - sh, pltpu.roll(b, GLA_BLOCK - sh, 0), 0.0)
        else:
            b = b + jnp.where(pos >= sh, pltpu.roll(b, sh, 0), 0.0)
        sh *= 2
    qs = q_ref[...] * (C_DK ** -0.5)
    kk = k_ref[...]
    q_in = (qs * jnp.exp(b)).astype(BF16)
    ii = lax.broadcasted_iota(jnp.int32, (c, c), 0)
    jj = lax.broadcasted_iota(jnp.int32, (c, c), 1)
    mask = (jj >= ii) if reverse else (jj <= ii)
    nt = (((1,), (1,)), ((), ()))
    tn = (((0,), (0,)), ((), ()))
    states = [st_ref[hd] for hd in range(C_HEADS)]
    for ch in (range(nch - 1, -1, -1) if reverse else range(nch)):
        rows = slice(ch * c, (ch + 1) * c)
        bc = b[rows]
        b_mid = bc[c // 2:c // 2 + 1, :]
        b_last = bc[0:1, :] if reverse else bc[c - 1:c, :]
        q_t = (qs[rows] * jnp.exp(bc - b_mid)).astype(BF16)
        k_t = (kk[rows] * jnp.exp(b_mid - bc)).astype(BF16)
        k_d = (kk[rows] * jnp.exp(b_last - bc)).astype(BF16)
        e_last = jnp.exp(b_last)
        for hd in range(C_HEADS):
            ks = slice(hd * C_DK, (hd + 1) * C_DK)
            v = v_ref[rows, hd * C_DV:(hd + 1) * C_DV].astype(BF16)
            att = lax.dot_general(q_t[:, ks], k_t[:, ks], nt, preferred_element_type=F32)
            att = jnp.where(mask, att, 0.0).astype(BF16)
            o = (lax.dot_general(q_in[rows, ks], states[hd].astype(BF16), nt, preferred_element_type=F32)
                 + jnp.dot(att, v, preferred_element_type=F32))
            states[hd] = e_last[:, ks] * states[hd] + lax.dot_general(v, k_d[:, ks], tn, preferred_element_type=F32)
            emit(rows, hd, o)
    for hd in range(C_HEADS):
        st_ref[hd] = states[hd]


def _gla_fwd_kernel(q_ref, k_ref, v_ref, a_ref, w2_ref, gb_ref, o_ref, st_ref):
    @pl.when(pl.program_id(0) == 0)
    def _():
        st_ref[...] = jnp.zeros_like(st_ref)

    def emit(rows, hd, o):
        o_ref[rows, hd * C_DV:(hd + 1) * C_DV] = o

    _gla_block(q_ref, k_ref, v_ref, a_ref, w2_ref, gb_ref, st_ref, False, emit)


def _gla_bwd_kernel(q_ref, k_ref, v_ref, a_ref, w2_ref, gb_ref, of_ref, g_ref, ng_ref, y_ref, st_ref):
    @pl.when(pl.program_id(0) == 0)
    def _():
        st_ref[...] = jnp.zeros_like(st_ref)

    def emit(rows, hd, o):
        vs = slice(hd * C_DV, (hd + 1) * C_DV)
        x = of_ref[rows, vs] + o
        g = g_ref[rows, vs]
        y = x * lax.rsqrt(jnp.mean(x * x, axis=-1, keepdims=True) + NORM_EPS) * ng_ref[...]
        y_ref[rows, vs] = (y * (g * _sigmoid(g))).astype(y_ref.dtype)

    _gla_block(q_ref, k_ref, v_ref, a_ref, w2_ref, gb_ref, st_ref, True, emit)


def gla_mixer(p1, ca, gate_w2, gate_b, norm_g, n_latent, n_ctx):
    m = p1.shape[0]
    c = GLA_BLOCK
    assert n_latent % c == 0 and n_ctx % c == 0
    nl, nc = n_latent // c, n_ctx // c
    kw = C_HEADS * C_DK
    vw = C_HEADS * C_DV

    def blk_f(s):
        return jnp.where(s < nc, nl + s, s - nc)

    def blk_r(s):
        return nl + nc - 1 - s

    w2 = jnp.zeros((2, 128, kw), F32)
    w2 = w2.at[0, 0:C_GATE_RANK].set(gate_w2[0]).at[1, C_GATE_RANK:2 * C_GATE_RANK].set(gate_w2[1]).astype(BF16)
    gb = gate_b.reshape(2, 1, kw).astype(F32)

    def common_specs(blk):
        return [pl.BlockSpec((c, kw), lambda s: (blk(s), P_CQ // kw)),
                pl.BlockSpec((c, kw), lambda s: (blk(s), P_CK // kw)),
                pl.BlockSpec((c, vw), lambda s: (blk(s), P_CV // vw)),
                pl.BlockSpec((c, 128), lambda s: (blk(s), 0))]

    st_scratch = [pltpu.VMEM((C_HEADS, C_DV, C_DK), F32)]
    o_f = pl.pallas_call(
        _gla_fwd_kernel,
        grid=(nl + nc,),
        in_specs=common_specs(blk_f) + [pl.BlockSpec((None, 128, kw), lambda s: (0, 0, 0)),
                                        pl.BlockSpec((None, 1, kw), lambda s: (0, 0, 0))],
        out_specs=pl.BlockSpec((c, vw), lambda s: (blk_f(s), 0)),
        out_shape=jax.ShapeDtypeStruct((m, vw), F32),
        scratch_shapes=st_scratch,
        compiler_params=_cparams("arbitrary"),
        name="gla_forward",
    )(p1, p1, p1, ca, w2, gb)
    y = pl.pallas_call(
        _gla_bwd_kernel,
        grid=(nl + nc,),
        in_specs=common_specs(blk_r) + [pl.BlockSpec((None, 128, kw), lambda s: (1, 0, 0)),
                                        pl.BlockSpec((None, 1, kw), lambda s: (1, 0, 0)),
                                        pl.BlockSpec((c, vw), lambda s: (blk_r(s), 0)),
                                        pl.BlockSpec((c, vw), lambda s: (blk_r(s), P_CG // vw)),
                                        pl.BlockSpec((1, C_DV), lambda s: (0, 0))],
        out_specs=pl.BlockSpec((c, vw), lambda s: (blk_r(s), 0)),
        out_shape=jax.ShapeDtypeStruct((m, vw), BF16),
        scratch_shapes=st_scratch,
        compiler_params=_cparams("arbitrary"),
        name="gla_reverse",
    )(p1, p1, p1, ca, w2, gb, o_f, p1, norm_g.reshape(1, C_DV))
    return y


def _conv3_rows(x, prev_row, next_row, w_ref, b_ref):
    t = x.shape[0]
    row = lax.broadcasted_iota(jnp.int32, x.shape, 0)
    up = jnp.where(row == 0, prev_row, pltpu.roll(x, 1, 0))
    dn = jnp.where(row == t - 1, next_row, pltpu.roll(x, t - 1, 0))
    return up * w_ref[0:1, :] + x * w_ref[1:2, :] + dn * w_ref[2:3, :] + b_ref[...]


def _hy_pre_kernel(*refs, tb, seq_starts, seq_ends):
    mains, prevs, nexts = refs[0:3], refs[3:6], refs[6:9]
    w_refs, b_refs = refs[9:12], refs[12:15]
    x0_ref, z_ref = refs[15:17]
    r0 = pl.program_id(0) * tb
    pvalid = jnp.where(functools.reduce(jnp.logical_or, [r0 == s for s in seq_starts]), 0.0, 1.0)
    nvalid = jnp.where(functools.reduce(jnp.logical_or, [r0 + tb == e for e in seq_ends]), 0.0, 1.0)
    res = []
    for t in range(3):
        res.append(_conv3_rows(mains[t][...], prevs[t][7:8, :] * pvalid, nexts[t][0:1, :] * nvalid,
                               w_refs[t], b_refs[t]))
    x0_ref[...] = res[0]
    z_ref[...] = res[2] * res[1]


def hyena_pre(p1, conv_w, conv_b, n_latent):
    m = p1.shape[0]
    w = MIX_W
    tb = _pick(math.gcd(n_latent, m), (256, 128, 64, 8))
    nb8 = m // 8
    c0 = P_DU // w
    mains = [pl.BlockSpec((tb, w), functools.partial(lambda i, t: (i, c0 + t), t=t)) for t in range(3)]
    prevs = [pl.BlockSpec((8, w), functools.partial(lambda i, t: (jnp.maximum(i * (tb // 8) - 1, 0), c0 + t), t=t))
             for t in range(3)]
    nexts = [pl.BlockSpec((8, w), functools.partial(lambda i, t: (jnp.minimum((i + 1) * (tb // 8), nb8 - 1), c0 + t), t=t))
             for t in range(3)]
    wspecs = [pl.BlockSpec((3, w), functools.partial(lambda i, t: (0, t), t=t)) for t in range(3)]
    bspecs = [pl.BlockSpec((1, w), functools.partial(lambda i, t: (0, t), t=t)) for t in range(3)]
    return pl.pallas_call(
        functools.partial(_hy_pre_kernel, tb=tb, seq_starts=(0, n_latent), seq_ends=(n_latent, m)),
        grid=(m // tb,),
        in_specs=mains + prevs + nexts + wspecs + bspecs,
        out_specs=[pl.BlockSpec((tb, w), lambda i: (i, 0))] * 2,
        out_shape=[jax.ShapeDtypeStruct((m, w), F32)] * 2,
        compiler_params=_cparams("parallel"),
        name="hyena_pre",
    )(*([p1] * 9), *([conv_w] * 3), *([conv_b.reshape(1, -1)] * 3))


def _hy_filter_kernel(z_ref, w1_ref, b1_ref, w2_ref, b2_ref, w3_ref, b3_ref, w4_ref, fr_ref, dl_ref,
                      h_ref, s_ref):
    hi = lax.Precision.HIGHEST
    z = z_ref[...]
    fr = fr_ref[...]
    hid = jnp.sin(fr * (jnp.dot(z, w1_ref[...], precision=hi, preferred_element_type=F32) + b1_ref[...]))
    hid = jnp.sin(fr * (jnp.dot(hid, w2_ref[...], precision=hi, preferred_element_type=F32) + b2_ref[...]))
    hid = jnp.sin(fr * (jnp.dot(hid, w3_ref[...], precision=hi, preferred_element_type=F32) + b3_ref[...]))
    h = jnp.dot(hid, w4_ref[...], precision=hi, preferred_element_type=F32)
    h = h * jnp.exp(-z[:, 0:1] * dl_ref[...])
    row = pl.program_id(0) * h.shape[0] + lax.broadcasted_iota(jnp.int32, h.shape, 0)
    col = lax.broadcasted_iota(jnp.int32, h.shape, 1)
    h_ref[...] = jnp.where((row == 0) & (col >= MIX_W), 0.0, h)

    @pl.when(pl.program_id(0) == 0)
    def _():
        s_ref[...] = jnp.zeros_like(s_ref)

    s_ref[...] += jnp.sum(jnp.abs(h), axis=0, keepdims=True)


def hyena_filter(length, w1, b1, w2, b2, w3, b3, w4, freq):
    bands = (HY_EMB - 1) // 2
    t = np.linspace(0.0, 1.0, length, dtype=np.float32)[:, None]
    wv = (np.float32(2.0 * math.pi) * np.arange(length, dtype=np.float32)[:, None] / np.float32(length)).astype(np.float32)
    f = np.linspace(1e-4, bands - 1, bands, dtype=np.float32)
    zt = np.concatenate([t, np.cos(f * wv), -np.sin(f * wv)], axis=-1).astype(np.float32)
    z = np.zeros((length, 128), np.float32)
    z[:, :HY_EMB] = zt
    deltas = np.abs(np.linspace(HY_DECAY_MIN, HY_DECAY_MAX, MIX_W, dtype=np.float32))
    dl = np.concatenate([deltas, deltas])[None, :].astype(np.float32)

    def pad2(a, r, c):
        return jnp.zeros((r, c), F32).at[:a.shape[0], :a.shape[1]].set(a.astype(F32))

    hf = w1.shape[1]
    tb = _pick(length, (512, 256, 128, 8))
    full = lambda i: (0, 0)
    return pl.pallas_call(
        _hy_filter_kernel,
        grid=(length // tb,),
        in_specs=[pl.BlockSpec((tb, 128), lambda i: (i, 0)),
                  pl.BlockSpec((128, 128), full), pl.BlockSpec((1, 128), full),
                  pl.BlockSpec((128, 128), full), pl.BlockSpec((1, 128), full),
                  pl.BlockSpec((128, 128), full), pl.BlockSpec((1, 128), full),
                  pl.BlockSpec((128, 2 * MIX_W), full), pl.BlockSpec((1, 128), full),
                  pl.BlockSpec((1, 2 * MIX_W), full)],
        out_specs=[pl.BlockSpec((tb, 2 * MIX_W), lambda i: (i, 0)),
                   pl.BlockSpec((1, 2 * MIX_W), full)],
        out_shape=[jax.ShapeDtypeStruct((length, 2 * MIX_W), F32),
                   jax.ShapeDtypeStruct((1, 2 * MIX_W), F32)],
        compiler_params=_cparams("arbitrary"),
        name="hyena_filter",
    )(jnp.asarray(z), pad2(w1, 128, 128), pad2(b1[None], 1, 128), pad2(w2, 128, 128), pad2(b2[None], 1, 128),
      pad2(w3, 128, 128), pad2(b3[None], 1, 128), pad2(w4, 128, 2 * MIX_W), pad2(freq[None], 1, 128),
      jnp.asarray(dl))


def _split_hi_lo(m):
    m = np.asarray(m, np.float32)
    hi = jnp.asarray(m, F32).astype(BF16)
    lo = (jnp.asarray(m, F32) - hi.astype(F32)).astype(BF16)
    return hi, lo


def _dft_consts(n1, n2):
    n = n1 * n2
    k = np.arange(n1, dtype=np.int64)
    ang1 = 2.0 * np.pi * ((k[:, None] * k[None, :]) % n1) / n1
    c1, s1 = np.cos(ang1), np.sin(ang1)
    hf = n1 // 2
    fwd_half = np.concatenate([c1[:, :hf], -s1[:, :hf]], axis=0)
    inv_half = np.concatenate([c1[:hf, :], -s1[:hf, :]], axis=1)
    k2 = np.arange(n2, dtype=np.int64)
    ang2 = 2.0 * np.pi * ((k2[:, None] * k2[None, :]) % n2) / n2
    c2, s2 = np.cos(ang2), np.sin(ang2)
    m2 = np.block([[c2, s2], [-s2, c2]])
    angt = 2.0 * np.pi * ((k[:, None] * k2[None, :]) % n) / n
    twr = np.cos(angt).astype(np.float32)[:, :, None]
    twi = (-np.sin(angt)).astype(np.float32)[:, :, None]
    return dict(fwd_half=_split_hi_lo(fwd_half), inv_half=_split_hi_lo(inv_half),
                m2=_split_hi_lo(m2), m2inv=_split_hi_lo(m2.T), twr=jnp.asarray(twr), twi=jnp.asarray(twi))


def _dot3(mh, ml, x):
    xh = x.astype(BF16)
    xl = (x - xh.astype(F32)).astype(BF16)
    return (jnp.dot(mh, xh, preferred_element_type=F32) + jnp.dot(ml, xh, preferred_element_type=F32)
            + jnp.dot(mh, xl, preferred_element_type=F32))


def _dft_left_kernel(mh_ref, ml_ref, x_ref, o_ref):
    o_ref[...] = _dot3(mh_ref[...], ml_ref[...], x_ref[...])


def _filter_spectrum(fr, fi, c):
    return fr[:, :c] + fr[:, c:], fi[:, :c] - fi[:, c:]


def _dft_left_mul_kernel(mh_ref, ml_ref, x_ref, f_ref, o_ref):
    a = _dot3(mh_ref[...], ml_ref[...], x_ref[...])
    r = a.shape[0] // 2
    ar, ai = a[:r], a[r:]
    kr, ki = _filter_spectrum(f_ref[0:r, :], f_ref[r:, :], a.shape[1])
    o_ref[0:r, :] = ar * kr - ai * ki
    o_ref[r:, :] = ar * ki + ai * kr


def dft_left(mats, x, filt_f=None):
    mh, ml = mats
    r, k = mh.shape
    w = x.shape[1]
    tc = _pick(w, (2048, 1024, 512))
    in_specs = [pl.BlockSpec((r, k), lambda j: (0, 0)), pl.BlockSpec((r, k), lambda j: (0, 0)),
                pl.BlockSpec((k, tc), lambda j: (0, j))]
    args = [mh, ml, x]
    kern = _dft_left_kernel
    if filt_f is not None:
        assert w == tc and filt_f.shape == (r, 2 * w)
        in_specs.append(pl.BlockSpec((r, 2 * w), lambda j: (0, 0)))
        args.append(filt_f)
        kern = _dft_left_mul_kernel
    return pl.pallas_call(
        kern,
        grid=(w // tc,),
        in_specs=in_specs,
        out_specs=pl.BlockSpec((r, tc), lambda j: (0, j)),
        out_shape=jax.ShapeDtypeStruct((r, w), F32),
        compiler_params=_cparams("parallel"),
        name="dft_left",
    )(*args)


def _dft_inv_post_kernel(mh_ref, ml_ref, b_ref, x0_ref, z_ref, bias_ref, scale_ref, o_ref):
    conv = _dot3(mh_ref[...], ml_ref[...], b_ref[...]) * scale_ref[...]
    o_ref[...] = (x0_ref[...] * (conv + z_ref[...] * bias_ref[...])).astype(o_ref.dtype)


def dft_inv_post(mats, b, x0, z, bias_t, scale_t):
    mh, ml = mats
    r, k = mh.shape
    w = b.shape[1]
    tc = _pick(w, (2048, 1024, 512))
    return pl.pallas_call(
        _dft_inv_post_kernel,
        grid=(w // tc,),
        in_specs=[pl.BlockSpec((r, k), lambda j: (0, 0)), pl.BlockSpec((r, k), lambda j: (0, 0)),
                  pl.BlockSpec((k, tc), lambda j: (0, j)),
                  pl.BlockSpec((r, tc), lambda j: (0, j)), pl.BlockSpec((r, tc), lambda j: (0, j)),
                  pl.BlockSpec((1, tc), lambda j: (0, j)), pl.BlockSpec((1, tc), lambda j: (0, j))],
        out_specs=pl.BlockSpec((r, tc), lambda j: (0, j)),
        out_shape=jax.ShapeDtypeStruct((r, w), BF16),
        compiler_params=_cparams("parallel"),
        name="dft_inv_post",
    )(mh, ml, b, x0, z, bias_t, scale_t)


def _fft_mid_kernel(*refs, n2, with_filter):
    if with_filter:
        a_ref, twr_ref, twi_ref, mh_ref, ml_ref, ih_ref, il_ref, k_ref, o_ref = refs
    else:
        a_ref, twr_ref, twi_ref, mh_ref, ml_ref, o_ref = refs
    for s in range(a_ref.shape[1]):
        twr, twi = twr_ref[s], twi_ref[s]
        ar, ai = a_ref[0, s], a_ref[1, s]
        t = jnp.concatenate([twr * ar - twi * ai, twr * ai + twi * ar], axis=0)
        x = _dot3(mh_ref[...], ml_ref[...], t)
        xr, xi = x[:n2], x[n2:]
        if not with_filter:
            o_ref[0, s], o_ref[1, s] = _filter_spectrum(xr, xi, x.shape[1] // 2)
            continue
        kr, ki = k_ref[0, s], k_ref[1, s]
        y = jnp.concatenate([xr * kr - xi * ki, xr * ki + xi * kr], axis=0)
        b = _dot3(ih_ref[...], il_ref[...], y)
        br, bi = b[:n2], b[n2:]
        o_ref[0, s] = twr * br + twi * bi
        o_ref[1, s] = twr * bi - twi * br


def fft_mid(consts, a, spectrum=None):
    _, n1, n2, c = a.shape
    c_out = c if spectrum is not None else c // 2
    kb = _pick(n1, (4, 2, 1))
    slab = pl.BlockSpec((2, kb, n2, c), lambda k: (0, k, 0, 0))
    out_slab = pl.BlockSpec((2, kb, n2, c_out), lambda k: (0, k, 0, 0))
    tw = pl.BlockSpec((kb, n2, 1), lambda k: (k, 0, 0))
    mat = pl.BlockSpec((2 * n2, 2 * n2), lambda k: (0, 0))
    in_specs = [slab, tw, tw, mat, mat]
    args = [a, consts["twr"], consts["twi"], *consts["m2"]]
    if spectrum is not None:
        in_specs += [mat, mat, slab]
        args += [*consts["m2inv"], spectrum]
    return pl.pallas_call(
        functools.partial(_fft_mid_kernel, n2=n2, with_filter=spectrum is not None),
        grid=(n1 // kb,),
        in_specs=in_specs,
        out_specs=out_slab,
        out_shape=jax.ShapeDtypeStruct((2, n1, n2, c_out), F32),
        compiler_params=_cparams("parallel"),
        name="fft_mid",
    )(*args)


def hyena_long_conv(x0, z, h, abs_sum, bias, length):
    c = MIX_W
    n = 2 * length
    n2 = 128 if length >= 1024 else 1
    n1 = n // n2
    consts = _dft_consts(n1, n2)
    inv_scale = 1.0 / ((abs_sum[:, :c] + abs_sum[:, c:]) * n)
    filt_f = dft_left(consts["fwd_half"], h.reshape(n1 // 2, n2 * 2 * c))
    xin = z.reshape(n1 // 2, n2 * c)
    if n2 > 1:
        spec = fft_mid(consts, filt_f.reshape(2, n1, n2, 2 * c))
        a = dft_left(consts["fwd_half"], xin)
        b = fft_mid(consts, a.reshape(2, n1, n2, c), spec).reshape(2 * n1, n2 * c)
    else:
        b = dft_left(consts["fwd_half"], xin, filt_f)
    y = dft_inv_post(consts["inv_half"], b, x0.reshape(n1 // 2, n2 * c), xin,
                     jnp.tile(bias.reshape(1, c), (1, n2)), jnp.tile(inv_scale, (1, n2)))
    return y.reshape(length, c)


def _merge_kernel(*refs):
    ys, gs, ws = refs[0:N_BRANCH], refs[N_BRANCH:2 * N_BRANCH], refs[2 * N_BRANCH:3 * N_BRANCH]
    o_ref = refs[3 * N_BRANCH]
    acc = None
    for i in range(N_BRANCH):
        g = gs[i][...]
        t = jnp.dot(ys[i][...], ws[i][...], preferred_element_type=F32) * _sigmoid(g)
        acc = t if acc is None else acc + t
    o_ref[...] = acc.astype(o_ref.dtype)


def merge_branches(ys, gates, branch_w):
    m = ys[0].shape[0]
    d = branch_w.shape[2]
    tm = _pick(m, (1280, 1024, 512, 256, 128))
    tn = _pick(d, (512, 256, 128))
    nj = d // tn
    y_specs = [pl.BlockSpec((tm, MIX_W), lambda i, j: (i, 0))] * N_BRANCH
    g_specs = [pl.BlockSpec((tm, tn), functools.partial(lambda i, j, b: (i, b * nj + j), b=b)) for b in range(N_BRANCH)]
    w_specs = [pl.BlockSpec((None, MIX_W, tn), functools.partial(lambda i, j, b: (b, 0, j), b=b)) for b in range(N_BRANCH)]
    return pl.pallas_call(
        _merge_kernel,
        grid=(m // tm, nj),
        in_specs=y_specs + g_specs + w_specs,
        out_specs=pl.BlockSpec((tm, tn), lambda i, j: (i, j)),
        out_shape=jax.ShapeDtypeStruct((m, d), BF16),
        compiler_params=_cparams("parallel", "parallel"),
        name="merge_branches",
    )(*ys, *([gates] * N_BRANCH), *([branch_w] * N_BRANCH))


FFN_HALO = 16


def _sigmoid(x):
    return 0.5 + 0.5 * jnp.tanh(0.5 * x)


def _ffn_up_kernel(a_ref, wa_ref, wg_ref, cwa_ref, cwg_ref, cba_ref, cbg_ref, o_ref, *, tm, chunk, fixes):
    hl = FFN_HALO

    def gated(a, rows, no_prev=None, no_next=None):
        def conv(w_ref, cw_ref, cb_ref):
            u = jnp.dot(a, w_ref[...], preferred_element_type=F32)
            up, dn = u[hl - 1:hl - 1 + rows], u[hl + 1:hl + 1 + rows]
            if no_prev is not None:
                up = jnp.where(no_prev, 0.0, up)
                dn = jnp.where(no_next, 0.0, dn)
            return up * cw_ref[0:1, :] + u[hl:hl + rows] * cw_ref[1:2, :] + dn * cw_ref[2:3, :] + cb_ref[...]

        g = conv(wg_ref, cwg_ref, cbg_ref)
        return (g * _sigmoid(g) * conv(wa_ref, cwa_ref, cba_ref)).astype(o_ref.dtype)

    for c in range(tm // chunk):
        o_ref[c * chunk:(c + 1) * chunk, :] = gated(a_ref[c * chunk:(c + 1) * chunk + 2 * hl, :], chunk)

    for blk, g0, first_row in fixes:
        @pl.when(pl.program_id(0) == blk)
        def _(g0=g0, first_row=first_row):
            row = blk * tm + g0 + lax.broadcasted_iota(jnp.int32, (hl, 1), 0)
            o_ref[g0:g0 + hl, :] = gated(a_ref[g0:g0 + 3 * hl, :], hl, row == first_row, row == first_row - 1)


def ffn_up(n, w_up, conv_w, conv_b, n_latent):
    m, d = n.shape
    f = w_up.shape[1] // 2
    tm = _pick(m, (1280, 1024, 512, 256, 128))
    tn = _pick(f, (512, 256, 128))
    chunk = _pick(tm, (320, 256, 128))
    nb, nj = m // tm, f // tn
    te = tm + 2 * FFN_HALO
    assert n_latent % FFN_HALO == 0 and 0 < n_latent < m
    fixes = []
    for r in (n_latent - 1, n_latent):
        g = r // FFN_HALO * FFN_HALO
        fixes.append((g // tm, g % tm, n_latent))
    padded = jnp.pad(n, ((FFN_HALO, FFN_HALO), (0, 0)))
    idx = (np.arange(nb)[:, None] * tm + np.arange(te)[None, :]).reshape(-1)
    a_ext = jnp.take(padded, jnp.asarray(idx, jnp.int32), axis=0)
    return pl.pallas_call(
        functools.partial(_ffn_up_kernel, tm=tm, chunk=chunk, fixes=tuple(fixes)),
        grid=(nb, nj),
        in_specs=[pl.BlockSpec((te, d), lambda i, j: (i, 0)),
                  pl.BlockSpec((d, tn), lambda i, j: (0, j)),
                  pl.BlockSpec((d, tn), lambda i, j: (0, nj + j)),
                  pl.BlockSpec((3, tn), lambda i, j: (0, j)),
                  pl.BlockSpec((3, tn), lambda i, j: (0, nj + j)),
                  pl.BlockSpec((1, tn), lambda i, j: (0, j)),
                  pl.BlockSpec((1, tn), lambda i, j: (0, nj + j))],
        out_specs=pl.BlockSpec((tm, tn), lambda i, j: (i, j)),
        out_shape=jax.ShapeDtypeStruct((m, f), BF16),
        compiler_params=_cparams("parallel", "parallel"),
        name="ffn_up",
    )(a_ext, w_up, w_up, conv_w, conv_w, conv_b.reshape(1, -1), conv_b.reshape(1, -1))


IN_CA = 3584
IN_GATE = 5664


def kernel(x, c, ctx, c_ctx, ada_w, ada_b, norm1_g, norm2_g, w_in, a_qn_g, a_kn_g, b_rpb,
           c_gate_w2, c_gate_b, c_norm_g, d_conv_w, d_conv_b, d_ffn_w1, d_ffn_b1, d_ffn_w2,
           d_ffn_b2, d_ffn_w3, d_ffn_b3, d_ffn_w4, d_sin_freq, d_bias, branch_w, w_out,
           ffn_up_w, ffn_conv_w, ffn_conv_b, ffn_down, final_norm_g):
    depth = ada_w.shape[0]
    n_lat, d = x.shape[1], x.shape[2]
    n_ctx = ctx.shape[1]
    h = jnp.concatenate([x[0], ctx[0]], axis=0)
    cvec = jnp.concatenate([c[0:1], c_ctx[None, :], jnp.zeros((14, d), F32)], axis=0)
    cvec = (cvec * jax.nn.sigmoid(cvec)).astype(BF16)
    ctab, stab = rope_tables(n_lat, n_ctx)
    zeros_c = jnp.zeros((n_ctx, MIX_W), BF16)

    for i in range(depth):
        ctx_out = i < depth - 1
        mod = matmul(cvec, ada_w[i].astype(BF16), F32, bias=ada_b[i])[0:2]
        sh1, sc1, g1, sh2, sc2, g2 = jnp.split(mod, 6, axis=-1)
        n1 = mod_norm(h, norm1_g[i], sh1, sc1, n_lat, BF16)
        w = w_in[i]
        w_main = jnp.concatenate([w[:, :IN_CA], w[:, IN_CA + 2 * C_GATE_RANK:IN_GATE]], axis=1).astype(BF16)
        w_gate = w[:, IN_GATE:].astype(BF16)
        w_ca = jnp.pad(w[:, IN_CA:IN_CA + 2 * C_GATE_RANK], ((0, 0), (0, 128 - 2 * C_GATE_RANK))).astype(BF16)
        p1 = matmul(n1, w_main, F32)
        gates = matmul(n1, w_gate, F32)
        ca = matmul(n1, w_ca, F32)

        q, k, v = a_prep(p1, ctab, stab, a_qn_g[i], a_kn_g[i])
        ya_l = flash_attention(q[:n_lat], k, v, A_KV_HEADS)
        yb_l = neighbourhood_attention(p1, na_bias_table(b_rpb[i]), n_lat, n_ctx)
        yc = gla_mixer(p1, ca, c_gate_w2[i], c_gate_b[i], c_norm_g[i], n_lat, n_ctx)
        x0, z = hyena_pre(p1, d_conv_w[i], d_conv_b[i], n_lat)
        filt = (d_ffn_w1[i], d_ffn_b1[i], d_ffn_w2[i], d_ffn_b2[i], d_ffn_w3[i], d_ffn_b3[i], d_ffn_w4[i],
                d_sin_freq[i])
        hl, sl = hyena_filter(n_lat, *filt)
        yd_l = hyena_long_conv(x0[:n_lat], z[:n_lat], hl, sl, d_bias[i], n_lat)
        if ctx_out:
            ya_c = flash_attention(q[n_lat:], k[n_lat:], v[n_lat:], A_KV_HEADS)
            pc = p1[n_lat:]
            yb_c = flash_attention((pc[:, P_BQ:P_BK] * QK_SCALE).astype(BF16), pc[:, P_BK:P_BV].astype(BF16),
                                   with_ones(pc[:, P_BV:P_CQ].astype(BF16), B_HEADS), B_HEADS)
            hc, sc = hyena_filter(n_ctx, *filt)
            yd_c = hyena_long_conv(x0[n_lat:], z[n_lat:], hc, sc, d_bias[i], n_ctx)
        else:
            ya_c = yb_c = yd_c = zeros_c
        ys = [jnp.concatenate([ya_l, ya_c], axis=0), jnp.concatenate([yb_l, yb_c], axis=0), yc,
              jnp.concatenate([yd_l, yd_c], axis=0)]
        merged = merge_branches(ys, gates, branch_w[i].astype(BF16))
        h = matmul_residual(merged, w_out[i].astype(BF16), h, g1, n_lat)
        n2 = mod_norm(h, norm2_g[i], sh2, sc2, n_lat, BF16)
        act = ffn_up(n2, ffn_up_w[i].astype(BF16), ffn_conv_w[i], ffn_conv_b[i], n_lat)
        h = matmul_residual(act, ffn_down[i].astype(BF16), h, g2, n_lat)

    zero2 = jnp.zeros((2, d), F32)
    out = mod_norm(h, final_norm_g, zero2, zero2, n_lat, F32, n_rows=n_lat)
    return out[None]
```

```python
import functools
import math

import numpy as np
import jax
import jax.numpy as jnp
from jax import lax
from jax.experimental import pallas as pl
from jax.experimental.pallas import tpu as pltpu

F32 = jnp.float32
BF16 = jnp.bfloat16

GRID_W = 64
HEAD_DIM = 128
A_HEADS = 4
A_KV_HEADS = 2
ROPE_THETA = 10000.0
B_HEADS = 4
NA_ROWS = 8
NA_COLS = 16
C_HEADS = 4
C_DK = 64
C_DV = 128
C_GATE_RANK = 16
C_GATE_TAU = 16.0
C_CHUNK = 64
MIX_W = 512
N_BRANCH = 4
HY_EMB = 33
HY_DECAY_MIN = math.log(1e-2) / 1.5
HY_DECAY_MAX = math.log(1e-2) / 0.3
NORM_EPS = 1e-6
NEG_BIG = -1e30

P_AQ, P_AK, P_AV = 0, 512, 768
P_BQ, P_BK, P_BV = 1024, 1536, 2048
P_CQ, P_CK, P_CV = 2560, 2816, 3072
PA_WIDTH = 3584
PB_CG, PB_DU, PB_GATE = 0, 512, 2048
PB_WIDTH = 10240
IN_CA = PA_WIDTH

VMEM_LIMIT = 56 * 1024 * 1024


def _cparams(*sem):
    return pltpu.CompilerParams(dimension_semantics=sem, vmem_limit_bytes=VMEM_LIMIT)


def _pick(n, candidates):
    for c in candidates:
        if n % c == 0:
            return c
    return n


def _modnorm_kernel(h_ref, g_ref, sh_ref, sc_ref, o_ref):
    x = h_ref[...]
    y = x * lax.rsqrt(jnp.mean(x * x, axis=-1, keepdims=True) + NORM_EPS) * g_ref[...]
    o_ref[...] = (y * (1.0 + sc_ref[...]) + sh_ref[...]).astype(o_ref.dtype)


def mod_norm(h, g, shift2, scale2, n_latent, out_dtype, n_rows=None):
    d = h.shape[1]
    tb = _pick(math.gcd(n_latent, h.shape[0]), (256, 128, 64, 8))
    m = n_rows or h.shape[0]
    nl = n_latent // tb
    sel = lambda i: (jnp.where(i >= nl, 1, 0), 0, 0)
    return pl.pallas_call(
        _modnorm_kernel,
        grid=(m // tb,),
        in_specs=[pl.BlockSpec((tb, d), lambda i: (i, 0)),
                  pl.BlockSpec((1, d), lambda i: (0, 0)),
                  pl.BlockSpec((None, 1, d), sel),
                  pl.BlockSpec((None, 1, d), sel)],
        out_specs=pl.BlockSpec((tb, d), lambda i: (i, 0)),
        out_shape=jax.ShapeDtypeStruct((m, d), out_dtype),
        compiler_params=_cparams("parallel"),
        name="mod_norm",
    )(h, g.reshape(1, d), shift2.reshape(2, 1, d), scale2.reshape(2, 1, d))


def _modnorm_halo_kernel(hp_ref, h_ref, hn_ref, g_ref, sh_ref, sc_ref, o_ref, *, tm, halo, n_latent, m):
    i = pl.program_id(0)

    def norm(x, row0):
        row = row0 + lax.broadcasted_iota(jnp.int32, (x.shape[0], 1), 0)
        y = x * lax.rsqrt(jnp.mean(x * x, axis=-1, keepdims=True) + NORM_EPS) * g_ref[...]
        lat = row < n_latent
        y = y * (1.0 + jnp.where(lat, sc_ref[0], sc_ref[1])) + jnp.where(lat, sh_ref[0], sh_ref[1])
        return jnp.where((row >= 0) & (row < m), y, 0.0).astype(o_ref.dtype)

    o_ref[0:halo, :] = norm(hp_ref[...], i * tm - halo)
    step = _pick(tm, (256, 128))
    for c in range(tm // step):
        o_ref[halo + c * step:halo + (c + 1) * step, :] = norm(h_ref[c * step:(c + 1) * step, :], i * tm + c * step)
    o_ref[halo + tm:, :] = norm(hn_ref[...], (i + 1) * tm)


def mod_norm_halo(h, g, shift2, scale2, n_latent, tm, halo):
    m, d = h.shape
    nb = m // tm
    hb = tm // halo
    return pl.pallas_call(
        functools.partial(_modnorm_halo_kernel, tm=tm, halo=halo, n_latent=n_latent, m=m),
        grid=(nb,),
        in_specs=[pl.BlockSpec((halo, d), lambda i: (jnp.maximum(i * hb - 1, 0), 0)),
                  pl.BlockSpec((tm, d), lambda i: (i, 0)),
                  pl.BlockSpec((halo, d), lambda i: (jnp.minimum((i + 1) * hb, m // halo - 1), 0)),
                  pl.BlockSpec((1, d), lambda i: (0, 0)),
                  pl.BlockSpec((2, 1, d), lambda i: (0, 0, 0)),
                  pl.BlockSpec((2, 1, d), lambda i: (0, 0, 0))],
        out_specs=pl.BlockSpec((tm + 2 * halo, d), lambda i: (i, 0)),
        out_shape=jax.ShapeDtypeStruct((nb * (tm + 2 * halo), d), BF16),
        compiler_params=_cparams("parallel"),
        name="mod_norm_halo",
    )(h, h, h, g.reshape(1, d), shift2.reshape(2, 1, d), scale2.reshape(2, 1, d))


VMEM_TILE_BUDGET = 46 * 1024 * 1024


def _cast_at_first_row_block(w_ref, wb_ref):
    @pl.when(pl.program_id(1) == 0)
    def _():
        wb_ref[...] = w_ref[...].astype(BF16)


def _mm_kernel(a_ref, w_ref, o_ref, wb_ref):
    _cast_at_first_row_block(w_ref, wb_ref)
    o_ref[...] = jnp.dot(a_ref[...], wb_ref[...], preferred_element_type=F32).astype(o_ref.dtype)


def _mm_bias_kernel(a_ref, w_ref, b_ref, o_ref, wb_ref):
    _cast_at_first_row_block(w_ref, wb_ref)
    o_ref[...] = (jnp.dot(a_ref[...], wb_ref[...], preferred_element_type=F32) + b_ref[...]).astype(o_ref.dtype)


def matmul(a, w3, layer, col0, n, out_dtype, bias=None):
    m, k = a.shape
    tm = _pick(m, (1280, 1024, 512, 256, 128, 16))
    tn = _pick(n, (512, 256, 128))
    assert col0 % tn == 0 and w3.shape[1] == k
    c0 = col0 // tn
    in_specs = [pl.BlockSpec((tm, k), lambda j, i: (i, 0)),
                pl.BlockSpec((None, k, tn), lambda j, i: (layer, 0, c0 + j), pipeline_mode=pl.Buffered(1))]
    args = [a, w3]
    kern = _mm_kernel
    if bias is not None:
        in_specs.append(pl.BlockSpec((1, tn), lambda j, i: (0, j)))
        args.append(bias.reshape(1, n))
        kern = _mm_bias_kernel
    return pl.pallas_call(
        kern,
        grid=(n // tn, m // tm),
        in_specs=in_specs,
        out_specs=pl.BlockSpec((tm, tn), lambda j, i: (i, j)),
        out_shape=jax.ShapeDtypeStruct((m, n), out_dtype),
        scratch_shapes=[pltpu.VMEM((k, tn), BF16)],
        compiler_params=_cparams("parallel", "arbitrary"),
        name="matmul",
    )(*args)


def _mm_res_kernel(a_ref, w_ref, h_ref, g_ref, o_ref, wb_ref, *, n_latent, tm):
    _cast_at_first_row_block(w_ref, wb_ref)
    row = pl.program_id(1) * tm + lax.broadcasted_iota(jnp.int32, (tm, 1), 0)
    gate = jnp.where(row < n_latent, g_ref[0], g_ref[1])
    o_ref[...] = h_ref[...] + gate * jnp.dot(a_ref[...], wb_ref[...], preferred_element_type=F32)


def matmul_residual(a, w3, layer, h, gate2, n_latent):
    m, k = a.shape
    n = w3.shape[2]
    tn = _pick(n, (512, 256, 128))

    def vmem(tm):
        return 2 * (tm * k * 2 + 2 * tm * tn * 4) + k * tn * 4 + k * tn * 2

    tm = next(t for t in (1280, 1024, 640, 512, 320, 256, 128) if m % t == 0 and vmem(t) <= VMEM_TILE_BUDGET)
    return pl.pallas_call(
        functools.partial(_mm_res_kernel, n_latent=n_latent, tm=tm),
        grid=(n // tn, m // tm),
        in_specs=[pl.BlockSpec((tm, k), lambda j, i: (i, 0)),
                  pl.BlockSpec((None, k, tn), lambda j, i: (layer, 0, j), pipeline_mode=pl.Buffered(1)),
                  pl.BlockSpec((tm, tn), lambda j, i: (i, j)),
                  pl.BlockSpec((2, 1, tn), lambda j, i: (0, 0, j))],
        out_specs=pl.BlockSpec((tm, tn), lambda j, i: (i, j)),
        out_shape=jax.ShapeDtypeStruct((m, n), F32),
        scratch_shapes=[pltpu.VMEM((k, tn), BF16)],
        compiler_params=_cparams("parallel", "arbitrary"),
        name="matmul_residual",
    )(a, w3, h, gate2.reshape(2, 1, n))


def _swap_halves(y):
    lane = lax.broadcasted_iota(jnp.int32, y.shape, 1)
    return jnp.where((lane % 64) < 32, pltpu.roll(y, 96, 1), pltpu.roll(y, 32, 1))


def _aprep_kernel(p_ref, cos_ref, sin_ref, qg_ref, kg_ref, q_ref, k_ref, v_ref):
    c = cos_ref[...]
    s = sin_ref[...]

    def norm_rope(x, g):
        y = x * lax.rsqrt(jnp.mean(x * x, axis=-1, keepdims=True) + NORM_EPS) * g
        return y * c + _swap_halves(y) * s

    for hd in range(A_HEADS):
        x = p_ref[:, hd * HEAD_DIM:(hd + 1) * HEAD_DIM]
        q_ref[:, hd * HEAD_DIM:(hd + 1) * HEAD_DIM] = (
            norm_rope(x, qg_ref[...]) * QK_SCALE).astype(q_ref.dtype)
    for hd in range(A_KV_HEADS):
        x = p_ref[:, P_AK + hd * HEAD_DIM:P_AK + (hd + 1) * HEAD_DIM]
        k_ref[:, hd * HEAD_DIM:(hd + 1) * HEAD_DIM] = norm_rope(x, kg_ref[...]).astype(k_ref.dtype)
        v = p_ref[:, P_AV + hd * HEAD_DIM:P_AV + (hd + 1) * HEAD_DIM]
        v_ref[:, 2 * hd * HEAD_DIM:(2 * hd + 1) * HEAD_DIM] = v.astype(v_ref.dtype)
        v_ref[:, (2 * hd + 1) * HEAD_DIM:(2 * hd + 2) * HEAD_DIM] = jnp.ones_like(v).astype(v_ref.dtype)


def rope_tables(n_latent, n_ctx):
    pos = np.arange(n_latent)
    axes = np.stack([pos // GRID_W, pos % GRID_W], axis=-1).astype(np.float32)
    quarter = HEAD_DIM // 4
    inv_freq = (np.float32(ROPE_THETA) ** (-np.arange(quarter, dtype=np.float32) / quarter)).astype(np.float32)
    ang = (axes[:, :, None] * inv_freq).astype(np.float32)
    cos, sin = np.cos(ang), np.sin(ang)
    ctab = np.concatenate([cos[:, 0], cos[:, 0], cos[:, 1], cos[:, 1]], axis=-1)
    stab = np.concatenate([-sin[:, 0], sin[:, 0], -sin[:, 1], sin[:, 1]], axis=-1)
    ctab = np.concatenate([ctab, np.ones((n_ctx, HEAD_DIM), np.float32)], axis=0)
    stab = np.concatenate([stab, np.zeros((n_ctx, HEAD_DIM), np.float32)], axis=0)
    return jnp.asarray(ctab, F32), jnp.asarray(stab, F32)


def a_prep(p1, ctab, stab, qn_g, kn_g):
    m = p1.shape[0]
    tb = _pick(m, (256, 128, 64, 8))
    wa = 1024
    return pl.pallas_call(
        _aprep_kernel,
        grid=(m // tb,),
        in_specs=[pl.BlockSpec((tb, wa), lambda i: (i, 0)),
                  pl.BlockSpec((tb, HEAD_DIM), lambda i: (i, 0)),
                  pl.BlockSpec((tb, HEAD_DIM), lambda i: (i, 0)),
                  pl.BlockSpec((1, HEAD_DIM), lambda i: (0, 0)),
                  pl.BlockSpec((1, HEAD_DIM), lambda i: (0, 0))],
        out_specs=[pl.BlockSpec((tb, 512), lambda i: (i, 0)),
                   pl.BlockSpec((tb, 256), lambda i: (i, 0)),
                   pl.BlockSpec((tb, 512), lambda i: (i, 0))],
        out_shape=[jax.ShapeDtypeStruct((m, 512), BF16),
                   jax.ShapeDtypeStruct((m, 256), BF16),
                   jax.ShapeDtypeStruct((m, 512), BF16)],
        compiler_params=_cparams("parallel"),
        name="a_prep",
    )(p1, ctab, stab, qn_g.reshape(1, HEAD_DIM), kn_g.reshape(1, HEAD_DIM))


LOG2E = 1.4426950408889634
QK_SCALE = HEAD_DIM ** -0.5 * LOG2E


def _flash_kernel(q_ref, k_ref, v_ref, o_ref, m_ref, acc_ref, sa_ref, sb_ref, *, group, tk, nk):
    m_ref[...] = jnp.full_like(m_ref, NEG_BIG)
    acc_ref[...] = jnp.zeros_like(acc_ref)

    def scores(j, s_ref):
        k = k_ref[pl.ds(pl.multiple_of(j * tk, tk), tk), :]
        for g in range(group):
            q = q_ref[:, g * HEAD_DIM:(g + 1) * HEAD_DIM]
            s_ref[g] = lax.dot_general(q, k, (((1,), (1,)), ((), ())), preferred_element_type=F32)

    def update(j, s_ref):
        v = v_ref[pl.ds(pl.multiple_of(j * tk, tk), tk), :]
        for g in range(group):
            s = s_ref[g]
            m_prev = m_ref[g]
            m_new = jnp.maximum(m_prev, jnp.max(s, axis=-1, keepdims=True))
            p = jnp.exp2(s - m_new).astype(BF16)
            acc_ref[g] = jnp.exp2(m_prev - m_new) * acc_ref[g] + jnp.dot(p, v, preferred_element_type=F32)
            m_ref[g] = m_new

    scores(0, sa_ref)
    pairs = (nk - 1) // 2

    def body(jj, carry):
        j = 2 * jj
        scores(j + 1, sb_ref)
        update(j, sa_ref)
        scores(j + 2, sa_ref)
        update(j + 1, sb_ref)
        return carry

    lax.fori_loop(0, pairs, body, 0)
    if (nk - 1) % 2 == 1:
        scores(nk - 1, sb_ref)
        update(nk - 2, sa_ref)
        update(nk - 1, sb_ref)
    else:
        update(nk - 1, sa_ref)
    for g in range(group):
        acc = acc_ref[g]
        o_ref[:, g * HEAD_DIM:(g + 1) * HEAD_DIM] = (acc[:, :HEAD_DIM] / acc[:, HEAD_DIM:]).astype(o_ref.dtype)


def with_ones(v, heads):
    n = v.shape[0]
    v3 = v.reshape(n, heads, HEAD_DIM)
    return jnp.concatenate([v3, jnp.ones_like(v3)], axis=-1).reshape(n, heads * 2 * HEAD_DIM)


def flash_attention(q, k, v1, kv_heads, tq=None, tk=None):
    lq, hq = q.shape
    lk = k.shape[0]
    group = hq // (kv_heads * HEAD_DIM)
    tq = tq or _pick(lq, (1024, 512, 256, 128))
    tk = tk or _pick(lk, (1280, 1024, 512, 256, 128))
    nk = lk // tk
    gw = group * HEAD_DIM
    return pl.pallas_call(
        functools.partial(_flash_kernel, group=group, tk=tk, nk=nk),
        grid=(kv_heads, lq // tq),
        in_specs=[pl.BlockSpec((tq, gw), lambda h, i: (i, h)),
                  pl.BlockSpec((lk, HEAD_DIM), lambda h, i: (0, h), pipeline_mode=pl.Buffered(1)),
                  pl.BlockSpec((lk, 2 * HEAD_DIM), lambda h, i: (0, h), pipeline_mode=pl.Buffered(1))],
        out_specs=pl.BlockSpec((tq, gw), lambda h, i: (i, h)),
        out_shape=jax.ShapeDtypeStruct((lq, hq), BF16),
        scratch_shapes=[pltpu.VMEM((group, tq, 1), F32),
                        pltpu.VMEM((group, tq, 2 * HEAD_DIM), F32),
                        pltpu.VMEM((group, tq, tk), F32),
                        pltpu.VMEM((group, tq, tk), F32)],
        compiler_params=_cparams("parallel", "parallel"),
        name="flash_attention",
    )(q, k, v1)


NA_GROUP = 4


def na_bias_table(rpb):
    col = np.arange(GRID_W)
    c_start = np.clip(col - NA_COLS // 2, 0, GRID_W - NA_COLS)
    kc = np.arange(GRID_W)
    col_ok = (kc[None, :] >= c_start[:, None]) & (kc[None, :] < c_start[:, None] + NA_COLS)
    dc = np.clip(kc[None, :] - col[:, None] + (NA_COLS - 1), 0, 2 * NA_COLS - 2)
    i = np.arange(NA_GROUP)
    j = np.arange(3 * NA_GROUP)
    dr = np.clip(j[None, :] - i[:, None] + NA_ROWS // 2 - 1, 0, 2 * NA_ROWS - 2)
    first = np.broadcast_to((j >= NA_GROUP)[None, :], dr.shape)
    inner = (j[None, :] >= i[:, None]) & (j[None, :] < i[:, None] + NA_ROWS)
    last = np.broadcast_to((j < NA_ROWS)[None, :], dr.shape)
    bias = rpb.astype(F32)[:, dr][:, :, :, dc]
    tabs = []
    for row_ok in (first, inner, last):
        ok = row_ok[:, :, None, None] & col_ok[None, None, :, :]
        t = jnp.where(jnp.asarray(ok)[None], bias * LOG2E, NEG_BIG)
        tabs.append(t.transpose(0, 1, 3, 2, 4).reshape(rpb.shape[0], NA_GROUP * GRID_W, 3 * NA_GROUP * GRID_W))
    return jnp.stack(tabs)


def _na_kernel(q_ref, k0_ref, k1_ref, k2_ref, v0_ref, v1_ref, v2_ref, kc_ref, vc_ref, b_ref, o_ref):
    k_u = jnp.concatenate([r[...].astype(BF16) for r in (k0_ref, k1_ref, k2_ref)], axis=0)
    v_u = jnp.concatenate([r[...].astype(BF16) for r in (v0_ref, v1_ref, v2_ref)], axis=0)
    k_c = kc_ref[...].astype(BF16)
    v_c = vc_ref[...].astype(BF16)
    nt = (((1,), (1,)), ((), ()))
    for hd in range(B_HEADS):
        sl = slice(hd * HEAD_DIM, (hd + 1) * HEAD_DIM)
        q = (q_ref[:, sl] * QK_SCALE).astype(BF16)
        s_loc = lax.dot_general(q, k_u[:, sl], nt, preferred_element_type=F32) + b_ref[hd]
        s_ctx = lax.dot_general(q, k_c[:, sl], nt, preferred_element_type=F32)
        mx = jnp.maximum(jnp.max(s_loc, axis=-1, keepdims=True), jnp.max(s_ctx, axis=-1, keepdims=True))
        p_loc = jnp.exp2(s_loc - mx)
        p_ctx = jnp.exp2(s_ctx - mx)
        den = jnp.sum(p_loc, axis=-1, keepdims=True) + jnp.sum(p_ctx, axis=-1, keepdims=True)
        o = (jnp.dot(p_loc.astype(BF16), v_u[:, sl], preferred_element_type=F32)
             + jnp.dot(p_ctx.astype(BF16), v_c[:, sl], preferred_element_type=F32))
        o_ref[:, sl] = (o / den).astype(o_ref.dtype)


def neighbourhood_attention(p1, bias_tab, n_latent, n_ctx):
    rows = n_latent // GRID_W
    ng = rows // NA_GROUP
    assert rows % NA_GROUP == 0 and rows >= 2 * NA_ROWS and NA_ROWS == 2 * NA_GROUP
    w = B_HEADS * HEAD_DIM
    t = NA_GROUP * GRID_W
    cq, ck, cv = P_BQ // w, P_BK // w, P_BV // w

    def kv_specs(col):
        return [pl.BlockSpec((t, w), functools.partial(lambda g, d, col: (jnp.clip(g + d, 0, ng - 1), col), d=d, col=col))
                for d in (-1, 0, 1)]

    cblk = n_latent // n_ctx
    return pl.pallas_call(
        _na_kernel,
        grid=(ng,),
        in_specs=[pl.BlockSpec((t, w), lambda g: (g, cq))] + kv_specs(ck) + kv_specs(cv) + [
            pl.BlockSpec((n_ctx, w), lambda g: (cblk, ck)),
            pl.BlockSpec((n_ctx, w), lambda g: (cblk, cv)),
            pl.BlockSpec((None,) + bias_tab.shape[1:],
                         lambda g: (jnp.where(g == 0, 0, jnp.where(g == ng - 1, 2, 1)), 0, 0, 0))],
        out_specs=pl.BlockSpec((t, w), lambda g: (g, 0)),
        out_shape=jax.ShapeDtypeStruct((n_latent, w), BF16),
        compiler_params=_cparams("parallel"),
        name="neighbourhood_attention",
    )(*([p1] * 9), bias_tab)


GLA_BLOCK = 4 * C_CHUNK


def _gla_block(q_ref, k_ref, v_ref, a_ref, w2_ref, gb_ref, st_ref, reverse, emit):
    c = C_CHUNK
    nch = GLA_BLOCK // c
    x = jnp.dot(a_ref[...].astype(BF16), w2_ref[...], preferred_element_type=F32) + gb_ref[...]
    la = (jnp.minimum(x, 0.0) - jnp.log(1.0 + jnp.exp(-jnp.abs(x)))) * (1.0 / C_GATE_TAU)
    pos = lax.broadcasted_iota(jnp.int32, la.shape, 0) % c
    b = la
    sh = 1
    while sh < c:
        if reverse:
            b = b + jnp.where(pos < c - sh, pltpu.roll(b, GLA_BLOCK - sh, 0), 0.0)
        else:
            b = b + jnp.where(pos >= sh, pltpu.roll(b, sh, 0), 0.0)
        sh *= 2
    qs = q_ref[...] * (C_DK ** -0.5)
    kk = k_ref[...]
    q_in = (qs * jnp.exp(b)).astype(BF16)
    ii = lax.broadcasted_iota(jnp.int32, (c, c), 0)
    jj = lax.broadcasted_iota(jnp.int32, (c, c), 1)
    mask = (jj >= ii) if reverse else (jj <= ii)
    nt = (((1,), (1,)), ((), ()))
    tn = (((0,), (0,)), ((), ()))
    states = [st_ref[hd] for hd in range(C_HEADS)]
    for ch in (range(nch - 1, -1, -1) if reverse else range(nch)):
        rows = slice(ch * c, (ch + 1) * c)
        bc = b[rows]
        b_mid = bc[c // 2:c // 2 + 1, :]
        b_last = bc[0:1, :] if reverse else bc[c - 1:c, :]
        q_t = (qs[rows] * jnp.exp(bc - b_mid)).astype(BF16)
        k_t = (kk[rows] * jnp.exp(b_mid - bc)).astype(BF16)
        k_d = (kk[rows] * jnp.exp(b_last - bc)).astype(BF16)
        e_last = jnp.exp(b_last)
        for hd in range(C_HEADS):
            ks = slice(hd * C_DK, (hd + 1) * C_DK)
            v = v_ref[rows, hd * C_DV:(hd + 1) * C_DV].astype(BF16)
            att = lax.dot_general(q_t[:, ks], k_t[:, ks], nt, preferred_element_type=F32)
            att = jnp.where(mask, att, 0.0).astype(BF16)
            o = (lax.dot_general(q_in[rows, ks], states[hd].astype(BF16), nt, preferred_element_type=F32)
                 + jnp.dot(att, v, preferred_element_type=F32))
            states[hd] = e_last[:, ks] * states[hd] + lax.dot_general(v, k_d[:, ks], tn, preferred_element_type=F32)
            emit(rows, hd, o)
    for hd in range(C_HEADS):
        st_ref[hd] = states[hd]


def _gla_fwd_kernel(q_ref, k_ref, v_ref, a_ref, w2_ref, gb_ref, o_ref, st_ref):
    @pl.when(pl.program_id(0) == 0)
    def _():
        st_ref[...] = jnp.zeros_like(st_ref)

    def emit(rows, hd, o):
        o_ref[rows, hd * C_DV:(hd + 1) * C_DV] = o

    _gla_block(q_ref, k_ref, v_ref, a_ref, w2_ref, gb_ref, st_ref, False, emit)


def _gla_bwd_kernel(q_ref, k_ref, v_ref, a_ref, w2_ref, gb_ref, of_ref, g_ref, ng_ref, y_ref, st_ref):
    @pl.when(pl.program_id(0) == 0)
    def _():
        st_ref[...] = jnp.zeros_like(st_ref)

    def emit(rows, hd, o):
        vs = slice(hd * C_DV, (hd + 1) * C_DV)
        x = of_ref[rows, vs] + o
        g = g_ref[rows, vs]
        y = x * lax.rsqrt(jnp.mean(x * x, axis=-1, keepdims=True) + NORM_EPS) * ng_ref[...]
        y_ref[rows, vs] = (y * (g * _sigmoid(g))).astype(y_ref.dtype)

    _gla_block(q_ref, k_ref, v_ref, a_ref, w2_ref, gb_ref, st_ref, True, emit)


def gla_mixer(p1, pb, ca, gate_w2, gate_b, norm_g, n_latent, n_ctx):
    m = p1.shape[0]
    c = GLA_BLOCK
    assert n_latent % c == 0 and n_ctx % c == 0
    nl, nc = n_latent // c, n_ctx // c
    kw = C_HEADS * C_DK
    vw = C_HEADS * C_DV

    def blk_f(s):
        return jnp.where(s < nc, nl + s, s - nc)

    def blk_r(s):
        return nl + nc - 1 - s

    w2 = jnp.zeros((2, 128, kw), F32)
    w2 = w2.at[0, 0:C_GATE_RANK].set(gate_w2[0]).at[1, C_GATE_RANK:2 * C_GATE_RANK].set(gate_w2[1]).astype(BF16)
    gb = gate_b.reshape(2, 1, kw).astype(F32)

    def common_specs(blk):
        return [pl.BlockSpec((c, kw), lambda s: (blk(s), P_CQ // kw)),
                pl.BlockSpec((c, kw), lambda s: (blk(s), P_CK // kw)),
                pl.BlockSpec((c, vw), lambda s: (blk(s), P_CV // vw)),
                pl.BlockSpec((c, 128), lambda s: (blk(s), 0))]

    st_scratch = [pltpu.VMEM((C_HEADS, C_DV, C_DK), F32)]
    o_f = pl.pallas_call(
        _gla_fwd_kernel,
        grid=(nl + nc,),
        in_specs=common_specs(blk_f) + [pl.BlockSpec((None, 128, kw), lambda s: (0, 0, 0)),
                                        pl.BlockSpec((None, 1, kw), lambda s: (0, 0, 0))],
        out_specs=pl.BlockSpec((c, vw), lambda s: (blk_f(s), 0)),
        out_shape=jax.ShapeDtypeStruct((m, vw), F32),
        scratch_shapes=st_scratch,
        compiler_params=_cparams("arbitrary"),
        name="gla_forward",
    )(p1, p1, p1, ca, w2, gb)
    y = pl.pallas_call(
        _gla_bwd_kernel,
        grid=(nl + nc,),
        in_specs=common_specs(blk_r) + [pl.BlockSpec((None, 128, kw), lambda s: (1, 0, 0)),
                                        pl.BlockSpec((None, 1, kw), lambda s: (1, 0, 0)),
                                        pl.BlockSpec((c, vw), lambda s: (blk_r(s), 0)),
                                        pl.BlockSpec((c, vw), lambda s: (blk_r(s), PB_CG // vw)),
                                        pl.BlockSpec((1, C_DV), lambda s: (0, 0))],
        out_specs=pl.BlockSpec((c, vw), lambda s: (blk_r(s), 0)),
        out_shape=jax.ShapeDtypeStruct((m, vw), BF16),
        scratch_shapes=st_scratch,
        compiler_params=_cparams("arbitrary"),
        name="gla_reverse",
    )(p1, p1, p1, ca, w2, gb, o_f, pb, norm_g.reshape(1, C_DV))
    return y


def _conv3_rows(x, prev_row, next_row, w_ref, b_ref):
    t = x.shape[0]
    row = lax.broadcasted_iota(jnp.int32, x.shape, 0)
    up = jnp.where(row == 0, prev_row, pltpu.roll(x, 1, 0))
    dn = jnp.where(row == t - 1, next_row, pltpu.roll(x, t - 1, 0))
    return up * w_ref[0:1, :] + x * w_ref[1:2, :] + dn * w_ref[2:3, :] + b_ref[...]


def _hy_pre_kernel(*refs, tb, seq_starts, seq_ends):
    mains, prevs, nexts = refs[0:3], refs[3:6], refs[6:9]
    w_refs, b_refs = refs[9:12], refs[12:15]
    x0_ref, z_ref = refs[15:17]
    r0 = pl.program_id(0) * tb
    pvalid = jnp.where(functools.reduce(jnp.logical_or, [r0 == s for s in seq_starts]), 0.0, 1.0)
    nvalid = jnp.where(functools.reduce(jnp.logical_or, [r0 + tb == e for e in seq_ends]), 0.0, 1.0)
    res = []
    for t in range(3):
        res.append(_conv3_rows(mains[t][...], prevs[t][7:8, :] * pvalid, nexts[t][0:1, :] * nvalid,
                               w_refs[t], b_refs[t]))
    x0_ref[...] = res[0]
    z_ref[...] = res[2] * res[1]


def hyena_pre(p1, conv_w, conv_b, n_latent):
    m = p1.shape[0]
    w = MIX_W
    tb = _pick(math.gcd(n_latent, m), (256, 128, 64, 8))
    nb8 = m // 8
    c0 = PB_DU // w
    mains = [pl.BlockSpec((tb, w), functools.partial(lambda i, t: (i, c0 + t), t=t)) for t in range(3)]
    prevs = [pl.BlockSpec((8, w), functools.partial(lambda i, t: (jnp.maximum(i * (tb // 8) - 1, 0), c0 + t), t=t))
             for t in range(3)]
    nexts = [pl.BlockSpec((8, w), functools.partial(lambda i, t: (jnp.minimum((i + 1) * (tb // 8), nb8 - 1), c0 + t), t=t))
             for t in range(3)]
    wspecs = [pl.BlockSpec((3, w), functools.partial(lambda i, t: (0, t), t=t)) for t in range(3)]
    bspecs = [pl.BlockSpec((1, w), functools.partial(lambda i, t: (0, t), t=t)) for t in range(3)]
    return pl.pallas_call(
        functools.partial(_hy_pre_kernel, tb=tb, seq_starts=(0, n_latent), seq_ends=(n_latent, m)),
        grid=(m // tb,),
        in_specs=mains + prevs + nexts + wspecs + bspecs,
        out_specs=[pl.BlockSpec((tb, w), lambda i: (i, 0))] * 2,
        out_shape=[jax.ShapeDtypeStruct((m, w), F32)] * 2,
        compiler_params=_cparams("parallel"),
        name="hyena_pre",
    )(*([p1] * 9), *([conv_w] * 3), *([conv_b.reshape(1, -1)] * 3))


def _hy_filter_kernel(z_ref, w1_ref, b1_ref, w2_ref, b2_ref, w3_ref, b3_ref, w4_ref, fr_ref, dl_ref,
                      h_ref, s_ref):
    def dot3(x, w):
        xh, wh = x.astype(BF16), w.astype(BF16)
        xl, wl = (x - xh.astype(F32)).astype(BF16), (w - wh.astype(F32)).astype(BF16)
        return (jnp.dot(xh, wh, preferred_element_type=F32) + jnp.dot(xh, wl, preferred_element_type=F32)
                + jnp.dot(xl, wh, preferred_element_type=F32))

    z = z_ref[...]
    fr = fr_ref[...]
    hid = jnp.sin(fr * (dot3(z, w1_ref[...]) + b1_ref[...]))
    hid = jnp.sin(fr * (dot3(hid, w2_ref[...]) + b2_ref[...]))
    hid = jnp.sin(fr * (dot3(hid, w3_ref[...]) + b3_ref[...]))
    h = dot3(hid, w4_ref[...])
    h = h * jnp.exp(-z[:, 0:1] * dl_ref[...])
    row = pl.program_id(0) * h.shape[0] + lax.broadcasted_iota(jnp.int32, h.shape, 0)
    col = lax.broadcasted_iota(jnp.int32, h.shape, 1)
    h_ref[...] = jnp.where((row == 0) & (col >= MIX_W), 0.0, h)

    @pl.when(pl.program_id(0) == 0)
    def _():
        s_ref[...] = jnp.zeros_like(s_ref)

    s_ref[...] += jnp.sum(jnp.abs(h), axis=0, keepdims=True)


def hyena_filter(length, w1, b1, w2, b2, w3, b3, w4, freq):
    bands = (HY_EMB - 1) // 2
    t = np.linspace(0.0, 1.0, length, dtype=np.float32)[:, None]
    wv = (np.float32(2.0 * math.pi) * np.arange(length, dtype=np.float32)[:, None] / np.float32(length)).astype(np.float32)
    f = np.linspace(1e-4, bands - 1, bands, dtype=np.float32)
    zt = np.concatenate([t, np.cos(f * wv), -np.sin(f * wv)], axis=-1).astype(np.float32)
    z = np.zeros((length, 128), np.float32)
    z[:, :HY_EMB] = zt
    deltas = np.abs(np.linspace(HY_DECAY_MIN, HY_DECAY_MAX, MIX_W, dtype=np.float32))
    dl = np.concatenate([deltas, deltas])[None, :].astype(np.float32)

    def pad2(a, r, c):
        return jnp.zeros((r, c), F32).at[:a.shape[0], :a.shape[1]].set(a.astype(F32))

    hf = w1.shape[1]
    tb = _pick(length, (512, 256, 128, 8))
    full = lambda i: (0, 0)
    return pl.pallas_call(
        _hy_filter_kernel,
        grid=(length // tb,),
        in_specs=[pl.BlockSpec((tb, 128), lambda i: (i, 0)),
                  pl.BlockSpec((128, 128), full), pl.BlockSpec((1, 128), full),
                  pl.BlockSpec((128, 128), full), pl.BlockSpec((1, 128), full),
                  pl.BlockSpec((128, 128), full), pl.BlockSpec((1, 128), full),
                  pl.BlockSpec((128, 2 * MIX_W), full), pl.BlockSpec((1, 128), full),
                  pl.BlockSpec((1, 2 * MIX_W), full)],
        out_specs=[pl.BlockSpec((tb, 2 * MIX_W), lambda i: (i, 0)),
                   pl.BlockSpec((1, 2 * MIX_W), full)],
        out_shape=[jax.ShapeDtypeStruct((length, 2 * MIX_W), F32),
                   jax.ShapeDtypeStruct((1, 2 * MIX_W), F32)],
        compiler_params=_cparams("arbitrary"),
        name="hyena_filter",
    )(jnp.asarray(z), pad2(w1, 128, 128), pad2(b1[None], 1, 128), pad2(w2, 128, 128), pad2(b2[None], 1, 128),
      pad2(w3, 128, 128), pad2(b3[None], 1, 128), pad2(w4, 128, 2 * MIX_W), pad2(freq[None], 1, 128),
      jnp.asarray(dl))


def _split_hi_lo(m):
    m = np.asarray(m, np.float32)
    hi = jnp.asarray(m, F32).astype(BF16)
    lo = (jnp.asarray(m, F32) - hi.astype(F32)).astype(BF16)
    return hi, lo


def _dft_consts(n1, n2):
    n = n1 * n2
    k = np.arange(n1, dtype=np.int64)
    ang1 = 2.0 * np.pi * ((k[:, None] * k[None, :]) % n1) / n1
    c1, s1 = np.cos(ang1), np.sin(ang1)
    hf = n1 // 2
    fwd_half = np.concatenate([c1[:, :hf], -s1[:, :hf]], axis=0)
    inv_half = np.concatenate([c1[:hf, :], -s1[:hf, :]], axis=1)
    k2 = np.arange(n2, dtype=np.int64)
    ang2 = 2.0 * np.pi * ((k2[:, None] * k2[None, :]) % n2) / n2
    c2, s2 = np.cos(ang2), np.sin(ang2)
    m2 = np.block([[c2, s2], [-s2, c2]])
    angt = 2.0 * np.pi * ((k[:, None] * k2[None, :]) % n) / n
    twr = np.cos(angt).astype(np.float32)[:, :, None]
    twi = (-np.sin(angt)).astype(np.float32)[:, :, None]
    return dict(fwd_half=_split_hi_lo(fwd_half), inv_half=_split_hi_lo(inv_half),
                m2=_split_hi_lo(m2), m2inv=_split_hi_lo(m2.T), twr=jnp.asarray(twr), twi=jnp.asarray(twi))


def _dot3(mh, ml, x):
    xh = x.astype(BF16)
    xl = (x - xh.astype(F32)).astype(BF16)
    return (jnp.dot(mh, xh, preferred_element_type=F32) + jnp.dot(ml, xh, preferred_element_type=F32)
            + jnp.dot(mh, xl, preferred_element_type=F32))


def _dft_left_kernel(mh_ref, ml_ref, x_ref, o_ref):
    o_ref[...] = _dot3(mh_ref[...], ml_ref[...], x_ref[...])


def _filter_spectrum(fr, fi, c):
    return fr[:, :c] + fr[:, c:], fi[:, :c] - fi[:, c:]


def _dft_left_mul_kernel(mh_ref, ml_ref, x_ref, f_ref, o_ref):
    a = _dot3(mh_ref[...], ml_ref[...], x_ref[...])
    r = a.shape[0] // 2
    ar, ai = a[:r], a[r:]
    kr, ki = _filter_spectrum(f_ref[0:r, :], f_ref[r:, :], a.shape[1])
    o_ref[0:r, :] = ar * kr - ai * ki
    o_ref[r:, :] = ar * ki + ai * kr


def dft_left(mats, x, filt_f=None):
    mh, ml = mats
    r, k = mh.shape
    w = x.shape[1]
    tc = _pick(w, (2048, 1024, 512))
    in_specs = [pl.BlockSpec((r, k), lambda j: (0, 0)), pl.BlockSpec((r, k), lambda j: (0, 0)),
                pl.BlockSpec((k, tc), lambda j: (0, j))]
    args = [mh, ml, x]
    kern = _dft_left_kernel
    if filt_f is not None:
        assert w == tc and filt_f.shape == (r, 2 * w)
        in_specs.append(pl.BlockSpec((r, 2 * w), lambda j: (0, 0)))
        args.append(filt_f)
        kern = _dft_left_mul_kernel
    return pl.pallas_call(
        kern,
        grid=(w // tc,),
        in_specs=in_specs,
        out_specs=pl.BlockSpec((r, tc), lambda j: (0, j)),
        out_shape=jax.ShapeDtypeStruct((r, w), F32),
        compiler_params=_cparams("parallel"),
        name="dft_left",
    )(*args)


def _dft_inv_post_kernel(mh_ref, ml_ref, b_ref, x0_ref, z_ref, bias_ref, scale_ref, o_ref):
    conv = _dot3(mh_ref[...], ml_ref[...], b_ref[...]) * scale_ref[...]
    o_ref[...] = (x0_ref[...] * (conv + z_ref[...] * bias_ref[...])).astype(o_ref.dtype)


def dft_inv_post(mats, b, x0, z, bias_t, scale_t):
    mh, ml = mats
    r, k = mh.shape
    w = b.shape[1]
    tc = _pick(w, (2048, 1024, 512))
    return pl.pallas_call(
        _dft_inv_post_kernel,
        grid=(w // tc,),
        in_specs=[pl.BlockSpec((r, k), lambda j: (0, 0)), pl.BlockSpec((r, k), lambda j: (0, 0)),
                  pl.BlockSpec((k, tc), lambda j: (0, j)),
                  pl.BlockSpec((r, tc), lambda j: (0, j)), pl.BlockSpec((r, tc), lambda j: (0, j)),
                  pl.BlockSpec((1, tc), lambda j: (0, j)), pl.BlockSpec((1, tc), lambda j: (0, j))],
        out_specs=pl.BlockSpec((r, tc), lambda j: (0, j)),
        out_shape=jax.ShapeDtypeStruct((r, w), BF16),
        compiler_params=_cparams("parallel"),
        name="dft_inv_post",
    )(mh, ml, b, x0, z, bias_t, scale_t)


def _fft_mid_kernel(*refs, n2, with_filter):
    if with_filter:
        a_ref, twr_ref, twi_ref, mh_ref, ml_ref, ih_ref, il_ref, k_ref, o_ref = refs
    else:
        a_ref, twr_ref, twi_ref, mh_ref, ml_ref, o_ref = refs
    for s in range(a_ref.shape[1]):
        twr, twi = twr_ref[s], twi_ref[s]
        ar, ai = a_ref[0, s], a_ref[1, s]
        t = jnp.concatenate([twr * ar - twi * ai, twr * ai + twi * ar], axis=0)
        x = _dot3(mh_ref[...], ml_ref[...], t)
        xr, xi = x[:n2], x[n2:]
        if not with_filter:
            o_ref[0, s], o_ref[1, s] = _filter_spectrum(xr, xi, x.shape[1] // 2)
            continue
        kr, ki = k_ref[0, s], k_ref[1, s]
        y = jnp.concatenate([xr * kr - xi * ki, xr * ki + xi * kr], axis=0)
        b = _dot3(ih_ref[...], il_ref[...], y)
        br, bi = b[:n2], b[n2:]
        o_ref[0, s] = twr * br + twi * bi
        o_ref[1, s] = twr * bi - twi * br


def fft_mid(consts, a, spectrum=None):
    _, n1, n2, c = a.shape
    c_out = c if spectrum is not None else c // 2
    kb = _pick(n1, (4, 2, 1))
    slab = pl.BlockSpec((2, kb, n2, c), lambda k: (0, k, 0, 0))
    out_slab = pl.BlockSpec((2, kb, n2, c_out), lambda k: (0, k, 0, 0))
    tw = pl.BlockSpec((kb, n2, 1), lambda k: (k, 0, 0))
    mat = pl.BlockSpec((2 * n2, 2 * n2), lambda k: (0, 0))
    in_specs = [slab, tw, tw, mat, mat]
    args = [a, consts["twr"], consts["twi"], *consts["m2"]]
    if spectrum is not None:
        in_specs += [mat, mat, slab]
        args += [*consts["m2inv"], spectrum]
    return pl.pallas_call(
        functools.partial(_fft_mid_kernel, n2=n2, with_filter=spectrum is not None),
        grid=(n1 // kb,),
        in_specs=in_specs,
        out_specs=out_slab,
        out_shape=jax.ShapeDtypeStruct((2, n1, n2, c_out), F32),
        compiler_params=_cparams("parallel"),
        name="fft_mid",
    )(*args)


def hyena_long_conv(x0, z, h, abs_sum, bias, length):
    c = MIX_W
    n = 2 * length
    n2 = 128 if length >= 1024 else 1
    n1 = n // n2
    consts = _dft_consts(n1, n2)
    inv_scale = 1.0 / ((abs_sum[:, :c] + abs_sum[:, c:]) * n)
    filt_f = dft_left(consts["fwd_half"], h.reshape(n1 // 2, n2 * 2 * c))
    xin = z.reshape(n1 // 2, n2 * c)
    if n2 > 1:
        spec = fft_mid(consts, filt_f.reshape(2, n1, n2, 2 * c))
        a = dft_left(consts["fwd_half"], xin)
        b = fft_mid(consts, a.reshape(2, n1, n2, c), spec).reshape(2 * n1, n2 * c)
    else:
        b = dft_left(consts["fwd_half"], xin, filt_f)
    y = dft_inv_post(consts["inv_half"], b, x0.reshape(n1 // 2, n2 * c), xin,
                     jnp.tile(bias.reshape(1, c), (1, n2)), jnp.tile(inv_scale, (1, n2)))
    return y.reshape(length, c)


def _merge_kernel(*refs):
    ys, gs, ws = refs[0:N_BRANCH], refs[N_BRANCH:2 * N_BRANCH], refs[2 * N_BRANCH:3 * N_BRANCH]
    o_ref, wb_ref = refs[3 * N_BRANCH:]

    @pl.when(pl.program_id(1) == 0)
    def _():
        for i in range(N_BRANCH):
            wb_ref[i] = ws[i][...].astype(BF16)

    acc = None
    for i in range(N_BRANCH):
        t = jnp.dot(ys[i][...], wb_ref[i], preferred_element_type=F32) * _sigmoid(gs[i][...])
        acc = t if acc is None else acc + t
    o_ref[...] = acc.astype(o_ref.dtype)


def merge_branches(ys, gates, gate_col0, branch_w, layer):
    m = ys[0].shape[0]
    d = branch_w.shape[3]
    tm = _pick(m, (1280, 1024, 512, 256, 128))
    tn = _pick(d, (512, 256, 128))
    nj = d // tn
    assert gate_col0 % tn == 0
    g0 = gate_col0 // tn
    y_specs = [pl.BlockSpec((tm, MIX_W), lambda j, i: (i, 0))] * N_BRANCH
    g_specs = [pl.BlockSpec((tm, tn), functools.partial(lambda j, i, b: (i, g0 + b * nj + j), b=b)) for b in range(N_BRANCH)]
    w_specs = [pl.BlockSpec((None, None, MIX_W, tn), functools.partial(lambda j, i, b: (layer, b, 0, j), b=b),
                            pipeline_mode=pl.Buffered(1)) for b in range(N_BRANCH)]
    return pl.pallas_call(
        _merge_kernel,
        grid=(nj, m // tm),
        in_specs=y_specs + g_specs + w_specs,
        out_specs=pl.BlockSpec((tm, tn), lambda j, i: (i, j)),
        out_shape=jax.ShapeDtypeStruct((m, d), BF16),
        scratch_shapes=[pltpu.VMEM((N_BRANCH, MIX_W, tn), BF16)],
        compiler_params=_cparams("parallel", "arbitrary"),
        name="merge_branches",
    )(*ys, *([gates] * N_BRANCH), *([branch_w] * N_BRANCH))


FFN_HALO = 16


def _sigmoid(x):
    return 0.5 + 0.5 * jnp.tanh(0.5 * x)


def _ffn_up_kernel(a_ref, wa32_ref, wg32_ref, cwa_ref, cwg_ref, cba_ref, cbg_ref, o_ref, wa_ref, wg_ref,
                   *, tm, chunk, fixes):
    hl = FFN_HALO

    @pl.when(pl.program_id(1) == 0)
    def _():
        wa_ref[...] = wa32_ref[...].astype(BF16)
        wg_ref[...] = wg32_ref[...].astype(BF16)

    def gated(a, rows, no_prev=None, no_next=None):
        def conv(w_ref, cw_ref, cb_ref):
            u = jnp.dot(a, w_ref[...], preferred_element_type=F32)
            up, dn = u[hl - 1:hl - 1 + rows], u[hl + 1:hl + 1 + rows]
            if no_prev is not None:
                up = jnp.where(no_prev, 0.0, up)
                dn = jnp.where(no_next, 0.0, dn)
            return up * cw_ref[0:1, :] + u[hl:hl + rows] * cw_ref[1:2, :] + dn * cw_ref[2:3, :] + cb_ref[...]

        g = conv(wg_ref, cwg_ref, cbg_ref)
        return (g * _sigmoid(g) * conv(wa_ref, cwa_ref, cba_ref)).astype(o_ref.dtype)

    for c in range(tm // chunk):
        o_ref[c * chunk:(c + 1) * chunk, :] = gated(a_ref[c * chunk:(c + 1) * chunk + 2 * hl, :], chunk)

    for blk, g0, first_row in fixes:
        @pl.when(pl.program_id(1) == blk)
        def _(g0=g0, first_row=first_row):
            row = blk * tm + g0 + lax.broadcasted_iota(jnp.int32, (hl, 1), 0)
            o_ref[g0:g0 + hl, :] = gated(a_ref[g0:g0 + 3 * hl, :], hl, row == first_row, row == first_row - 1)


FFN_TM = 1280


def ffn_up(a_ext, m, w_up, layer, conv_w, conv_b, n_latent):
    d = a_ext.shape[1]
    f = w_up.shape[2] // 2
    tm = FFN_TM
    tn = _pick(f, (512, 256, 128))
    chunk = _pick(tm, (640, 256, 128))
    nb, nj = m // tm, f // tn
    te = tm + 2 * FFN_HALO
    assert n_latent % FFN_HALO == 0 and 0 < n_latent < m and a_ext.shape[0] == nb * te
    fixes = []
    for r in (n_latent - 1, n_latent):
        g = r // FFN_HALO * FFN_HALO
        fixes.append((g // tm, g % tm, n_latent))
    wspec = lambda off: pl.BlockSpec((None, d, tn), lambda j, i: (layer, 0, off + j), pipeline_mode=pl.Buffered(1))
    return pl.pallas_call(
        functools.partial(_ffn_up_kernel, tm=tm, chunk=chunk, fixes=tuple(fixes)),
        grid=(nj, nb),
        in_specs=[pl.BlockSpec((te, d), lambda j, i: (i, 0)),
                  wspec(0), wspec(nj),
                  pl.BlockSpec((3, tn), lambda j, i: (0, j)),
                  pl.BlockSpec((3, tn), lambda j, i: (0, nj + j)),
                  pl.BlockSpec((1, tn), lambda j, i: (0, j)),
                  pl.BlockSpec((1, tn), lambda j, i: (0, nj + j))],
        out_specs=pl.BlockSpec((tm, tn), lambda j, i: (i, j)),
        out_shape=jax.ShapeDtypeStruct((m, f), BF16),
        scratch_shapes=[pltpu.VMEM((d, tn), BF16), pltpu.VMEM((d, tn), BF16)],
        compiler_params=_cparams("parallel", "arbitrary"),
        name="ffn_up",
    )(a_ext, w_up, w_up, conv_w, conv_w, conv_b.reshape(1, -1), conv_b.reshape(1, -1))


def kernel(x, c, ctx, c_ctx, ada_w, ada_b, norm1_g, norm2_g, w_in, a_qn_g, a_kn_g, b_rpb,
           c_gate_w2, c_gate_b, c_norm_g, d_conv_w, d_conv_b, d_ffn_w1, d_ffn_b1, d_ffn_w2,
           d_ffn_b2, d_ffn_w3, d_ffn_b3, d_ffn_w4, d_sin_freq, d_bias, branch_w, w_out,
           ffn_up_w, ffn_conv_w, ffn_conv_b, ffn_down, final_norm_g):
    depth = ada_w.shape[0]
    n_lat, d = x.shape[1], x.shape[2]
    n_ctx = ctx.shape[1]
    h = jnp.concatenate([x[0], ctx[0]], axis=0)
    cvec = jnp.concatenate([c[0:1], c_ctx[None, :], jnp.zeros((14, d), F32)], axis=0)
    cvec = (cvec * jax.nn.sigmoid(cvec)).astype(BF16)
    ctab, stab = rope_tables(n_lat, n_ctx)
    zeros_c = jnp.zeros((n_ctx, MIX_W), BF16)

    for i in range(depth):
        ctx_out = i < depth - 1
        mod = matmul(cvec, ada_w, i, 0, ada_w.shape[2], F32, bias=ada_b[i])[0:2]
        sh1, sc1, g1, sh2, sc2, g2 = jnp.split(mod, 6, axis=-1)
        n1 = mod_norm(h, norm1_g[i], sh1, sc1, n_lat, BF16)
        ca_end = IN_CA + 2 * C_GATE_RANK
        w_rest = w_in[i, :, ca_end:][None]
        w_ca = jnp.pad(w_in[i, :, IN_CA:ca_end], ((0, 0), (0, 128 - 2 * C_GATE_RANK)))[None]
        p1 = matmul(n1, w_in, i, 0, PA_WIDTH, F32)
        pb = matmul(n1, w_rest, 0, 0, PB_WIDTH, F32)
        ca = matmul(n1, w_ca, 0, 0, 128, F32)

        q, k, v = a_prep(p1, ctab, stab, a_qn_g[i], a_kn_g[i])
        ya_l = flash_attention(q[:n_lat], k, v, A_KV_HEADS)
        yb_l = neighbourhood_attention(p1, na_bias_table(b_rpb[i]), n_lat, n_ctx)
        yc = gla_mixer(p1, pb, ca, c_gate_w2[i], c_gate_b[i], c_norm_g[i], n_lat, n_ctx)
        x0, z = hyena_pre(pb, d_conv_w[i], d_conv_b[i], n_lat)
        filt = (d_ffn_w1[i], d_ffn_b1[i], d_ffn_w2[i], d_ffn_b2[i], d_ffn_w3[i], d_ffn_b3[i], d_ffn_w4[i],
                d_sin_freq[i])
        hl, sl = hyena_filter(n_lat, *filt)
        yd_l = hyena_long_conv(x0[:n_lat], z[:n_lat], hl, sl, d_bias[i], n_lat)
        if ctx_out:
            ya_c = flash_attention(q[n_lat:], k[n_lat:], v[n_lat:], A_KV_HEADS)
            pc = p1[n_lat:]
            yb_c = flash_attention((pc[:, P_BQ:P_BK] * QK_SCALE).astype(BF16), pc[:, P_BK:P_BV].astype(BF16),
                                   with_ones(pc[:, P_BV:P_CQ].astype(BF16), B_HEADS), B_HEADS)
            hc, sc = hyena_filter(n_ctx, *filt)
            yd_c = hyena_long_conv(x0[n_lat:], z[n_lat:], hc, sc, d_bias[i], n_ctx)
        else:
            ya_c = yb_c = yd_c = zeros_c
        ys = [jnp.concatenate([ya_l, ya_c], axis=0), jnp.concatenate([yb_l, yb_c], axis=0), yc,
              jnp.concatenate([yd_l, yd_c], axis=0)]
        merged = merge_branches(ys, pb, PB_GATE, branch_w, i)
        h = matmul_residual(merged, w_out, i, h, g1, n_lat)
        n2_ext = mod_norm_halo(h, norm2_g[i], sh2, sc2, n_lat, FFN_TM, FFN_HALO)
        act = ffn_up(n2_ext, h.shape[0], ffn_up_w, i, ffn_conv_w[i], ffn_conv_b[i], n_lat)
        h = matmul_residual(act, ffn_down, i, h, g2, n_lat)

    zero2 = jnp.zeros((2, d), F32)
    out = mod_norm(h, final_norm_g, zero2, zero2, n_lat, F32, n_rows=n_lat)
    return out[None]
```

```python
import functools
import math

import numpy as np
import jax
import jax.numpy as jnp
from jax import lax
from jax.experimental import pallas as pl
from jax.experimental.pallas import tpu as pltpu

F32 = jnp.float32
BF16 = jnp.bfloat16

GRID_W = 64
HEAD_DIM = 128
A_HEADS = 4
A_KV_HEADS = 2
ROPE_THETA = 10000.0
B_HEADS = 4
NA_ROWS = 8
NA_COLS = 16
C_HEADS = 4
C_DK = 64
C_DV = 128
C_GATE_RANK = 16
C_GATE_TAU = 16.0
C_CHUNK = 64
MIX_W = 512
N_BRANCH = 4
HY_EMB = 33
HY_DECAY_MIN = math.log(1e-2) / 1.5
HY_DECAY_MAX = math.log(1e-2) / 0.3
NORM_EPS = 1e-6
NEG_BIG = -1e30

P_AQ, P_AK, P_AV = 0, 512, 768
P_BQ, P_BK, P_BV = 1024, 1536, 2048
P_CQ, P_CK, P_CV = 2560, 2816, 3072
PA_WIDTH = 3584
PB_CG, PB_DU = 0, 512
PB_WIDTH = 2048
IN_CA = PA_WIDTH

VMEM_LIMIT = 56 * 1024 * 1024


def _cparams(*sem):
    return pltpu.CompilerParams(dimension_semantics=sem, vmem_limit_bytes=VMEM_LIMIT)


def _pick(n, candidates):
    for c in candidates:
        if n % c == 0:
            return c
    return n


def _modnorm_kernel(h_ref, g_ref, sh_ref, sc_ref, o_ref):
    x = h_ref[...]
    y = x * lax.rsqrt(jnp.mean(x * x, axis=-1, keepdims=True) + NORM_EPS) * g_ref[...]
    o_ref[...] = (y * (1.0 + sc_ref[...]) + sh_ref[...]).astype(o_ref.dtype)


def mod_norm(h, g, shift2, scale2, n_latent, out_dtype, n_rows=None):
    d = h.shape[1]
    tb = _pick(math.gcd(n_latent, h.shape[0]), (256, 128, 64, 8))
    m = n_rows or h.shape[0]
    nl = n_latent // tb
    sel = lambda i: (jnp.where(i >= nl, 1, 0), 0, 0)
    return pl.pallas_call(
        _modnorm_kernel,
        grid=(m // tb,),
        in_specs=[pl.BlockSpec((tb, d), lambda i: (i, 0)),
                  pl.BlockSpec((1, d), lambda i: (0, 0)),
                  pl.BlockSpec((None, 1, d), sel),
                  pl.BlockSpec((None, 1, d), sel)],
        out_specs=pl.BlockSpec((tb, d), lambda i: (i, 0)),
        out_shape=jax.ShapeDtypeStruct((m, d), out_dtype),
        compiler_params=_cparams("parallel"),
        name="mod_norm",
    )(h, g.reshape(1, d), shift2.reshape(2, 1, d), scale2.reshape(2, 1, d))


def _modnorm_halo_kernel(hp_ref, h_ref, hn_ref, g_ref, sh_ref, sc_ref, o_ref, *, tm, halo, n_latent, m):
    i = pl.program_id(0)

    def norm(x, row0):
        row = row0 + lax.broadcasted_iota(jnp.int32, (x.shape[0], 1), 0)
        y = x * lax.rsqrt(jnp.mean(x * x, axis=-1, keepdims=True) + NORM_EPS) * g_ref[...]
        lat = row < n_latent
        y = y * (1.0 + jnp.where(lat, sc_ref[0], sc_ref[1])) + jnp.where(lat, sh_ref[0], sh_ref[1])
        return jnp.where((row >= 0) & (row < m), y, 0.0).astype(o_ref.dtype)

    o_ref[0:halo, :] = norm(hp_ref[...], i * tm - halo)
    step = _pick(tm, (256, 128))
    for c in range(tm // step):
        o_ref[halo + c * step:halo + (c + 1) * step, :] = norm(h_ref[c * step:(c + 1) * step, :], i * tm + c * step)
    o_ref[halo + tm:, :] = norm(hn_ref[...], (i + 1) * tm)


def mod_norm_halo(h, g, shift2, scale2, n_latent, tm, halo):
    m, d = h.shape
    nb = m // tm
    hb = tm // halo
    return pl.pallas_call(
        functools.partial(_modnorm_halo_kernel, tm=tm, halo=halo, n_latent=n_latent, m=m),
        grid=(nb,),
        in_specs=[pl.BlockSpec((halo, d), lambda i: (jnp.maximum(i * hb - 1, 0), 0)),
                  pl.BlockSpec((tm, d), lambda i: (i, 0)),
                  pl.BlockSpec((halo, d), lambda i: (jnp.minimum((i + 1) * hb, m // halo - 1), 0)),
                  pl.BlockSpec((1, d), lambda i: (0, 0)),
                  pl.BlockSpec((2, 1, d), lambda i: (0, 0, 0)),
                  pl.BlockSpec((2, 1, d), lambda i: (0, 0, 0))],
        out_specs=pl.BlockSpec((tm + 2 * halo, d), lambda i: (i, 0)),
        out_shape=jax.ShapeDtypeStruct((nb * (tm + 2 * halo), d), BF16),
        compiler_params=_cparams("parallel"),
        name="mod_norm_halo",
    )(h, h, h, g.reshape(1, d), shift2.reshape(2, 1, d), scale2.reshape(2, 1, d))


VMEM_TILE_BUDGET = 46 * 1024 * 1024


def _dense_tiles(m, k, n, col0, extra_tiles, out_bytes):
    for tms in ((1280, 1024, 640), (512, 320, 256, 128, 16)):
        for tn in (1792, 1024, 512, 256, 128):
            if n % tn or col0 % tn:
                continue
            for tm in tms:
                if m % tm:
                    continue
                blocks = 2 * tm * k * 2 + k * tn * (4 + 2) + 2 * tm * tn * (out_bytes + 4 * extra_tiles)
                if blocks <= VMEM_TILE_BUDGET:
                    return tm, tn
    raise ValueError("no dense tiling fits")


def _cast_at_first_row_block(w_ref, wb_ref):
    @pl.when(pl.program_id(1) == 0)
    def _():
        wb_ref[...] = w_ref[...].astype(BF16)


def _mm_kernel(a_ref, w_ref, o_ref, wb_ref):
    _cast_at_first_row_block(w_ref, wb_ref)
    o_ref[...] = jnp.dot(a_ref[...], wb_ref[...], preferred_element_type=F32).astype(o_ref.dtype)


def _mm_bias_kernel(a_ref, w_ref, b_ref, o_ref, wb_ref):
    _cast_at_first_row_block(w_ref, wb_ref)
    o_ref[...] = (jnp.dot(a_ref[...], wb_ref[...], preferred_element_type=F32) + b_ref[...]).astype(o_ref.dtype)


def matmul(a, w3, layer, col0, n, out_dtype, bias=None):
    m, k = a.shape
    tm, tn = _dense_tiles(m, k, n, col0, extra_tiles=0, out_bytes=jnp.dtype(out_dtype).itemsize)
    assert col0 % tn == 0 and w3.shape[1] == k
    c0 = col0 // tn
    in_specs = [pl.BlockSpec((tm, k), lambda j, i: (i, 0)),
                pl.BlockSpec((None, k, tn), lambda j, i: (layer, 0, c0 + j), pipeline_mode=pl.Buffered(1))]
    args = [a, w3]
    kern = _mm_kernel
    if bias is not None:
        in_specs.append(pl.BlockSpec((1, tn), lambda j, i: (0, j)))
        args.append(bias.reshape(1, n))
        kern = _mm_bias_kernel
    return pl.pallas_call(
        kern,
        grid=(n // tn, m // tm),
        in_specs=in_specs,
        out_specs=pl.BlockSpec((tm, tn), lambda j, i: (i, j)),
        out_shape=jax.ShapeDtypeStruct((m, n), out_dtype),
        scratch_shapes=[pltpu.VMEM((k, tn), BF16)],
        compiler_params=_cparams("parallel", "arbitrary"),
        name="matmul",
    )(*args)


def _mm_shifted_kernel(a_ref, w0_ref, w1_ref, wx_ref, o_ref, wb_ref, *, shift):
    @pl.when(pl.program_id(1) == 0)
    def _():
        x = jnp.concatenate([w0_ref[...], w1_ref[...], wx_ref[...]], axis=1)
        tn = wb_ref.shape[1]
        wb_ref[...] = pltpu.roll(x, x.shape[1] - shift, 1)[:, :tn].astype(BF16)

    o_ref[...] = jnp.dot(a_ref[...], wb_ref[...], preferred_element_type=F32).astype(o_ref.dtype)


def matmul_shifted(a, w3, layer, col0, n, out_dtype):
    m, k = a.shape
    half, lane = 512, 128
    tn = 2 * half
    shift = col0 % lane
    base = col0 - shift
    assert shift and base % half == 0 and n % tn == 0 and col0 + n <= w3.shape[2]
    tm = _pick(m, (1280, 1024, 512, 256, 128))
    b0 = base // half
    single = dict(pipeline_mode=pl.Buffered(1))
    return pl.pallas_call(
        functools.partial(_mm_shifted_kernel, shift=shift),
        grid=(n // tn, m // tm),
        in_specs=[pl.BlockSpec((tm, k), lambda j, i: (i, 0)),
                  pl.BlockSpec((None, k, half), lambda j, i: (layer, 0, b0 + 2 * j), **single),
                  pl.BlockSpec((None, k, half), lambda j, i: (layer, 0, b0 + 2 * j + 1), **single),
                  pl.BlockSpec((None, k, lane), lambda j, i: (layer, 0, base // lane + (j + 1) * (tn // lane)), **single)],
        out_specs=pl.BlockSpec((tm, tn), lambda j, i: (i, j)),
        out_shape=jax.ShapeDtypeStruct((m, n), out_dtype),
        scratch_shapes=[pltpu.VMEM((k, tn), BF16)],
        compiler_params=_cparams("parallel", "arbitrary"),
        name="matmul_shifted",
    )(a, w3, w3, w3)


def _mm_res_kernel(a_ref, w_ref, h_ref, g_ref, o_ref, wb_ref, *, n_latent, tm):
    _cast_at_first_row_block(w_ref, wb_ref)
    row = pl.program_id(1) * tm + lax.broadcasted_iota(jnp.int32, (tm, 1), 0)
    gate = jnp.where(row < n_latent, g_ref[0], g_ref[1])
    o_ref[...] = h_ref[...] + gate * jnp.dot(a_ref[...], wb_ref[...], preferred_element_type=F32)


def matmul_residual(a, w3, layer, h, gate2, n_latent):
    m, k = a.shape
    n = w3.shape[2]
    tm, tn = _dense_tiles(m, k, n, 0, extra_tiles=1, out_bytes=4)
    return pl.pallas_call(
        functools.partial(_mm_res_kernel, n_latent=n_latent, tm=tm),
        grid=(n // tn, m // tm),
        in_specs=[pl.BlockSpec((tm, k), lambda j, i: (i, 0)),
                  pl.BlockSpec((None, k, tn), lambda j, i: (layer, 0, j), pipeline_mode=pl.Buffered(1)),
                  pl.BlockSpec((tm, tn), lambda j, i: (i, j)),
                  pl.BlockSpec((2, 1, tn), lambda j, i: (0, 0, j))],
        out_specs=pl.BlockSpec((tm, tn), lambda j, i: (i, j)),
        out_shape=jax.ShapeDtypeStruct((m, n), F32),
        scratch_shapes=[pltpu.VMEM((k, tn), BF16)],
        compiler_params=_cparams("parallel", "arbitrary"),
        name="matmul_residual",
    )(a, w3, h, gate2.reshape(2, 1, n))


def _swap_halves(y):
    lane = lax.broadcasted_iota(jnp.int32, y.shape, 1)
    return jnp.where((lane % 64) < 32, pltpu.roll(y, 96, 1), pltpu.roll(y, 32, 1))


def _aprep_kernel(p_ref, cos_ref, sin_ref, qg_ref, kg_ref, q_ref, k_ref, v_ref):
    c = cos_ref[...]
    s = sin_ref[...]

    def norm_rope(x, g):
        y = x * lax.rsqrt(jnp.mean(x * x, axis=-1, keepdims=True) + NORM_EPS) * g
        return y * c + _swap_halves(y) * s

    for hd in range(A_HEADS):
        x = p_ref[:, hd * HEAD_DIM:(hd + 1) * HEAD_DIM]
        q_ref[:, hd * HEAD_DIM:(hd + 1) * HEAD_DIM] = (
            norm_rope(x, qg_ref[...]) * QK_SCALE).astype(q_ref.dtype)
    for hd in range(A_KV_HEADS):
        x = p_ref[:, P_AK + hd * HEAD_DIM:P_AK + (hd + 1) * HEAD_DIM]
        k_ref[:, hd * HEAD_DIM:(hd + 1) * HEAD_DIM] = norm_rope(x, kg_ref[...]).astype(k_ref.dtype)
        v = p_ref[:, P_AV + hd * HEAD_DIM:P_AV + (hd + 1) * HEAD_DIM]
        v_ref[:, 2 * hd * HEAD_DIM:(2 * hd + 1) * HEAD_DIM] = v.astype(v_ref.dtype)
        v_ref[:, (2 * hd + 1) * HEAD_DIM:(2 * hd + 2) * HEAD_DIM] = jnp.ones_like(v).astype(v_ref.dtype)


def rope_tables(n_latent, n_ctx):
    pos = np.arange(n_latent)
    axes = np.stack([pos // GRID_W, pos % GRID_W], axis=-1).astype(np.float32)
    quarter = HEAD_DIM // 4
    inv_freq = (np.float32(ROPE_THETA) ** (-np.arange(quarter, dtype=np.float32) / quarter)).astype(np.float32)
    ang = (axes[:, :, None] * inv_freq).astype(np.float32)
    cos, sin = np.cos(ang), np.sin(ang)
    ctab = np.concatenate([cos[:, 0], cos[:, 0], cos[:, 1], cos[:, 1]], axis=-1)
    stab = np.concatenate([-sin[:, 0], sin[:, 0], -sin[:, 1], sin[:, 1]], axis=-1)
    ctab = np.concatenate([ctab, np.ones((n_ctx, HEAD_DIM), np.float32)], axis=0)
    stab = np.concatenate([stab, np.zeros((n_ctx, HEAD_DIM), np.float32)], axis=0)
    return jnp.asarray(ctab, F32), jnp.asarray(stab, F32)


def a_prep(p1, ctab, stab, qn_g, kn_g):
    m = p1.shape[0]
    tb = _pick(m, (256, 128, 64, 8))
    wa = 1024
    return pl.pallas_call(
        _aprep_kernel,
        grid=(m // tb,),
        in_specs=[pl.BlockSpec((tb, wa), lambda i: (i, 0)),
                  pl.BlockSpec((tb, HEAD_DIM), lambda i: (i, 0)),
                  pl.BlockSpec((tb, HEAD_DIM), lambda i: (i, 0)),
                  pl.BlockSpec((1, HEAD_DIM), lambda i: (0, 0)),
                  pl.BlockSpec((1, HEAD_DIM), lambda i: (0, 0))],
        out_specs=[pl.BlockSpec((tb, 512), lambda i: (i, 0)),
                   pl.BlockSpec((tb, 256), lambda i: (i, 0)),
                   pl.BlockSpec((tb, 512), lambda i: (i, 0))],
        out_shape=[jax.ShapeDtypeStruct((m, 512), BF16),
                   jax.ShapeDtypeStruct((m, 256), BF16),
                   jax.ShapeDtypeStruct((m, 512), BF16)],
        compiler_params=_cparams("parallel"),
        name="a_prep",
    )(p1, ctab, stab, qn_g.reshape(1, HEAD_DIM), kn_g.reshape(1, HEAD_DIM))


LOG2E = 1.4426950408889634
QK_SCALE = HEAD_DIM ** -0.5 * LOG2E


def _flash_kernel(q_ref, k_ref, v_ref, o_ref, m_ref, acc_ref, sa_ref, sb_ref, *, group, tk, nk):
    m_ref[...] = jnp.full_like(m_ref, NEG_BIG)
    acc_ref[...] = jnp.zeros_like(acc_ref)

    def scores(j, s_ref):
        k = k_ref[pl.ds(pl.multiple_of(j * tk, tk), tk), :]
        for g in range(group):
            q = q_ref[:, g * HEAD_DIM:(g + 1) * HEAD_DIM]
            s_ref[g] = lax.dot_general(q, k, (((1,), (1,)), ((), ())), preferred_element_type=F32)

    def update(j, s_ref):
        v = v_ref[pl.ds(pl.multiple_of(j * tk, tk), tk), :]
        for g in range(group):
            s = s_ref[g]
            m_prev = m_ref[g]
            m_new = jnp.maximum(m_prev, jnp.max(s, axis=-1, keepdims=True))
            p = jnp.exp2(s - m_new).astype(BF16)
            acc_ref[g] = jnp.exp2(m_prev - m_new) * acc_ref[g] + jnp.dot(p, v, preferred_element_type=F32)
            m_ref[g] = m_new

    scores(0, sa_ref)
    pairs = (nk - 1) // 2

    def body(jj, carry):
        j = 2 * jj
        scores(j + 1, sb_ref)
        update(j, sa_ref)
        scores(j + 2, sa_ref)
        update(j + 1, sb_ref)
        return carry

    lax.fori_loop(0, pairs, body, 0)
    if (nk - 1) % 2 == 1:
        scores(nk - 1, sb_ref)
        update(nk - 2, sa_ref)
        update(nk - 1, sb_ref)
    else:
        update(nk - 1, sa_ref)
    for g in range(group):
        acc = acc_ref[g]
        o_ref[:, g * HEAD_DIM:(g + 1) * HEAD_DIM] = (acc[:, :HEAD_DIM] / acc[:, HEAD_DIM:]).astype(o_ref.dtype)


def with_ones(v, heads):
    n = v.shape[0]
    v3 = v.reshape(n, heads, HEAD_DIM)
    return jnp.concatenate([v3, jnp.ones_like(v3)], axis=-1).reshape(n, heads * 2 * HEAD_DIM)


def flash_attention(q, k, v1, kv_heads, tq=None, tk=None):
    lq, hq = q.shape
    lk = k.shape[0]
    group = hq // (kv_heads * HEAD_DIM)
    tq = tq or _pick(lq, (1024, 512, 256, 128))
    tk = tk or _pick(lk, (1280, 1024, 512, 256, 128))
    nk = lk // tk
    gw = group * HEAD_DIM
    return pl.pallas_call(
        functools.partial(_flash_kernel, group=group, tk=tk, nk=nk),
        grid=(kv_heads, lq // tq),
        in_specs=[pl.BlockSpec((tq, gw), lambda h, i: (i, h)),
                  pl.BlockSpec((lk, HEAD_DIM), lambda h, i: (0, h), pipeline_mode=pl.Buffered(1)),
                  pl.BlockSpec((lk, 2 * HEAD_DIM), lambda h, i: (0, h), pipeline_mode=pl.Buffered(1))],
        out_specs=pl.BlockSpec((tq, gw), lambda h, i: (i, h)),
        out_shape=jax.ShapeDtypeStruct((lq, hq), BF16),
        scratch_shapes=[pltpu.VMEM((group, tq, 1), F32),
                        pltpu.VMEM((group, tq, 2 * HEAD_DIM), F32),
                        pltpu.VMEM((group, tq, tk), F32),
                        pltpu.VMEM((group, tq, tk), F32)],
        compiler_params=_cparams("parallel", "parallel"),
        name="flash_attention",
    )(q, k, v1)


NA_GROUP = 4


def na_bias_table(rpb):
    col = np.arange(GRID_W)
    c_start = np.clip(col - NA_COLS // 2, 0, GRID_W - NA_COLS)
    kc = np.arange(GRID_W)
    col_ok = (kc[None, :] >= c_start[:, None]) & (kc[None, :] < c_start[:, None] + NA_COLS)
    dc = np.clip(kc[None, :] - col[:, None] + (NA_COLS - 1), 0, 2 * NA_COLS - 2)
    i = np.arange(NA_GROUP)
    j = np.arange(3 * NA_GROUP)
    dr = np.clip(j[None, :] - i[:, None] + NA_ROWS // 2 - 1, 0, 2 * NA_ROWS - 2)
    first = np.broadcast_to((j >= NA_GROUP)[None, :], dr.shape)
    inner = (j[None, :] >= i[:, None]) & (j[None, :] < i[:, None] + NA_ROWS)
    last = np.broadcast_to((j < NA_ROWS)[None, :], dr.shape)
    bias = rpb.astype(F32)[:, dr][:, :, :, dc]
    tabs = []
    for row_ok in (first, inner, last):
        ok = row_ok[:, :, None, None] & col_ok[None, None, :, :]
        t = jnp.where(jnp.asarray(ok)[None], bias * LOG2E, NEG_BIG)
        tabs.append(t.transpose(0, 1, 3, 2, 4).reshape(rpb.shape[0], NA_GROUP * GRID_W, 3 * NA_GROUP * GRID_W))
    return jnp.stack(tabs)


def _na_kernel(q_ref, k0_ref, k1_ref, k2_ref, v0_ref, v1_ref, v2_ref, kc_ref, vc_ref, b_ref, o_ref):
    k_u = jnp.concatenate([r[...].astype(BF16) for r in (k0_ref, k1_ref, k2_ref)], axis=0)
    v_u = jnp.concatenate([r[...].astype(BF16) for r in (v0_ref, v1_ref, v2_ref)], axis=0)
    k_c = kc_ref[...].astype(BF16)
    v_c = vc_ref[...].astype(BF16)
    nt = (((1,), (1,)), ((), ()))
    for hd in range(B_HEADS):
        sl = slice(hd * HEAD_DIM, (hd + 1) * HEAD_DIM)
        q = (q_ref[:, sl] * QK_SCALE).astype(BF16)
        s_loc = lax.dot_general(q, k_u[:, sl], nt, preferred_element_type=F32) + b_ref[hd]
        s_ctx = lax.dot_general(q, k_c[:, sl], nt, preferred_element_type=F32)
        mx = jnp.maximum(jnp.max(s_loc, axis=-1, keepdims=True), jnp.max(s_ctx, axis=-1, keepdims=True))
        p_loc = jnp.exp2(s_loc - mx)
        p_ctx = jnp.exp2(s_ctx - mx)
        den = jnp.sum(p_loc, axis=-1, keepdims=True) + jnp.sum(p_ctx, axis=-1, keepdims=True)
        o = (jnp.dot(p_loc.astype(BF16), v_u[:, sl], preferred_element_type=F32)
             + jnp.dot(p_ctx.astype(BF16), v_c[:, sl], preferred_element_type=F32))
        o_ref[:, sl] = (o / den).astype(o_ref.dtype)


def neighbourhood_attention(p1, bias_tab, n_latent, n_ctx):
    rows = n_latent // GRID_W
    ng = rows // NA_GROUP
    assert rows % NA_GROUP == 0 and rows >= 2 * NA_ROWS and NA_ROWS == 2 * NA_GROUP
    w = B_HEADS * HEAD_DIM
    t = NA_GROUP * GRID_W
    cq, ck, cv = P_BQ // w, P_BK // w, P_BV // w

    def kv_specs(col):
        return [pl.BlockSpec((t, w), functools.partial(lambda g, d, col: (jnp.clip(g + d, 0, ng - 1), col), d=d, col=col))
                for d in (-1, 0, 1)]

    cblk = n_latent // n_ctx
    return pl.pallas_call(
        _na_kernel,
        grid=(ng,),
        in_specs=[pl.BlockSpec((t, w), lambda g: (g, cq))] + kv_specs(ck) + kv_specs(cv) + [
            pl.BlockSpec((n_ctx, w), lambda g: (cblk, ck)),
            pl.BlockSpec((n_ctx, w), lambda g: (cblk, cv)),
            pl.BlockSpec((None,) + bias_tab.shape[1:],
                         lambda g: (jnp.where(g == 0, 0, jnp.where(g == ng - 1, 2, 1)), 0, 0, 0))],
        out_specs=pl.BlockSpec((t, w), lambda g: (g, 0)),
        out_shape=jax.ShapeDtypeStruct((n_latent, w), BF16),
        compiler_params=_cparams("parallel"),
        name="neighbourhood_attention",
    )(*([p1] * 9), bias_tab)


GLA_BLOCK = 4 * C_CHUNK


def _gla_block(q_ref, k_ref, v_ref, a_ref, w2_ref, gb_ref, st_ref, reverse, emit):
    c = C_CHUNK
    nch = GLA_BLOCK // c
    x = jnp.dot(a_ref[...].astype(BF16), w2_ref[...], preferred_element_type=F32) + gb_ref[...]
    la = (jnp.minimum(x, 0.0) - jnp.log(1.0 + jnp.exp(-jnp.abs(x)))) * (1.0 / C_GATE_TAU)
    pos = lax.broadcasted_iota(jnp.int32, la.shape, 0) % c
    b = la
    sh = 1
    while sh < c:
        if reverse:
            b = b + jnp.where(pos < c - sh, pltpu.roll(b, GLA_BLOCK - sh, 0), 0.0)
        else:
            b = b + jnp.where(pos >= sh, pltpu.roll(b, sh, 0), 0.0)
        sh *= 2
    qs = q_ref[...] * (C_DK ** -0.5)
    kk = k_ref[...]
    q_in = (qs * jnp.exp(b)).astype(BF16)
    ii = lax.broadcasted_iota(jnp.int32, (c, c), 0)
    jj = lax.broadcasted_iota(jnp.int32, (c, c), 1)
    mask = (jj >= ii) if reverse else (jj <= ii)
    nt = (((1,), (1,)), ((), ()))
    tn = (((0,), (0,)), ((), ()))
    states = [st_ref[hd] for hd in range(C_HEADS)]
    for ch in (range(nch - 1, -1, -1) if reverse else range(nch)):
        rows = slice(ch * c, (ch + 1) * c)
        bc = b[rows]
        b_mid = bc[c // 2:c // 2 + 1, :]
        b_last = bc[0:1, :] if reverse else bc[c - 1:c, :]
        q_t = (qs[rows] * jnp.exp(bc - b_mid)).astype(BF16)
        k_t = (kk[rows] * jnp.exp(b_mid - bc)).astype(BF16)
        k_d = (kk[rows] * jnp.exp(b_last - bc)).astype(BF16)
        e_last = jnp.exp(b_last)
        for hd in range(C_HEADS):
            ks = slice(hd * C_DK, (hd + 1) * C_DK)
            v = v_ref[rows, hd * C_DV:(hd + 1) * C_DV].astype(BF16)
            att = lax.dot_general(q_t[:, ks], k_t[:, ks], nt, preferred_element_type=F32)
            att = jnp.where(mask, att, 0.0).astype(BF16)
            o = (lax.dot_general(q_in[rows, ks], states[hd].astype(BF16), nt, preferred_element_type=F32)
                 + jnp.dot(att, v, preferred_element_type=F32))
            states[hd] = e_last[:, ks] * states[hd] + lax.dot_general(v, k_d[:, ks], tn, preferred_element_type=F32)
            emit(rows, hd, o)
    for hd in range(C_HEADS):
        st_ref[hd] = states[hd]


def _gla_fwd_kernel(q_ref, k_ref, v_ref, a_ref, w2_ref, gb_ref, o_ref, st_ref):
    @pl.when(pl.program_id(0) == 0)
    def _():
        st_ref[...] = jnp.zeros_like(st_ref)

    def emit(rows, hd, o):
        o_ref[rows, hd * C_DV:(hd + 1) * C_DV] = o

    _gla_block(q_ref, k_ref, v_ref, a_ref, w2_ref, gb_ref, st_ref, False, emit)


def _gla_bwd_kernel(q_ref, k_ref, v_ref, a_ref, w2_ref, gb_ref, of_ref, g_ref, ng_ref, y_ref, st_ref):
    @pl.when(pl.program_id(0) == 0)
    def _():
        st_ref[...] = jnp.zeros_like(st_ref)

    def emit(rows, hd, o):
        vs = slice(hd * C_DV, (hd + 1) * C_DV)
        x = of_ref[rows, vs] + o
        g = g_ref[rows, vs]
        y = x * lax.rsqrt(jnp.mean(x * x, axis=-1, keepdims=True) + NORM_EPS) * ng_ref[...]
        y_ref[rows, vs] = (y * (g * _sigmoid(g))).astype(y_ref.dtype)

    _gla_block(q_ref, k_ref, v_ref, a_ref, w2_ref, gb_ref, st_ref, True, emit)


def gla_mixer(p1, pb, ca, gate_w2, gate_b, norm_g, n_latent, n_ctx):
    m = p1.shape[0]
    c = GLA_BLOCK
    assert n_latent % c == 0 and n_ctx % c == 0
    nl, nc = n_latent // c, n_ctx // c
    kw = C_HEADS * C_DK
    vw = C_HEADS * C_DV

    def blk_f(s):
        return jnp.where(s < nc, nl + s, s - nc)

    def blk_r(s):
        return nl + nc - 1 - s

    w2 = jnp.zeros((2, 128, kw), F32)
    w2 = w2.at[0, 0:C_GATE_RANK].set(gate_w2[0]).at[1, C_GATE_RANK:2 * C_GATE_RANK].set(gate_w2[1]).astype(BF16)
    gb = gate_b.reshape(2, 1, kw).astype(F32)

    def common_specs(blk):
        return [pl.BlockSpec((c, kw), lambda s: (blk(s), P_CQ // kw)),
                pl.BlockSpec((c, kw), lambda s: (blk(s), P_CK // kw)),
                pl.BlockSpec((c, vw), lambda s: (blk(s), P_CV // vw)),
                pl.BlockSpec((c, 128), lambda s: (blk(s), 0))]

    st_scratch = [pltpu.VMEM((C_HEADS, C_DV, C_DK), F32)]
    o_f = pl.pallas_call(
        _gla_fwd_kernel,
        grid=(nl + nc,),
        in_specs=common_specs(blk_f) + [pl.BlockSpec((None, 128, kw), lambda s: (0, 0, 0)),
                                        pl.BlockSpec((None, 1, kw), lambda s: (0, 0, 0))],
        out_specs=pl.BlockSpec((c, vw), lambda s: (blk_f(s), 0)),
        out_shape=jax.ShapeDtypeStruct((m, vw), F32),
        scratch_shapes=st_scratch,
        compiler_params=_cparams("arbitrary"),
        name="gla_forward",
    )(p1, p1, p1, ca, w2, gb)
    y = pl.pallas_call(
        _gla_bwd_kernel,
        grid=(nl + nc,),
        in_specs=common_specs(blk_r) + [pl.BlockSpec((None, 128, kw), lambda s: (1, 0, 0)),
                                        pl.BlockSpec((None, 1, kw), lambda s: (1, 0, 0)),
                                        pl.BlockSpec((c, vw), lambda s: (blk_r(s), 0)),
                                        pl.BlockSpec((c, vw), lambda s: (blk_r(s), PB_CG // vw)),
                                        pl.BlockSpec((1, C_DV), lambda s: (0, 0))],
        out_specs=pl.BlockSpec((c, vw), lambda s: (blk_r(s), 0)),
        out_shape=jax.ShapeDtypeStruct((m, vw), BF16),
        scratch_shapes=st_scratch,
        compiler_params=_cparams("arbitrary"),
        name="gla_reverse",
    )(p1, p1, p1, ca, w2, gb, o_f, pb, norm_g.reshape(1, C_DV))
    return y


def _conv3_rows(x, prev_row, next_row, w_ref, b_ref):
    t = x.shape[0]
    row = lax.broadcasted_iota(jnp.int32, x.shape, 0)
    up = jnp.where(row == 0, prev_row, pltpu.roll(x, 1, 0))
    dn = jnp.where(row == t - 1, next_row, pltpu.roll(x, t - 1, 0))
    return up * w_ref[0:1, :] + x * w_ref[1:2, :] + dn * w_ref[2:3, :] + b_ref[...]


def _hy_pre_kernel(*refs, tb, seq_starts, seq_ends):
    mains, prevs, nexts = refs[0:3], refs[3:6], refs[6:9]
    w_refs, b_refs = refs[9:12], refs[12:15]
    x0_ref, z_ref = refs[15:17]
    r0 = pl.program_id(0) * tb
    pvalid = jnp.where(functools.reduce(jnp.logical_or, [r0 == s for s in seq_starts]), 0.0, 1.0)
    nvalid = jnp.where(functools.reduce(jnp.logical_or, [r0 + tb == e for e in seq_ends]), 0.0, 1.0)
    res = []
    for t in range(3):
        res.append(_conv3_rows(mains[t][...], prevs[t][7:8, :] * pvalid, nexts[t][0:1, :] * nvalid,
                               w_refs[t], b_refs[t]))
    x0_ref[...] = res[0]
    z_ref[...] = res[2] * res[1]


def hyena_pre(p1, conv_w, conv_b, n_latent):
    m = p1.shape[0]
    w = MIX_W
    tb = _pick(math.gcd(n_latent, m), (256, 128, 64, 8))
    nb8 = m // 8
    c0 = PB_DU // w
    mains = [pl.BlockSpec((tb, w), functools.partial(lambda i, t: (i, c0 + t), t=t)) for t in range(3)]
    prevs = [pl.BlockSpec((8, w), functools.partial(lambda i, t: (jnp.maximum(i * (tb // 8) - 1, 0), c0 + t), t=t))
             for t in range(3)]
    nexts = [pl.BlockSpec((8, w), functools.partial(lambda i, t: (jnp.minimum((i + 1) * (tb // 8), nb8 - 1), c0 + t), t=t))
             for t in range(3)]
    wspecs = [pl.BlockSpec((3, w), functools.partial(lambda i, t: (0, t), t=t)) for t in range(3)]
    bspecs = [pl.BlockSpec((1, w), functools.partial(lambda i, t: (0, t), t=t)) for t in range(3)]
    return pl.pallas_call(
        functools.partial(_hy_pre_kernel, tb=tb, seq_starts=(0, n_latent), seq_ends=(n_latent, m)),
        grid=(m // tb,),
        in_specs=mains + prevs + nexts + wspecs + bspecs,
        out_specs=[pl.BlockSpec((tb, w), lambda i: (i, 0))] * 2,
        out_shape=[jax.ShapeDtypeStruct((m, w), F32)] * 2,
        compiler_params=_cparams("parallel"),
        name="hyena_pre",
    )(*([p1] * 9), *([conv_w] * 3), *([conv_b.reshape(1, -1)] * 3))


def _hy_filter_kernel(z_ref, w1_ref, b1_ref, w2_ref, b2_ref, w3_ref, b3_ref, w4_ref, fr_ref, dl_ref,
                      h_ref, s_ref):
    def dot3(x, w):
        xh, wh = x.astype(BF16), w.astype(BF16)
        xl, wl = (x - xh.astype(F32)).astype(BF16), (w - wh.astype(F32)).astype(BF16)
        return (jnp.dot(xh, wh, preferred_element_type=F32) + jnp.dot(xh, wl, preferred_element_type=F32)
                + jnp.dot(xl, wh, preferred_element_type=F32))

    z = z_ref[...]
    fr = fr_ref[...]
    hid = jnp.sin(fr * (dot3(z, w1_ref[...]) + b1_ref[...]))
    hid = jnp.sin(fr * (dot3(hid, w2_ref[...]) + b2_ref[...]))
    hid = jnp.sin(fr * (dot3(hid, w3_ref[...]) + b3_ref[...]))
    h = dot3(hid, w4_ref[...])
    h = h * jnp.exp(-z[:, 0:1] * dl_ref[...])
    row = pl.program_id(0) * h.shape[0] + lax.broadcasted_iota(jnp.int32, h.shape, 0)
    col = lax.broadcasted_iota(jnp.int32, h.shape, 1)
    h_ref[...] = jnp.where((row == 0) & (col >= MIX_W), 0.0, h)

    @pl.when(pl.program_id(0) == 0)
    def _():
        s_ref[...] = jnp.zeros_like(s_ref)

    s_ref[...] += jnp.sum(jnp.abs(h), axis=0, keepdims=True)


def hyena_filter(length, w1, b1, w2, b2, w3, b3, w4, freq):
    bands = (HY_EMB - 1) // 2
    t = np.linspace(0.0, 1.0, length, dtype=np.float32)[:, None]
    wv = (np.float32(2.0 * math.pi) * np.arange(length, dtype=np.float32)[:, None] / np.float32(length)).astype(np.float32)
    f = np.linspace(1e-4, bands - 1, bands, dtype=np.float32)
    zt = np.concatenate([t, np.cos(f * wv), -np.sin(f * wv)], axis=-1).astype(np.float32)
    z = np.zeros((length, 128), np.float32)
    z[:, :HY_EMB] = zt
    deltas = np.abs(np.linspace(HY_DECAY_MIN, HY_DECAY_MAX, MIX_W, dtype=np.float32))
    dl = np.concatenate([deltas, deltas])[None, :].astype(np.float32)

    def pad2(a, r, c):
        return jnp.zeros((r, c), F32).at[:a.shape[0], :a.shape[1]].set(a.astype(F32))

    hf = w1.shape[1]
    tb = _pick(length, (512, 256, 128, 8))
    full = lambda i: (0, 0)
    return pl.pallas_call(
        _hy_filter_kernel,
        grid=(length // tb,),
        in_specs=[pl.BlockSpec((tb, 128), lambda i: (i, 0)),
                  pl.BlockSpec((128, 128), full), pl.BlockSpec((1, 128), full),
                  pl.BlockSpec((128, 128), full), pl.BlockSpec((1, 128), full),
                  pl.BlockSpec((128, 128), full), pl.BlockSpec((1, 128), full),
                  pl.BlockSpec((128, 2 * MIX_W), full), pl.BlockSpec((1, 128), full),
                  pl.BlockSpec((1, 2 * MIX_W), full)],
        out_specs=[pl.BlockSpec((tb, 2 * MIX_W), lambda i: (i, 0)),
                   pl.BlockSpec((1, 2 * MIX_W), full)],
        out_shape=[jax.ShapeDtypeStruct((length, 2 * MIX_W), F32),
                   jax.ShapeDtypeStruct((1, 2 * MIX_W), F32)],
        compiler_params=_cparams("arbitrary"),
        name="hyena_filter",
    )(jnp.asarray(z), pad2(w1, 128, 128), pad2(b1[None], 1, 128), pad2(w2, 128, 128), pad2(b2[None], 1, 128),
      pad2(w3, 128, 128), pad2(b3[None], 1, 128), pad2(w4, 128, 2 * MIX_W), pad2(freq[None], 1, 128),
      jnp.asarray(dl))


def _split_hi_lo(m):
    m = np.asarray(m, np.float32)
    hi = jnp.asarray(m, F32).astype(BF16)
    lo = (jnp.asarray(m, F32) - hi.astype(F32)).astype(BF16)
    return hi, lo


def _dft_rows(n1):
    return -(-(n1 // 2 + 1) // 8) * 8


def _dft_consts(n1, n2):
    n = n1 * n2
    hf = n1 // 2
    nh = _dft_rows(n1)
    k = np.arange(nh, dtype=np.int64)
    kept = (k <= hf)[:, None]
    ang1 = 2.0 * np.pi * ((k[:, None] * np.arange(hf, dtype=np.int64)[None, :]) % n1) / n1
    c1, s1 = np.where(kept, np.cos(ang1), 0.0), np.where(kept, np.sin(ang1), 0.0)
    fwd_half = np.concatenate([c1, -s1], axis=0)
    wgt = np.where((k == 0) | (k == hf), 1.0, 2.0)[:, None]
    inv_half = np.concatenate([(wgt * c1).T, -(wgt * s1).T], axis=1)
    k2 = np.arange(n2, dtype=np.int64)
    ang2 = 2.0 * np.pi * ((k2[:, None] * k2[None, :]) % n2) / n2
    c2, s2 = np.cos(ang2), np.sin(ang2)
    m2 = np.block([[c2, s2], [-s2, c2]])
    angt = 2.0 * np.pi * ((k[:, None] * k2[None, :]) % n) / n
    twr = np.cos(angt).astype(np.float32)[:, :, None]
    twi = (-np.sin(angt)).astype(np.float32)[:, :, None]
    return dict(fwd_half=_split_hi_lo(fwd_half), inv_half=_split_hi_lo(inv_half),
                m2=_split_hi_lo(m2), m2inv=_split_hi_lo(m2.T), twr=jnp.asarray(twr), twi=jnp.asarray(twi))


def _dot3(mh, ml, x):
    xh = x.astype(BF16)
    xl = (x - xh.astype(F32)).astype(BF16)
    return (jnp.dot(mh, xh, preferred_element_type=F32) + jnp.dot(ml, xh, preferred_element_type=F32)
            + jnp.dot(mh, xl, preferred_element_type=F32))


def _dft_left_kernel(mh_ref, ml_ref, x_ref, o_ref):
    o_ref[...] = _dot3(mh_ref[...], ml_ref[...], x_ref[...])


def _filter_spectrum(fr, fi, c):
    return fr[:, :c] + fr[:, c:], fi[:, :c] - fi[:, c:]


def _dft_left_mul_kernel(mh_ref, ml_ref, x_ref, f_ref, o_ref):
    a = _dot3(mh_ref[...], ml_ref[...], x_ref[...])
    r = a.shape[0] // 2
    ar, ai = a[:r], a[r:]
    kr, ki = _filter_spectrum(f_ref[0:r, :], f_ref[r:, :], a.shape[1])
    o_ref[0:r, :] = ar * kr - ai * ki
    o_ref[r:, :] = ar * ki + ai * kr


def dft_left(mats, x, filt_f=None):
    mh, ml = mats
    r, k = mh.shape
    w = x.shape[1]
    tc = _pick(w, (2048, 1024, 512))
    in_specs = [pl.BlockSpec((r, k), lambda j: (0, 0)), pl.BlockSpec((r, k), lambda j: (0, 0)),
                pl.BlockSpec((k, tc), lambda j: (0, j))]
    args = [mh, ml, x]
    kern = _dft_left_kernel
    if filt_f is not None:
        assert w == tc and filt_f.shape == (r, 2 * w)
        in_specs.append(pl.BlockSpec((r, 2 * w), lambda j: (0, 0)))
        args.append(filt_f)
        kern = _dft_left_mul_kernel
    return pl.pallas_call(
        kern,
        grid=(w // tc,),
        in_specs=in_specs,
        out_specs=pl.BlockSpec((r, tc), lambda j: (0, j)),
        out_shape=jax.ShapeDtypeStruct((r, w), F32),
        compiler_params=_cparams("parallel"),
        name="dft_left",
    )(*args)


def _dft_inv_post_kernel(mh_ref, ml_ref, b_ref, x0_ref, z_ref, bias_ref, scale_ref, o_ref):
    conv = _dot3(mh_ref[...], ml_ref[...], b_ref[...]) * scale_ref[...]
    o_ref[...] = (x0_ref[...] * (conv + z_ref[...] * bias_ref[...])).astype(o_ref.dtype)


def dft_inv_post(mats, b, x0, z, bias_t, scale_t):
    mh, ml = mats
    r, k = mh.shape
    w = b.shape[1]
    tc = _pick(w, (2048, 1024, 512))
    return pl.pallas_call(
        _dft_inv_post_kernel,
        grid=(w // tc,),
        in_specs=[pl.BlockSpec((r, k), lambda j: (0, 0)), pl.BlockSpec((r, k), lambda j: (0, 0)),
                  pl.BlockSpec((k, tc), lambda j: (0, j)),
                  pl.BlockSpec((r, tc), lambda j: (0, j)), pl.BlockSpec((r, tc), lambda j: (0, j)),
                  pl.BlockSpec((1, tc), lambda j: (0, j)), pl.BlockSpec((1, tc), lambda j: (0, j))],
        out_specs=pl.BlockSpec((r, tc), lambda j: (0, j)),
        out_shape=jax.ShapeDtypeStruct((r, w), BF16),
        compiler_params=_cparams("parallel"),
        name="dft_inv_post",
    )(mh, ml, b, x0, z, bias_t, scale_t)


def _fft_mid_kernel(*refs, n2, with_filter):
    if with_filter:
        a_ref, twr_ref, twi_ref, mh_ref, ml_ref, ih_ref, il_ref, k_ref, o_ref = refs
    else:
        a_ref, twr_ref, twi_ref, mh_ref, ml_ref, o_ref = refs
    for s in range(a_ref.shape[1]):
        twr, twi = twr_ref[s], twi_ref[s]
        ar, ai = a_ref[0, s], a_ref[1, s]
        t = jnp.concatenate([twr * ar - twi * ai, twr * ai + twi * ar], axis=0)
        x = _dot3(mh_ref[...], ml_ref[...], t)
        xr, xi = x[:n2], x[n2:]
        if not with_filter:
            o_ref[0, s], o_ref[1, s] = _filter_spectrum(xr, xi, x.shape[1] // 2)
            continue
        kr, ki = k_ref[0, s], k_ref[1, s]
        y = jnp.concatenate([xr * kr - xi * ki, xr * ki + xi * kr], axis=0)
        b = _dot3(ih_ref[...], il_ref[...], y)
        br, bi = b[:n2], b[n2:]
        o_ref[0, s] = twr * br + twi * bi
        o_ref[1, s] = twr * bi - twi * br


def fft_mid(consts, a, spectrum=None):
    _, n1, n2, c = a.shape
    c_out = c if spectrum is not None else c // 2
    kb = _pick(n1, (4, 2, 1))
    slab = pl.BlockSpec((2, kb, n2, c), lambda k: (0, k, 0, 0))
    out_slab = pl.BlockSpec((2, kb, n2, c_out), lambda k: (0, k, 0, 0))
    tw = pl.BlockSpec((kb, n2, 1), lambda k: (k, 0, 0))
    mat = pl.BlockSpec((2 * n2, 2 * n2), lambda k: (0, 0))
    in_specs = [slab, tw, tw, mat, mat]
    args = [a, consts["twr"], consts["twi"], *consts["m2"]]
    if spectrum is not None:
        in_specs += [mat, mat, slab]
        args += [*consts["m2inv"], spectrum]
    return pl.pallas_call(
        functools.partial(_fft_mid_kernel, n2=n2, with_filter=spectrum is not None),
        grid=(n1 // kb,),
        in_specs=in_specs,
        out_specs=out_slab,
        out_shape=jax.ShapeDtypeStruct((2, n1, n2, c_out), F32),
        compiler_params=_cparams("parallel"),
        name="fft_mid",
    )(*args)


def hyena_long_conv(x0, z, h, abs_sum, bias, length):
    c = MIX_W
    n = 2 * length
    n2 = 128 if length >= 1024 else 1
    n1 = n // n2
    consts = _dft_consts(n1, n2)
    nh = _dft_rows(n1)
    inv_scale = 1.0 / ((abs_sum[:, :c] + abs_sum[:, c:]) * n)
    filt_f = dft_left(consts["fwd_half"], h.reshape(n1 // 2, n2 * 2 * c))
    xin = z.reshape(n1 // 2, n2 * c)
    if n2 > 1:
        spec = fft_mid(consts, filt_f.reshape(2, nh, n2, 2 * c))
        a = dft_left(consts["fwd_half"], xin)
        b = fft_mid(consts, a.reshape(2, nh, n2, c), spec).reshape(2 * nh, n2 * c)
    else:
        b = dft_left(consts["fwd_half"], xin, filt_f)
    y = dft_inv_post(consts["inv_half"], b, x0.reshape(n1 // 2, n2 * c), xin,
                     jnp.tile(bias.reshape(1, c), (1, n2)), jnp.tile(inv_scale, (1, n2)))
    return y.reshape(length, c)


def _merge_kernel(*refs):
    ys, gs, ws = refs[0:N_BRANCH], refs[N_BRANCH:2 * N_BRANCH], refs[2 * N_BRANCH:3 * N_BRANCH]
    o_ref, wb_ref = refs[3 * N_BRANCH:]

    @pl.when(pl.program_id(1) == 0)
    def _():
        for i in range(N_BRANCH):
            wb_ref[i] = ws[i][...].astype(BF16)

    acc = None
    for i in range(N_BRANCH):
        t = jnp.dot(ys[i][...], wb_ref[i], preferred_element_type=F32) * _sigmoid(gs[i][...].astype(F32))
        acc = t if acc is None else acc + t
    o_ref[...] = acc.astype(o_ref.dtype)


def merge_branches(ys, gates, gate_col0, branch_w, layer):
    m = ys[0].shape[0]
    d = branch_w.shape[3]
    tm = _pick(m, (640, 512, 256, 128))
    tn = _pick(d, (1024, 512, 256, 128))
    nj = d // tn
    assert gate_col0 % tn == 0
    g0 = gate_col0 // tn
    y_specs = [pl.BlockSpec((tm, MIX_W), lambda j, i: (i, 0))] * N_BRANCH
    g_specs = [pl.BlockSpec((tm, tn), functools.partial(lambda j, i, b: (i, g0 + b * nj + j), b=b)) for b in range(N_BRANCH)]
    w_specs = [pl.BlockSpec((None, None, MIX_W, tn), functools.partial(lambda j, i, b: (layer, b, 0, j), b=b),
                            pipeline_mode=pl.Buffered(1)) for b in range(N_BRANCH)]
    return pl.pallas_call(
        _merge_kernel,
        grid=(nj, m // tm),
        in_specs=y_specs + g_specs + w_specs,
        out_specs=pl.BlockSpec((tm, tn), lambda j, i: (i, j)),
        out_shape=jax.ShapeDtypeStruct((m, d), BF16),
        scratch_shapes=[pltpu.VMEM((N_BRANCH, MIX_W, tn), BF16)],
        compiler_params=_cparams("parallel", "arbitrary"),
        name="merge_branches",
    )(*ys, *([gates] * N_BRANCH), *([branch_w] * N_BRANCH))


FFN_HALO = 16


def _sigmoid(x):
    return 0.5 + 0.5 * jnp.tanh(0.5 * x)


def _ffn_up_kernel(a_ref, wa32_ref, wg32_ref, cwa_ref, cwg_ref, cba_ref, cbg_ref, o_ref, wa_ref, wg_ref,
                   *, tm, chunk, fixes):
    hl = FFN_HALO

    @pl.when(pl.program_id(1) == 0)
    def _():
        wa_ref[...] = wa32_ref[...].astype(BF16)
        wg_ref[...] = wg32_ref[...].astype(BF16)

    def gated(a, rows, no_prev=None, no_next=None):
        def conv(w_ref, cw_ref, cb_ref):
            u = jnp.dot(a, w_ref[...], preferred_element_type=F32)
            up, dn = u[hl - 1:hl - 1 + rows], u[hl + 1:hl + 1 + rows]
            if no_prev is not None:
                up = jnp.where(no_prev, 0.0, up)
                dn = jnp.where(no_next, 0.0, dn)
            return up * cw_ref[0:1, :] + u[hl:hl + rows] * cw_ref[1:2, :] + dn * cw_ref[2:3, :] + cb_ref[...]

        g = conv(wg_ref, cwg_ref, cbg_ref)
        return (g * _sigmoid(g) * conv(wa_ref, cwa_ref, cba_ref)).astype(o_ref.dtype)

    for c in range(tm // chunk):
        o_ref[c * chunk:(c + 1) * chunk, :] = gated(a_ref[c * chunk:(c + 1) * chunk + 2 * hl, :], chunk)

    for blk, g0, first_row in fixes:
        @pl.when(pl.program_id(1) == blk)
        def _(g0=g0, first_row=first_row):
            row = blk * tm + g0 + lax.broadcasted_iota(jnp.int32, (hl, 1), 0)
            o_ref[g0:g0 + hl, :] = gated(a_ref[g0:g0 + 3 * hl, :], hl, row == first_row, row == first_row - 1)


FFN_TM = 1280


def ffn_up(a_ext, m, w_up, layer, conv_w, conv_b, n_latent):
    d = a_ext.shape[1]
    f = w_up.shape[2] // 2
    tm = FFN_TM
    tn = _pick(f, (512, 256, 128))
    chunk = _pick(tm, (640, 256, 128))
    nb, nj = m // tm, f // tn
    te = tm + 2 * FFN_HALO
    assert n_latent % FFN_HALO == 0 and 0 < n_latent < m and a_ext.shape[0] == nb * te
    fixes = []
    for r in (n_latent - 1, n_latent):
        g = r // FFN_HALO * FFN_HALO
        fixes.append((g // tm, g % tm, n_latent))
    wspec = lambda off: pl.BlockSpec((None, d, tn), lambda j, i: (layer, 0, off + j), pipeline_mode=pl.Buffered(1))
    return pl.pallas_call(
        functools.partial(_ffn_up_kernel, tm=tm, chunk=chunk, fixes=tuple(fixes)),
        grid=(nj, nb),
        in_specs=[pl.BlockSpec((te, d), lambda j, i: (i, 0)),
                  wspec(0), wspec(nj),
                  pl.BlockSpec((3, tn), lambda j, i: (0, j)),
                  pl.BlockSpec((3, tn), lambda j, i: (0, nj + j)),
                  pl.BlockSpec((1, tn), lambda j, i: (0, j)),
                  pl.BlockSpec((1, tn), lambda j, i: (0, nj + j))],
        out_specs=pl.BlockSpec((tm, tn), lambda j, i: (i, j)),
        out_shape=jax.ShapeDtypeStruct((m, f), BF16),
        scratch_shapes=[pltpu.VMEM((d, tn), BF16), pltpu.VMEM((d, tn), BF16)],
        compiler_params=_cparams("parallel", "arbitrary"),
        name="ffn_up",
    )(a_ext, w_up, w_up, conv_w, conv_w, conv_b.reshape(1, -1), conv_b.reshape(1, -1))


def kernel(x, c, ctx, c_ctx, ada_w, ada_b, norm1_g, norm2_g, w_in, a_qn_g, a_kn_g, b_rpb,
           c_gate_w2, c_gate_b, c_norm_g, d_conv_w, d_conv_b, d_ffn_w1, d_ffn_b1, d_ffn_w2,
           d_ffn_b2, d_ffn_w3, d_ffn_b3, d_ffn_w4, d_sin_freq, d_bias, branch_w, w_out,
           ffn_up_w, ffn_conv_w, ffn_conv_b, ffn_down, final_norm_g):
    depth = ada_w.shape[0]
    n_lat, d = x.shape[1], x.shape[2]
    n_ctx = ctx.shape[1]
    h = jnp.concatenate([x[0], ctx[0]], axis=0)
    cvec = jnp.concatenate([c[0:1], c_ctx[None, :], jnp.zeros((14, d), F32)], axis=0)
    cvec = (cvec * jax.nn.sigmoid(cvec)).astype(BF16)
    ctab, stab = rope_tables(n_lat, n_ctx)
    zeros_c = jnp.zeros((n_ctx, MIX_W), BF16)

    for i in range(depth):
        ctx_out = i < depth - 1
        mod = matmul(cvec, ada_w, i, 0, ada_w.shape[2], F32, bias=ada_b[i])[0:2]
        sh1, sc1, g1, sh2, sc2, g2 = jnp.split(mod, 6, axis=-1)
        n1 = mod_norm(h, norm1_g[i], sh1, sc1, n_lat, BF16)
        ca_end = IN_CA + 2 * C_GATE_RANK
        w_ca = jnp.pad(w_in[i, :, IN_CA:ca_end], ((0, 0), (0, 128 - 2 * C_GATE_RANK)))[None]
        p1 = matmul(n1, w_in, i, 0, PA_WIDTH, F32)
        pb = matmul_shifted(n1, w_in, i, ca_end, PB_WIDTH, F32)
        gates = matmul_shifted(n1, w_in, i, ca_end + PB_WIDTH, N_BRANCH * d, BF16)
        ca = matmul(n1, w_ca, 0, 0, 128, F32)

        q, k, v = a_prep(p1, ctab, stab, a_qn_g[i], a_kn_g[i])
        ya_l = flash_attention(q[:n_lat], k, v, A_KV_HEADS)
        yb_l = neighbourhood_attention(p1, na_bias_table(b_rpb[i]), n_lat, n_ctx)
        yc = gla_mixer(p1, pb, ca, c_gate_w2[i], c_gate_b[i], c_norm_g[i], n_lat, n_ctx)
        x0, z = hyena_pre(pb, d_conv_w[i], d_conv_b[i], n_lat)
        filt = (d_ffn_w1[i], d_ffn_b1[i], d_ffn_w2[i], d_ffn_b2[i], d_ffn_w3[i], d_ffn_b3[i], d_ffn_w4[i],
                d_sin_freq[i])
        hl, sl = hyena_filter(n_lat, *filt)
        yd_l = hyena_long_conv(x0[:n_lat], z[:n_lat], hl, sl, d_bias[i], n_lat)
        if ctx_out:
            ya_c = flash_attention(q[n_lat:], k[n_lat:], v[n_lat:], A_KV_HEADS)
            pc = p1[n_lat:]
            yb_c = flash_attention((pc[:, P_BQ:P_BK] * QK_SCALE).astype(BF16), pc[:, P_BK:P_BV].astype(BF16),
                                   with_ones(pc[:, P_BV:P_CQ].astype(BF16), B_HEADS), B_HEADS)
            hc, sc = hyena_filter(n_ctx, *filt)
            yd_c = hyena_long_conv(x0[n_lat:], z[n_lat:], hc, sc, d_bias[i], n_ctx)
        else:
            ya_c = yb_c = yd_c = zeros_c
        ys = [jnp.concatenate([ya_l, ya_c], axis=0), jnp.concatenate([yb_l, yb_c], axis=0), yc,
              jnp.concatenate([yd_l, yd_c], axis=0)]
        merged = merge_branches(ys, gates, 0, branch_w, i)
        h = matmul_residual(merged, w_out, i, h, g1, n_lat)
        n2_ext = mod_norm_halo(h, norm2_g[i], sh2, sc2, n_lat, FFN_TM, FFN_HALO)
        act = ffn_up(n2_ext, h.shape[0], ffn_up_w, i, ffn_conv_w[i], ffn_conv_b[i], n_lat)
        h = matmul_residual(act, ffn_down, i, h, g2, n_lat)

    zero2 = jnp.zeros((2, d), F32)
    out = mod_norm(h, final_norm_g, zero2, zero2, n_lat, F32, n_rows=n_lat)
    return out[None]
```

```python
import functools
import math

import numpy as np
import jax
import jax.numpy as jnp
from jax import lax
from jax.experimental import pallas as pl
from jax.experimental.pallas import tpu as pltpu

F32 = jnp.float32
BF16 = jnp.bfloat16

GRID_W = 64
HEAD_DIM = 128
A_HEADS = 4
A_KV_HEADS = 2
ROPE_THETA = 10000.0
B_HEADS = 4
NA_ROWS = 8
NA_COLS = 16
C_HEADS = 4
C_DK = 64
C_DV = 128
C_GATE_RANK = 16
C_GATE_TAU = 16.0
C_CHUNK = 64
MIX_W = 512
N_BRANCH = 4
HY_EMB = 33
HY_DECAY_MIN = math.log(1e-2) / 1.5
HY_DECAY_MAX = math.log(1e-2) / 0.3
NORM_EPS = 1e-6
NEG_BIG = -1e30

P_AQ, P_AK, P_AV = 0, 512, 768
P_BQ, P_BK, P_BV = 1024, 1536, 2048
P_CQ, P_CK, P_CV = 2560, 2816, 3072
PA_WIDTH = 3584
PB_CG, PB_DU = 0, 512
PB_WIDTH = 2048
IN_CA = PA_WIDTH

VMEM_LIMIT = 56 * 1024 * 1024


def _cparams(*sem):
    return pltpu.CompilerParams(dimension_semantics=sem, vmem_limit_bytes=VMEM_LIMIT)


def _pick(n, candidates):
    for c in candidates:
        if n % c == 0:
            return c
    return n


def _modnorm_kernel(h_ref, g_ref, sh_ref, sc_ref, o_ref, *, tb, n_latent):
    step = _pick(tb, (256, 128, 8))
    for c in range(tb // step):
        rows = slice(c * step, (c + 1) * step)
        x = h_ref[rows, :]
        row = pl.program_id(0) * tb + c * step + lax.broadcasted_iota(jnp.int32, (step, 1), 0)
        lat = row < n_latent
        y = x * lax.rsqrt(jnp.mean(x * x, axis=-1, keepdims=True) + NORM_EPS) * g_ref[...]
        y = y * (1.0 + jnp.where(lat, sc_ref[0], sc_ref[1])) + jnp.where(lat, sh_ref[0], sh_ref[1])
        o_ref[rows, :] = y.astype(o_ref.dtype)


def mod_norm(h, g, shift2, scale2, n_latent, out_dtype, n_rows=None):
    d = h.shape[1]
    m = n_rows or h.shape[0]
    tb = _pick(m, (1280, 1024, 512, 256, 128, 64, 8))
    return pl.pallas_call(
        functools.partial(_modnorm_kernel, tb=tb, n_latent=n_latent),
        grid=(m // tb,),
        in_specs=[pl.BlockSpec((tb, d), lambda i: (i, 0)),
                  pl.BlockSpec((1, d), lambda i: (0, 0)),
                  pl.BlockSpec((2, 1, d), lambda i: (0, 0, 0)),
                  pl.BlockSpec((2, 1, d), lambda i: (0, 0, 0))],
        out_specs=pl.BlockSpec((tb, d), lambda i: (i, 0)),
        out_shape=jax.ShapeDtypeStruct((m, d), out_dtype),
        compiler_params=_cparams("parallel"),
        name="mod_norm",
    )(h, g.reshape(1, d), shift2.reshape(2, 1, d), scale2.reshape(2, 1, d))


def _modnorm_halo_kernel(hp_ref, h_ref, hn_ref, g_ref, sh_ref, sc_ref, o_ref, *, tm, halo, n_latent, m):
    i = pl.program_id(0)

    def norm(x, row0):
        row = row0 + lax.broadcasted_iota(jnp.int32, (x.shape[0], 1), 0)
        y = x * lax.rsqrt(jnp.mean(x * x, axis=-1, keepdims=True) + NORM_EPS) * g_ref[...]
        lat = row < n_latent
        y = y * (1.0 + jnp.where(lat, sc_ref[0], sc_ref[1])) + jnp.where(lat, sh_ref[0], sh_ref[1])
        return jnp.where((row >= 0) & (row < m), y, 0.0).astype(o_ref.dtype)

    o_ref[0:halo, :] = norm(hp_ref[...], i * tm - halo)
    step = _pick(tm, (256, 128))
    for c in range(tm // step):
        o_ref[halo + c * step:halo + (c + 1) * step, :] = norm(h_ref[c * step:(c + 1) * step, :], i * tm + c * step)
    o_ref[halo + tm:, :] = norm(hn_ref[...], (i + 1) * tm)


def mod_norm_halo(h, g, shift2, scale2, n_latent, tm, halo):
    m, d = h.shape
    nb = m // tm
    hb = tm // halo
    return pl.pallas_call(
        functools.partial(_modnorm_halo_kernel, tm=tm, halo=halo, n_latent=n_latent, m=m),
        grid=(nb,),
        in_specs=[pl.BlockSpec((halo, d), lambda i: (jnp.maximum(i * hb - 1, 0), 0)),
                  pl.BlockSpec((tm, d), lambda i: (i, 0)),
                  pl.BlockSpec((halo, d), lambda i: (jnp.minimum((i + 1) * hb, m // halo - 1), 0)),
                  pl.BlockSpec((1, d), lambda i: (0, 0)),
                  pl.BlockSpec((2, 1, d), lambda i: (0, 0, 0)),
                  pl.BlockSpec((2, 1, d), lambda i: (0, 0, 0))],
        out_specs=pl.BlockSpec((tm + 2 * halo, d), lambda i: (i, 0)),
        out_shape=jax.ShapeDtypeStruct((nb * (tm + 2 * halo), d), BF16),
        compiler_params=_cparams("parallel"),
        name="mod_norm_halo",
    )(h, h, h, g.reshape(1, d), shift2.reshape(2, 1, d), scale2.reshape(2, 1, d))


VMEM_TILE_BUDGET = 46 * 1024 * 1024


def _dense_tiles(m, k, n, col0, extra_tiles, out_bytes):
    for tms in ((1280, 1024, 640), (512, 320, 256, 128, 16)):
        for tn in (1792, 1024, 512, 256, 128):
            if n % tn or col0 % tn:
                continue
            for tm in tms:
                if m % tm:
                    continue
                blocks = 2 * tm * k * 2 + k * tn * (4 + 2) + 2 * tm * tn * (out_bytes + 4 * extra_tiles)
                if blocks <= VMEM_TILE_BUDGET:
                    return tm, tn
    raise ValueError("no dense tiling fits")


def _cast_at_first_row_block(w_ref, wb_ref):
    @pl.when(pl.program_id(1) == 0)
    def _():
        wb_ref[...] = w_ref[...].astype(BF16)


def _mm_kernel(a_ref, w_ref, o_ref, wb_ref):
    _cast_at_first_row_block(w_ref, wb_ref)
    o_ref[...] = jnp.dot(a_ref[...], wb_ref[...], preferred_element_type=F32).astype(o_ref.dtype)


def _mm_bias_kernel(a_ref, w_ref, b_ref, o_ref, wb_ref):
    _cast_at_first_row_block(w_ref, wb_ref)
    o_ref[...] = (jnp.dot(a_ref[...], wb_ref[...], preferred_element_type=F32) + b_ref[...]).astype(o_ref.dtype)


def matmul(a, w3, layer, col0, n, out_dtype, bias=None):
    m, k = a.shape
    tm, tn = _dense_tiles(m, k, n, col0, extra_tiles=0, out_bytes=jnp.dtype(out_dtype).itemsize)
    assert col0 % tn == 0 and w3.shape[1] == k
    c0 = col0 // tn
    in_specs = [pl.BlockSpec((tm, k), lambda j, i: (i, 0)),
                pl.BlockSpec((None, k, tn), lambda j, i: (layer, 0, c0 + j), pipeline_mode=pl.Buffered(1))]
    args = [a, w3]
    kern = _mm_kernel
    if bias is not None:
        in_specs.append(pl.BlockSpec((1, tn), lambda j, i: (0, j)))
        args.append(bias.reshape(1, n))
        kern = _mm_bias_kernel
    return pl.pallas_call(
        kern,
        grid=(n // tn, m // tm),
        in_specs=in_specs,
        out_specs=pl.BlockSpec((tm, tn), lambda j, i: (i, j)),
        out_shape=jax.ShapeDtypeStruct((m, n), out_dtype),
        scratch_shapes=[pltpu.VMEM((k, tn), BF16)],
        compiler_params=_cparams("parallel", "arbitrary"),
        name="matmul",
    )(*args)


WT_PIECE, WT_TAIL = 256, 128


def _mm_wt_kernel(*refs, n_pieces, shift, tn):
    a_ref, w_refs = refs[0], refs[1:1 + n_pieces]
    o_ref, wb_ref = refs[1 + n_pieces:]

    @pl.when(pl.program_id(1) == 0)
    def _():
        x = jnp.concatenate([r[...] for r in w_refs], axis=0)
        wb_ref[...] = x[shift:shift + tn].astype(BF16)

    o_ref[...] = lax.dot_general(a_ref[...], wb_ref[...], (((1,), (1,)), ((), ())),
                                 preferred_element_type=F32).astype(o_ref.dtype)


def matmul_wt(a, wt3, layer, col0, n, out_dtype):
    m, k = a.shape
    shift = col0 % WT_TAIL
    base = col0 - shift
    assert base % WT_PIECE == 0 and shift % 8 == 0 and col0 + n <= wt3.shape[1] and wt3.shape[2] == k
    tm, tn = _dense_tiles(m, k, n, 0, extra_tiles=0, out_bytes=jnp.dtype(out_dtype).itemsize)
    assert tn % WT_PIECE == 0
    per = tn // WT_PIECE
    single = dict(pipeline_mode=pl.Buffered(1))
    w_specs = [pl.BlockSpec((None, WT_PIECE, k), functools.partial(
        lambda j, i, p: (layer, base // WT_PIECE + j * per + p, 0), p=p), **single) for p in range(per)]
    if shift:
        w_specs.append(pl.BlockSpec((None, WT_TAIL, k),
                                    lambda j, i: (layer, base // WT_TAIL + (j + 1) * (tn // WT_TAIL), 0), **single))
    return pl.pallas_call(
        functools.partial(_mm_wt_kernel, n_pieces=len(w_specs), shift=shift, tn=tn),
        grid=(n // tn, m // tm),
        in_specs=[pl.BlockSpec((tm, k), lambda j, i: (i, 0))] + w_specs,
        out_specs=pl.BlockSpec((tm, tn), lambda j, i: (i, j)),
        out_shape=jax.ShapeDtypeStruct((m, n), out_dtype),
        scratch_shapes=[pltpu.VMEM((tn, k), BF16)],
        compiler_params=_cparams("parallel", "arbitrary"),
        name="matmul_wt",
    )(a, *([wt3] * len(w_specs)))


def _mm_res_kernel(a_ref, w_ref, h_ref, g_ref, o_ref, wb_ref, *, n_latent, tm):
    _cast_at_first_row_block(w_ref, wb_ref)
    row = pl.program_id(1) * tm + lax.broadcasted_iota(jnp.int32, (tm, 1), 0)
    gate = jnp.where(row < n_latent, g_ref[0], g_ref[1])
    o_ref[...] = h_ref[...] + gate * jnp.dot(a_ref[...], wb_ref[...], preferred_element_type=F32)


def matmul_residual(a, w3, layer, h, gate2, n_latent):
    m, k = a.shape
    n = w3.shape[2]
    tm, tn = _dense_tiles(m, k, n, 0, extra_tiles=1, out_bytes=4)
    return pl.pallas_call(
        functools.partial(_mm_res_kernel, n_latent=n_latent, tm=tm),
        grid=(n // tn, m // tm),
        in_specs=[pl.BlockSpec((tm, k), lambda j, i: (i, 0)),
                  pl.BlockSpec((None, k, tn), lambda j, i: (layer, 0, j), pipeline_mode=pl.Buffered(1)),
                  pl.BlockSpec((tm, tn), lambda j, i: (i, j)),
                  pl.BlockSpec((2, 1, tn), lambda j, i: (0, 0, j))],
        out_specs=pl.BlockSpec((tm, tn), lambda j, i: (i, j)),
        out_shape=jax.ShapeDtypeStruct((m, n), F32),
        scratch_shapes=[pltpu.VMEM((k, tn), BF16)],
        compiler_params=_cparams("parallel", "arbitrary"),
        name="matmul_residual",
    )(a, w3, h, gate2.reshape(2, 1, n))


def _swap_halves(y):
    lane = lax.broadcasted_iota(jnp.int32, y.shape, 1)
    return jnp.where((lane % 64) < 32, pltpu.roll(y, 96, 1), pltpu.roll(y, 32, 1))


def _aprep_kernel(p_ref, cos_ref, sin_ref, qg_ref, kg_ref, q_ref, k_ref, v_ref):
    c = cos_ref[...]
    s = sin_ref[...]

    def norm_rope(x, g):
        y = x * lax.rsqrt(jnp.mean(x * x, axis=-1, keepdims=True) + NORM_EPS) * g
        return y * c + _swap_halves(y) * s

    for hd in range(A_HEADS):
        x = p_ref[:, hd * HEAD_DIM:(hd + 1) * HEAD_DIM]
        q_ref[:, hd * HEAD_DIM:(hd + 1) * HEAD_DIM] = (
            norm_rope(x, qg_ref[...]) * QK_SCALE).astype(q_ref.dtype)
    for hd in range(A_KV_HEADS):
        x = p_ref[:, P_AK + hd * HEAD_DIM:P_AK + (hd + 1) * HEAD_DIM]
        k_ref[:, hd * HEAD_DIM:(hd + 1) * HEAD_DIM] = norm_rope(x, kg_ref[...]).astype(k_ref.dtype)
        v = p_ref[:, P_AV + hd * HEAD_DIM:P_AV + (hd + 1) * HEAD_DIM]
        v_ref[:, 2 * hd * HEAD_DIM:(2 * hd + 1) * HEAD_DIM] = v.astype(v_ref.dtype)
        v_ref[:, (2 * hd + 1) * HEAD_DIM:(2 * hd + 2) * HEAD_DIM] = jnp.ones_like(v).astype(v_ref.dtype)


def rope_tables(n_latent, n_ctx):
    pos = np.arange(n_latent)
    axes = np.stack([pos // GRID_W, pos % GRID_W], axis=-1).astype(np.float32)
    quarter = HEAD_DIM // 4
    inv_freq = (np.float32(ROPE_THETA) ** (-np.arange(quarter, dtype=np.float32) / quarter)).astype(np.float32)
    ang = (axes[:, :, None] * inv_freq).astype(np.float32)
    cos, sin = np.cos(ang), np.sin(ang)
    ctab = np.concatenate([cos[:, 0], cos[:, 0], cos[:, 1], cos[:, 1]], axis=-1)
    stab = np.concatenate([-sin[:, 0], sin[:, 0], -sin[:, 1], sin[:, 1]], axis=-1)
    ctab = np.concatenate([ctab, np.ones((n_ctx, HEAD_DIM), np.float32)], axis=0)
    stab = np.concatenate([stab, np.zeros((n_ctx, HEAD_DIM), np.float32)], axis=0)
    return jnp.asarray(ctab, F32), jnp.asarray(stab, F32)


def a_prep(p1, ctab, stab, qn_g, kn_g):
    m = p1.shape[0]
    tb = _pick(m, (256, 128, 64, 8))
    wa = 1024
    return pl.pallas_call(
        _aprep_kernel,
        grid=(m // tb,),
        in_specs=[pl.BlockSpec((tb, wa), lambda i: (i, 0)),
                  pl.BlockSpec((tb, HEAD_DIM), lambda i: (i, 0)),
                  pl.BlockSpec((tb, HEAD_DIM), lambda i: (i, 0)),
                  pl.BlockSpec((1, HEAD_DIM), lambda i: (0, 0)),
                  pl.BlockSpec((1, HEAD_DIM), lambda i: (0, 0))],
        out_specs=[pl.BlockSpec((tb, 512), lambda i: (i, 0)),
                   pl.BlockSpec((tb, 256), lambda i: (i, 0)),
                   pl.BlockSpec((tb, 512), lambda i: (i, 0))],
        out_shape=[jax.ShapeDtypeStruct((m, 512), BF16),
                   jax.ShapeDtypeStruct((m, 256), BF16),
                   jax.ShapeDtypeStruct((m, 512), BF16)],
        compiler_params=_cparams("parallel"),
        name="a_prep",
    )(p1, ctab, stab, qn_g.reshape(1, HEAD_DIM), kn_g.reshape(1, HEAD_DIM))


LOG2E = 1.4426950408889634
QK_SCALE = HEAD_DIM ** -0.5 * LOG2E


def _flash_kernel(q_ref, k_ref, v_ref, o_ref, m_ref, acc_ref, sa_ref, sb_ref, *, group, tk, nk):
    m_ref[...] = jnp.full_like(m_ref, NEG_BIG)
    acc_ref[...] = jnp.zeros_like(acc_ref)

    def scores(j, s_ref):
        k = k_ref[pl.ds(pl.multiple_of(j * tk, tk), tk), :]
        for g in range(group):
            q = q_ref[:, g * HEAD_DIM:(g + 1) * HEAD_DIM]
            s_ref[g] = lax.dot_general(q, k, (((1,), (1,)), ((), ())), preferred_element_type=F32)

    def update(j, s_ref):
        v = v_ref[pl.ds(pl.multiple_of(j * tk, tk), tk), :]
        for g in range(group):
            s = s_ref[g]
            m_prev = m_ref[g]
            m_new = jnp.maximum(m_prev, jnp.max(s, axis=-1, keepdims=True))
            p = jnp.exp2(s - m_new).astype(BF16)
            acc_ref[g] = jnp.exp2(m_prev - m_new) * acc_ref[g] + jnp.dot(p, v, preferred_element_type=F32)
            m_ref[g] = m_new

    scores(0, sa_ref)
    pairs = (nk - 1) // 2

    def body(jj, carry):
        j = 2 * jj
        scores(j + 1, sb_ref)
        update(j, sa_ref)
        scores(j + 2, sa_ref)
        update(j + 1, sb_ref)
        return carry

    lax.fori_loop(0, pairs, body, 0)
    if (nk - 1) % 2 == 1:
        scores(nk - 1, sb_ref)
        update(nk - 2, sa_ref)
        update(nk - 1, sb_ref)
    else:
        update(nk - 1, sa_ref)
    for g in range(group):
        acc = acc_ref[g]
        o_ref[:, g * HEAD_DIM:(g + 1) * HEAD_DIM] = (acc[:, :HEAD_DIM] / acc[:, HEAD_DIM:]).astype(o_ref.dtype)


def with_ones(v, heads):
    n = v.shape[0]
    v3 = v.reshape(n, heads, HEAD_DIM)
    return jnp.concatenate([v3, jnp.ones_like(v3)], axis=-1).reshape(n, heads * 2 * HEAD_DIM)


def flash_attention(q, k, v1, kv_heads, tq=None, tk=None):
    lq, hq = q.shape
    lk = k.shape[0]
    group = hq // (kv_heads * HEAD_DIM)
    tq = tq or _pick(lq, (1024, 512, 256, 128))
    tk = tk or _pick(lk, (1280, 1024, 512, 256, 128))
    nk = lk // tk
    gw = group * HEAD_DIM
    return pl.pallas_call(
        functools.partial(_flash_kernel, group=group, tk=tk, nk=nk),
        grid=(kv_heads, lq // tq),
        in_specs=[pl.BlockSpec((tq, gw), lambda h, i: (i, h)),
                  pl.BlockSpec((lk, HEAD_DIM), lambda h, i: (0, h), pipeline_mode=pl.Buffered(1)),
                  pl.BlockSpec((lk, 2 * HEAD_DIM), lambda h, i: (0, h), pipeline_mode=pl.Buffered(1))],
        out_specs=pl.BlockSpec((tq, gw), lambda h, i: (i, h)),
        out_shape=jax.ShapeDtypeStruct((lq, hq), BF16),
        scratch_shapes=[pltpu.VMEM((group, tq, 1), F32),
                        pltpu.VMEM((group, tq, 2 * HEAD_DIM), F32),
                        pltpu.VMEM((group, tq, tk), F32),
                        pltpu.VMEM((group, tq, tk), F32)],
        compiler_params=_cparams("parallel", "parallel"),
        name="flash_attention",
    )(q, k, v1)


NA_GROUP = 4


def na_bias_table(rpb):
    col = np.arange(GRID_W)
    c_start = np.clip(col - NA_COLS // 2, 0, GRID_W - NA_COLS)
    kc = np.arange(GRID_W)
    col_ok = (kc[None, :] >= c_start[:, None]) & (kc[None, :] < c_start[:, None] + NA_COLS)
    dc = np.clip(kc[None, :] - col[:, None] + (NA_COLS - 1), 0, 2 * NA_COLS - 2)
    i = np.arange(NA_GROUP)
    j = np.arange(3 * NA_GROUP)
    dr = np.clip(j[None, :] - i[:, None] + NA_ROWS // 2 - 1, 0, 2 * NA_ROWS - 2)
    first = np.broadcast_to((j >= NA_GROUP)[None, :], dr.shape)
    inner = (j[None, :] >= i[:, None]) & (j[None, :] < i[:, None] + NA_ROWS)
    last = np.broadcast_to((j < NA_ROWS)[None, :], dr.shape)
    pick_r = jnp.asarray(np.eye(2 * NA_ROWS - 1, dtype=np.float32)[dr])
    pick_c = jnp.asarray(np.eye(2 * NA_COLS - 1, dtype=np.float32)[dc])
    bias = jnp.einsum("ijr,hrs,cks->hijck", pick_r, rpb.astype(F32), pick_c, precision=lax.Precision.HIGHEST)
    tabs = []
    for row_ok in (first, inner, last):
        ok = row_ok[:, :, None, None] & col_ok[None, None, :, :]
        t = jnp.where(jnp.asarray(ok)[None], bias * LOG2E, NEG_BIG)
        tabs.append(t.transpose(0, 1, 3, 2, 4).reshape(rpb.shape[0], NA_GROUP * GRID_W, 3 * NA_GROUP * GRID_W))
    return jnp.stack(tabs)


def _na_kernel(q_ref, k0_ref, k1_ref, k2_ref, v0_ref, v1_ref, v2_ref, kc_ref, vc_ref, b_ref, o_ref):
    k_u = jnp.concatenate([r[...].astype(BF16) for r in (k0_ref, k1_ref, k2_ref)], axis=0)
    v_u = jnp.concatenate([r[...].astype(BF16) for r in (v0_ref, v1_ref, v2_ref)], axis=0)
    k_c = kc_ref[...].astype(BF16)
    v_c = vc_ref[...].astype(BF16)
    nt = (((1,), (1,)), ((), ()))
    for hd in range(B_HEADS):
        sl = slice(hd * HEAD_DIM, (hd + 1) * HEAD_DIM)
        q = (q_ref[:, sl] * QK_SCALE).astype(BF16)
        s_loc = lax.dot_general(q, k_u[:, sl], nt, preferred_element_type=F32) + b_ref[hd]
        s_ctx = lax.dot_general(q, k_c[:, sl], nt, preferred_element_type=F32)
        mx = jnp.maximum(jnp.max(s_loc, axis=-1, keepdims=True), jnp.max(s_ctx, axis=-1, keepdims=True))
        p_loc = jnp.exp2(s_loc - mx)
        p_ctx = jnp.exp2(s_ctx - mx)
        den = jnp.sum(p_loc, axis=-1, keepdims=True) + jnp.sum(p_ctx, axis=-1, keepdims=True)
        o = (jnp.dot(p_loc.astype(BF16), v_u[:, sl], preferred_element_type=F32)
             + jnp.dot(p_ctx.astype(BF16), v_c[:, sl], preferred_element_type=F32))
        o_ref[:, sl] = (o / den).astype(o_ref.dtype)


def neighbourhood_attention(p1, bias_tab, n_latent, n_ctx):
    rows = n_latent // GRID_W
    ng = rows // NA_GROUP
    assert rows % NA_GROUP == 0 and rows >= 2 * NA_ROWS and NA_ROWS == 2 * NA_GROUP
    w = B_HEADS * HEAD_DIM
    t = NA_GROUP * GRID_W
    cq, ck, cv = P_BQ // w, P_BK // w, P_BV // w

    def kv_specs(col):
        return [pl.BlockSpec((t, w), functools.partial(lambda g, d, col: (jnp.clip(g + d, 0, ng - 1), col), d=d, col=col))
                for d in (-1, 0, 1)]

    cblk = n_latent // n_ctx
    return pl.pallas_call(
        _na_kernel,
        grid=(ng,),
        in_specs=[pl.BlockSpec((t, w), lambda g: (g, cq))] + kv_specs(ck) + kv_specs(cv) + [
            pl.BlockSpec((n_ctx, w), lambda g: (cblk, ck)),
            pl.BlockSpec((n_ctx, w), lambda g: (cblk, cv)),
            pl.BlockSpec((None,) + bias_tab.shape[1:],
                         lambda g: (jnp.where(g == 0, 0, jnp.where(g == ng - 1, 2, 1)), 0, 0, 0))],
        out_specs=pl.BlockSpec((t, w), lambda g: (g, 0)),
        out_shape=jax.ShapeDtypeStruct((n_latent, w), BF16),
        compiler_params=_cparams("parallel"),
        name="neighbourhood_attention",
    )(*([p1] * 9), bias_tab)


GLA_BLOCK = 4 * C_CHUNK


def _gla_block(q_ref, k_ref, v_ref, a_ref, w2_ref, gb_ref, st_ref, reverse, emit):
    c = C_CHUNK
    nch = GLA_BLOCK // c
    x = jnp.dot(a_ref[...].astype(BF16), w2_ref[...], preferred_element_type=F32) + gb_ref[...]
    la = (jnp.minimum(x, 0.0) - jnp.log(1.0 + jnp.exp(-jnp.abs(x)))) * (1.0 / C_GATE_TAU)
    pos = lax.broadcasted_iota(jnp.int32, la.shape, 0) % c
    b = la
    sh = 1
    while sh < c:
        if reverse:
            b = b + jnp.where(pos < c - sh, pltpu.roll(b, GLA_BLOCK - sh, 0), 0.0)
        else:
            b = b + jnp.where(pos >= sh, pltpu.roll(b, sh, 0), 0.0)
        sh *= 2
    qs = q_ref[...] * (C_DK ** -0.5)
    kk = k_ref[...]
    q_in = (qs * jnp.exp(b)).astype(BF16)
    ii = lax.broadcasted_iota(jnp.int32, (c, c), 0)
    jj = lax.broadcasted_iota(jnp.int32, (c, c), 1)
    mask = (jj >= ii) if reverse else (jj <= ii)
    nt = (((1,), (1,)), ((), ()))
    tn = (((0,), (0,)), ((), ()))
    states = [st_ref[hd] for hd in range(C_HEADS)]
    for ch in (range(nch - 1, -1, -1) if reverse else range(nch)):
        rows = slice(ch * c, (ch + 1) * c)
        bc = b[rows]
        b_mid = bc[c // 2:c // 2 + 1, :]
        b_last = bc[0:1, :] if reverse else bc[c - 1:c, :]
        q_t = (qs[rows] * jnp.exp(bc - b_mid)).astype(BF16)
        k_t = (kk[rows] * jnp.exp(b_mid - bc)).astype(BF16)
        k_d = (kk[rows] * jnp.exp(b_last - bc)).astype(BF16)
        e_last = jnp.exp(b_last)
        for hd in range(C_HEADS):
            ks = slice(hd * C_DK, (hd + 1) * C_DK)
            v = v_ref[rows, hd * C_DV:(hd + 1) * C_DV].astype(BF16)
            att = lax.dot_general(q_t[:, ks], k_t[:, ks], nt, preferred_element_type=F32)
            att = jnp.where(mask, att, 0.0).astype(BF16)
            o = (lax.dot_general(q_in[rows, ks], states[hd].astype(BF16), nt, preferred_element_type=F32)
                 + jnp.dot(att, v, preferred_element_type=F32))
            states[hd] = e_last[:, ks] * states[hd] + lax.dot_general(v, k_d[:, ks], tn, preferred_element_type=F32)
            emit(rows, hd, o)
    for hd in range(C_HEADS):
        st_ref[hd] = states[hd]


def _gla_fwd_kernel(q_ref, k_ref, v_ref, a_ref, w2_ref, gb_ref, o_ref, st_ref):
    @pl.when(pl.program_id(0) == 0)
    def _():
        st_ref[...] = jnp.zeros_like(st_ref)

    def emit(rows, hd, o):
        o_ref[rows, hd * C_DV:(hd + 1) * C_DV] = o

    _gla_block(q_ref, k_ref, v_ref, a_ref, w2_ref, gb_ref, st_ref, False, emit)


def _gla_bwd_kernel(q_ref, k_ref, v_ref, a_ref, w2_ref, gb_ref, of_ref, g_ref, ng_ref, y_ref, st_ref):
    @pl.when(pl.program_id(0) == 0)
    def _():
        st_ref[...] = jnp.zeros_like(st_ref)

    def emit(rows, hd, o):
        vs = slice(hd * C_DV, (hd + 1) * C_DV)
        x = of_ref[rows, vs] + o
        g = g_ref[rows, vs]
        y = x * lax.rsqrt(jnp.mean(x * x, axis=-1, keepdims=True) + NORM_EPS) * ng_ref[...]
        y_ref[rows, vs] = (y * (g * _sigmoid(g))).astype(y_ref.dtype)

    _gla_block(q_ref, k_ref, v_ref, a_ref, w2_ref, gb_ref, st_ref, True, emit)


def gla_mixer(p1, pb, ca, gate_w2, gate_b, norm_g, n_latent, n_ctx):
    m = p1.shape[0]
    c = GLA_BLOCK
    assert n_latent % c == 0 and n_ctx % c == 0
    nl, nc = n_latent // c, n_ctx // c
    kw = C_HEADS * C_DK
    vw = C_HEADS * C_DV

    def blk_f(s):
        return jnp.where(s < nc, nl + s, s - nc)

    def blk_r(s):
        return nl + nc - 1 - s

    w2 = jnp.zeros((2, 128, kw), F32)
    w2 = w2.at[0, 0:C_GATE_RANK].set(gate_w2[0]).at[1, C_GATE_RANK:2 * C_GATE_RANK].set(gate_w2[1]).astype(BF16)
    gb = gate_b.reshape(2, 1, kw).astype(F32)

    def common_specs(blk):
        return [pl.BlockSpec((c, kw), lambda s: (blk(s), P_CQ // kw)),
                pl.BlockSpec((c, kw), lambda s: (blk(s), P_CK // kw)),
                pl.BlockSpec((c, vw), lambda s: (blk(s), P_CV // vw)),
                pl.BlockSpec((c, 128), lambda s: (blk(s), 0))]

    st_scratch = [pltpu.VMEM((C_HEADS, C_DV, C_DK), F32)]
    o_f = pl.pallas_call(
        _gla_fwd_kernel,
        grid=(nl + nc,),
        in_specs=common_specs(blk_f) + [pl.BlockSpec((None, 128, kw), lambda s: (0, 0, 0)),
                                        pl.BlockSpec((None, 1, kw), lambda s: (0, 0, 0))],
        out_specs=pl.BlockSpec((c, vw), lambda s: (blk_f(s), 0)),
        out_shape=jax.ShapeDtypeStruct((m, vw), F32),
        scratch_shapes=st_scratch,
        compiler_params=_cparams("arbitrary"),
        name="gla_forward",
    )(p1, p1, p1, ca, w2, gb)
    y = pl.pallas_call(
        _gla_bwd_kernel,
        grid=(nl + nc,),
        in_specs=common_specs(blk_r) + [pl.BlockSpec((None, 128, kw), lambda s: (1, 0, 0)),
                                        pl.BlockSpec((None, 1, kw), lambda s: (1, 0, 0)),
                                        pl.BlockSpec((c, vw), lambda s: (blk_r(s), 0)),
                                        pl.BlockSpec((c, vw), lambda s: (blk_r(s), PB_CG // vw)),
                                        pl.BlockSpec((1, C_DV), lambda s: (0, 0))],
        out_specs=pl.BlockSpec((c, vw), lambda s: (blk_r(s), 0)),
        out_shape=jax.ShapeDtypeStruct((m, vw), BF16),
        scratch_shapes=st_scratch,
        compiler_params=_cparams("arbitrary"),
        name="gla_reverse",
    )(p1, p1, p1, ca, w2, gb, o_f, pb, norm_g.reshape(1, C_DV))
    return y


def _conv3_rows(x, prev_row, next_row, w_ref, b_ref):
    t = x.shape[0]
    row = lax.broadcasted_iota(jnp.int32, x.shape, 0)
    up = jnp.where(row == 0, prev_row, pltpu.roll(x, 1, 0))
    dn = jnp.where(row == t - 1, next_row, pltpu.roll(x, t - 1, 0))
    return up * w_ref[0:1, :] + x * w_ref[1:2, :] + dn * w_ref[2:3, :] + b_ref[...]


def _hy_pre_kernel(*refs, tb, seq_starts, seq_ends):
    mains, prevs, nexts = refs[0:3], refs[3:6], refs[6:9]
    w_refs, b_refs = refs[9:12], refs[12:15]
    x0_ref, z_ref = refs[15:17]
    r0 = pl.program_id(0) * tb
    pvalid = jnp.where(functools.reduce(jnp.logical_or, [r0 == s for s in seq_starts]), 0.0, 1.0)
    nvalid = jnp.where(functools.reduce(jnp.logical_or, [r0 + tb == e for e in seq_ends]), 0.0, 1.0)
    res = []
    for t in range(3):
        res.append(_conv3_rows(mains[t][...], prevs[t][7:8, :] * pvalid, nexts[t][0:1, :] * nvalid,
                               w_refs[t], b_refs[t]))
    x0_ref[...] = res[0]
    z_ref[...] = res[2] * res[1]


def hyena_pre(p1, conv_w, conv_b, n_latent):
    m = p1.shape[0]
    w = MIX_W
    tb = _pick(math.gcd(n_latent, m), (256, 128, 64, 8))
    nb8 = m // 8
    c0 = PB_DU // w
    mains = [pl.BlockSpec((tb, w), functools.partial(lambda i, t: (i, c0 + t), t=t)) for t in range(3)]
    prevs = [pl.BlockSpec((8, w), functools.partial(lambda i, t: (jnp.maximum(i * (tb // 8) - 1, 0), c0 + t), t=t))
             for t in range(3)]
    nexts = [pl.BlockSpec((8, w), functools.partial(lambda i, t: (jnp.minimum((i + 1) * (tb // 8), nb8 - 1), c0 + t), t=t))
             for t in range(3)]
    wspecs = [pl.BlockSpec((3, w), functools.partial(lambda i, t: (0, t), t=t)) for t in range(3)]
    bspecs = [pl.BlockSpec((1, w), functools.partial(lambda i, t: (0, t), t=t)) for t in range(3)]
    return pl.pallas_call(
        functools.partial(_hy_pre_kernel, tb=tb, seq_starts=(0, n_latent), seq_ends=(n_latent, m)),
        grid=(m // tb,),
        in_specs=mains + prevs + nexts + wspecs + bspecs,
        out_specs=[pl.BlockSpec((tb, w), lambda i: (i, 0))] * 2,
        out_shape=[jax.ShapeDtypeStruct((m, w), F32)] * 2,
        compiler_params=_cparams("parallel"),
        name="hyena_pre",
    )(*([p1] * 9), *([conv_w] * 3), *([conv_b.reshape(1, -1)] * 3))


def _hy_filter_kernel(z_ref, w1_ref, b1_ref, w2_ref, b2_ref, w3_ref, b3_ref, w4_ref, fr_ref, dl_ref,
                      h_ref, s_ref):
    def dot3(x, w):
        xh, wh = x.astype(BF16), w.astype(BF16)
        xl, wl = (x - xh.astype(F32)).astype(BF16), (w - wh.astype(F32)).astype(BF16)
        return (jnp.dot(xh, wh, preferred_element_type=F32) + jnp.dot(xh, wl, preferred_element_type=F32)
                + jnp.dot(xl, wh, preferred_element_type=F32))

    z = z_ref[...]
    fr = fr_ref[...]
    hid = jnp.sin(fr * (dot3(z, w1_ref[...]) + b1_ref[...]))
    hid = jnp.sin(fr * (dot3(hid, w2_ref[...]) + b2_ref[...]))
    hid = jnp.sin(fr * (dot3(hid, w3_ref[...]) + b3_ref[...]))
    h = dot3(hid, w4_ref[...])
    h = h * jnp.exp(-z[:, 0:1] * dl_ref[...])
    row = pl.program_id(0) * h.shape[0] + lax.broadcasted_iota(jnp.int32, h.shape, 0)
    col = lax.broadcasted_iota(jnp.int32, h.shape, 1)
    h_ref[...] = jnp.where((row == 0) & (col >= MIX_W), 0.0, h)

    @pl.when(pl.program_id(0) == 0)
    def _():
        s_ref[...] = jnp.zeros_like(s_ref)

    s_ref[...] += jnp.sum(jnp.abs(h), axis=0, keepdims=True)


def hyena_filter(length, w1, b1, w2, b2, w3, b3, w4, freq):
    bands = (HY_EMB - 1) // 2
    t = np.linspace(0.0, 1.0, length, dtype=np.float32)[:, None]
    wv = (np.float32(2.0 * math.pi) * np.arange(length, dtype=np.float32)[:, None] / np.float32(length)).astype(np.float32)
    f = np.linspace(1e-4, bands - 1, bands, dtype=np.float32)
    zt = np.concatenate([t, np.cos(f * wv), -np.sin(f * wv)], axis=-1).astype(np.float32)
    z = np.zeros((length, 128), np.float32)
    z[:, :HY_EMB] = zt
    deltas = np.abs(np.linspace(HY_DECAY_MIN, HY_DECAY_MAX, MIX_W, dtype=np.float32))
    dl = np.concatenate([deltas, deltas])[None, :].astype(np.float32)

    def pad2(a, r, c):
        return jnp.zeros((r, c), F32).at[:a.shape[0], :a.shape[1]].set(a.astype(F32))

    hf = w1.shape[1]
    tb = _pick(length, (512, 256, 128, 8))
    full = lambda i: (0, 0)
    return pl.pallas_call(
        _hy_filter_kernel,
        grid=(length // tb,),
        in_specs=[pl.BlockSpec((tb, 128), lambda i: (i, 0)),
                  pl.BlockSpec((128, 128), full), pl.BlockSpec((1, 128), full),
                  pl.BlockSpec((128, 128), full), pl.BlockSpec((1, 128), full),
                  pl.BlockSpec((128, 128), full), pl.BlockSpec((1, 128), full),
                  pl.BlockSpec((128, 2 * MIX_W), full), pl.BlockSpec((1, 128), full),
                  pl.BlockSpec((1, 2 * MIX_W), full)],
        out_specs=[pl.BlockSpec((tb, 2 * MIX_W), lambda i: (i, 0)),
                   pl.BlockSpec((1, 2 * MIX_W), full)],
        out_shape=[jax.ShapeDtypeStruct((length, 2 * MIX_W), F32),
                   jax.ShapeDtypeStruct((1, 2 * MIX_W), F32)],
        compiler_params=_cparams("arbitrary"),
        name="hyena_filter",
    )(jnp.asarray(z), pad2(w1, 128, 128), pad2(b1[None], 1, 128), pad2(w2, 128, 128), pad2(b2[None], 1, 128),
      pad2(w3, 128, 128), pad2(b3[None], 1, 128), pad2(w4, 128, 2 * MIX_W), pad2(freq[None], 1, 128),
      jnp.asarray(dl))


def _split_hi_lo(m):
    m = np.asarray(m, np.float32)
    hi = jnp.asarray(m, F32).astype(BF16)
    lo = (jnp.asarray(m, F32) - hi.astype(F32)).astype(BF16)
    return hi, lo


def _dft_rows(n1):
    return -(-(n1 // 2 + 1) // 8) * 8


def _dft_consts(n1, n2):
    n = n1 * n2
    hf = n1 // 2
    nh = _dft_rows(n1)
    k = np.arange(nh, dtype=np.int64)
    kept = (k <= hf)[:, None]
    ang1 = 2.0 * np.pi * ((k[:, None] * np.arange(hf, dtype=np.int64)[None, :]) % n1) / n1
    c1, s1 = np.where(kept, np.cos(ang1), 0.0), np.where(kept, np.sin(ang1), 0.0)
    fwd_half = np.concatenate([c1, -s1], axis=0)
    wgt = np.where((k == 0) | (k == hf), 1.0, 2.0)[:, None]
    inv_half = np.concatenate([(wgt * c1).T, -(wgt * s1).T], axis=1)
    k2 = np.arange(n2, dtype=np.int64)
    ang2 = 2.0 * np.pi * ((k2[:, None] * k2[None, :]) % n2) / n2
    c2, s2 = np.cos(ang2), np.sin(ang2)
    m2 = np.block([[c2, s2], [-s2, c2]])
    angt = 2.0 * np.pi * ((k[:, None] * k2[None, :]) % n) / n
    twr = np.cos(angt).astype(np.float32)[:, :, None]
    twi = (-np.sin(angt)).astype(np.float32)[:, :, None]
    return dict(fwd_half=_split_hi_lo(fwd_half), inv_half=_split_hi_lo(inv_half),
                m2=_split_hi_lo(m2), m2inv=_split_hi_lo(m2.T), twr=jnp.asarray(twr), twi=jnp.asarray(twi))


def _dot3(mh, ml, x):
    xh = x.astype(BF16)
    xl = (x - xh.astype(F32)).astype(BF16)
    return (jnp.dot(mh, xh, preferred_element_type=F32) + jnp.dot(ml, xh, preferred_element_type=F32)
            + jnp.dot(mh, xl, preferred_element_type=F32))


def _dft_left_kernel(mh_ref, ml_ref, x_ref, o_ref):
    o_ref[...] = _dot3(mh_ref[...], ml_ref[...], x_ref[...])


def _filter_spectrum(fr, fi, c):
    return fr[:, :c] + fr[:, c:], fi[:, :c] - fi[:, c:]


def _dft_left_mul_kernel(mh_ref, ml_ref, x_ref, f_ref, o_ref):
    a = _dot3(mh_ref[...], ml_ref[...], x_ref[...])
    r = a.shape[0] // 2
    ar, ai = a[:r], a[r:]
    kr, ki = _filter_spectrum(f_ref[0:r, :], f_ref[r:, :], a.shape[1])
    o_ref[0:r, :] = ar * kr - ai * ki
    o_ref[r:, :] = ar * ki + ai * kr


def dft_left(mats, x, filt_f=None):
    mh, ml = mats
    r, k = mh.shape
    w = x.shape[1]
    tc = _pick(w, (2048, 1024, 512))
    in_specs = [pl.BlockSpec((r, k), lambda j: (0, 0)), pl.BlockSpec((r, k), lambda j: (0, 0)),
                pl.BlockSpec((k, tc), lambda j: (0, j))]
    args = [mh, ml, x]
    kern = _dft_left_kernel
    if filt_f is not None:
        assert w == tc and filt_f.shape == (r, 2 * w)
        in_specs.append(pl.BlockSpec((r, 2 * w), lambda j: (0, 0)))
        args.append(filt_f)
        kern = _dft_left_mul_kernel
    return pl.pallas_call(
        kern,
        grid=(w // tc,),
        in_specs=in_specs,
        out_specs=pl.BlockSpec((r, tc), lambda j: (0, j)),
        out_shape=jax.ShapeDtypeStruct((r, w), F32),
        compiler_params=_cparams("parallel"),
        name="dft_left",
    )(*args)


def _dft_inv_post_kernel(mh_ref, ml_ref, b_ref, x0_ref, z_ref, bias_ref, scale_ref, o_ref):
    conv = _dot3(mh_ref[...], ml_ref[...], b_ref[...]) * scale_ref[...]
    o_ref[...] = (x0_ref[...] * (conv + z_ref[...] * bias_ref[...])).astype(o_ref.dtype)


def dft_inv_post(mats, b, x0, z, bias_t, scale_t):
    mh, ml = mats
    r, k = mh.shape
    w = b.shape[1]
    tc = _pick(w, (2048, 1024, 512))
    return pl.pallas_call(
        _dft_inv_post_kernel,
        grid=(w // tc,),
        in_specs=[pl.BlockSpec((r, k), lambda j: (0, 0)), pl.BlockSpec((r, k), lambda j: (0, 0)),
                  pl.BlockSpec((k, tc), lambda j: (0, j)),
                  pl.BlockSpec((r, tc), lambda j: (0, j)), pl.BlockSpec((r, tc), lambda j: (0, j)),
                  pl.BlockSpec((1, tc), lambda j: (0, j)), pl.BlockSpec((1, tc), lambda j: (0, j))],
        out_specs=pl.BlockSpec((r, tc), lambda j: (0, j)),
        out_shape=jax.ShapeDtypeStruct((r, w), BF16),
        compiler_params=_cparams("parallel"),
        name="dft_inv_post",
    )(mh, ml, b, x0, z, bias_t, scale_t)


def _fft_mid_kernel(*refs, n2, with_filter):
    if with_filter:
        a_ref, twr_ref, twi_ref, mh_ref, ml_ref, ih_ref, il_ref, k_ref, o_ref = refs
    else:
        a_ref, twr_ref, twi_ref, mh_ref, ml_ref, o_ref = refs
    for s in range(a_ref.shape[1]):
        twr, twi = twr_ref[s], twi_ref[s]
        ar, ai = a_ref[0, s], a_ref[1, s]
        t = jnp.concatenate([twr * ar - twi * ai, twr * ai + twi * ar], axis=0)
        x = _dot3(mh_ref[...], ml_ref[...], t)
        xr, xi = x[:n2], x[n2:]
        if not with_filter:
            o_ref[0, s], o_ref[1, s] = _filter_spectrum(xr, xi, x.shape[1] // 2)
            continue
        kr, ki = k_ref[0, s], k_ref[1, s]
        y = jnp.concatenate([xr * kr - xi * ki, xr * ki + xi * kr], axis=0)
        b = _dot3(ih_ref[...], il_ref[...], y)
        br, bi = b[:n2], b[n2:]
        o_ref[0, s] = twr * br + twi * bi
        o_ref[1, s] = twr * bi - twi * br


def fft_mid(consts, a, spectrum=None):
    _, n1, n2, c = a.shape
    c_out = c if spectrum is not None else c // 2
    kb = _pick(n1, (4, 2, 1))
    slab = pl.BlockSpec((2, kb, n2, c), lambda k: (0, k, 0, 0))
    out_slab = pl.BlockSpec((2, kb, n2, c_out), lambda k: (0, k, 0, 0))
    tw = pl.BlockSpec((kb, n2, 1), lambda k: (k, 0, 0))
    mat = pl.BlockSpec((2 * n2, 2 * n2), lambda k: (0, 0))
    in_specs = [slab, tw, tw, mat, mat]
    args = [a, consts["twr"], consts["twi"], *consts["m2"]]
    if spectrum is not None:
        in_specs += [mat, mat, slab]
        args += [*consts["m2inv"], spectrum]
    return pl.pallas_call(
        functools.partial(_fft_mid_kernel, n2=n2, with_filter=spectrum is not None),
        grid=(n1 // kb,),
        in_specs=in_specs,
        out_specs=out_slab,
        out_shape=jax.ShapeDtypeStruct((2, n1, n2, c_out), F32),
        compiler_params=_cparams("parallel"),
        name="fft_mid",
    )(*args)


def hyena_long_conv(x0, z, h, abs_sum, bias, length):
    c = MIX_W
    n = 2 * length
    n2 = 128 if length >= 1024 else 1
    n1 = n // n2
    consts = _dft_consts(n1, n2)
    nh = _dft_rows(n1)
    inv_scale = 1.0 / ((abs_sum[:, :c] + abs_sum[:, c:]) * n)
    filt_f = dft_left(consts["fwd_half"], h.reshape(n1 // 2, n2 * 2 * c))
    xin = z.reshape(n1 // 2, n2 * c)
    if n2 > 1:
        spec = fft_mid(consts, filt_f.reshape(2, nh, n2, 2 * c))
        a = dft_left(consts["fwd_half"], xin)
        b = fft_mid(consts, a.reshape(2, nh, n2, c), spec).reshape(2 * nh, n2 * c)
    else:
        b = dft_left(consts["fwd_half"], xin, filt_f)
    y = dft_inv_post(consts["inv_half"], b, x0.reshape(n1 // 2, n2 * c), xin,
                     jnp.tile(bias.reshape(1, c), (1, n2)), jnp.tile(inv_scale, (1, n2)))
    return y.reshape(length, c)


def _merge_kernel(*refs):
    ys, gs, ws = refs[0:N_BRANCH], refs[N_BRANCH:2 * N_BRANCH], refs[2 * N_BRANCH:3 * N_BRANCH]
    o_ref, wb_ref = refs[3 * N_BRANCH:]

    @pl.when(pl.program_id(1) == 0)
    def _():
        for i in range(N_BRANCH):
            wb_ref[i] = ws[i][...].astype(BF16)

    acc = None
    for i in range(N_BRANCH):
        t = jnp.dot(ys[i][...], wb_ref[i], preferred_element_type=F32) * _sigmoid(gs[i][...].astype(F32))
        acc = t if acc is None else acc + t
    o_ref[...] = acc.astype(o_ref.dtype)


def merge_branches(ys, gates, gate_col0, branch_w, layer):
    m = ys[0].shape[0]
    d = branch_w.shape[3]
    tm = _pick(m, (640, 512, 256, 128))
    tn = _pick(d, (1024, 512, 256, 128))
    nj = d // tn
    assert gate_col0 % tn == 0
    g0 = gate_col0 // tn
    y_specs = [pl.BlockSpec((tm, MIX_W), lambda j, i: (i, 0))] * N_BRANCH
    g_specs = [pl.BlockSpec((tm, tn), functools.partial(lambda j, i, b: (i, g0 + b * nj + j), b=b)) for b in range(N_BRANCH)]
    w_specs = [pl.BlockSpec((None, None, MIX_W, tn), functools.partial(lambda j, i, b: (layer, b, 0, j), b=b),
                            pipeline_mode=pl.Buffered(1)) for b in range(N_BRANCH)]
    return pl.pallas_call(
        _merge_kernel,
        grid=(nj, m // tm),
        in_specs=y_specs + g_specs + w_specs,
        out_specs=pl.BlockSpec((tm, tn), lambda j, i: (i, j)),
        out_shape=jax.ShapeDtypeStruct((m, d), BF16),
        scratch_shapes=[pltpu.VMEM((N_BRANCH, MIX_W, tn), BF16)],
        compiler_params=_cparams("parallel", "arbitrary"),
        name="merge_branches",
    )(*ys, *([gates] * N_BRANCH), *([branch_w] * N_BRANCH))


FFN_HALO = 16


def _sigmoid(x):
    return 0.5 + 0.5 * jnp.tanh(0.5 * x)


def _ffn_up_kernel(a_ref, wa32_ref, wg32_ref, cwa_ref, cwg_ref, cba_ref, cbg_ref, o_ref, wa_ref, wg_ref,
                   *, tm, chunk, fixes):
    hl = FFN_HALO

    @pl.when(pl.program_id(1) == 0)
    def _():
        wa_ref[...] = wa32_ref[...].astype(BF16)
        wg_ref[...] = wg32_ref[...].astype(BF16)

    def conv(a, rows, w_ref, cw_ref, cb_ref, no_prev=None, no_next=None):
        u = jnp.dot(a, w_ref[...], preferred_element_type=F32)
        up, dn = u[hl - 1:hl - 1 + rows], u[hl + 1:hl + 1 + rows]
        if no_prev is not None:
            up = jnp.where(no_prev, 0.0, up)
            dn = jnp.where(no_next, 0.0, dn)
        return up * cw_ref[0:1, :] + u[hl:hl + rows] * cw_ref[1:2, :] + dn * cw_ref[2:3, :] + cb_ref[...]

    def silu_gate(a, rows, *masks):
        g = conv(a, rows, wg_ref, cwg_ref, cbg_ref, *masks)
        return g * _sigmoid(g)

    def gated(a, rows, *masks):
        return (silu_gate(a, rows, *masks) * conv(a, rows, wa_ref, cwa_ref, cba_ref, *masks)).astype(o_ref.dtype)

    pieces = [a_ref[c * chunk:(c + 1) * chunk + 2 * hl, :] for c in range(tm // chunk)]
    gates = [silu_gate(a, chunk) for a in pieces]
    for c, a in enumerate(pieces):
        o_ref[c * chunk:(c + 1) * chunk, :] = (gates[c] * conv(a, chunk, wa_ref, cwa_ref, cba_ref)).astype(o_ref.dtype)

    for blk, g0, first_row in fixes:
        @pl.when(pl.program_id(1) == blk)
        def _(g0=g0, first_row=first_row):
            row = blk * tm + g0 + lax.broadcasted_iota(jnp.int32, (hl, 1), 0)
            o_ref[g0:g0 + hl, :] = gated(a_ref[g0:g0 + 3 * hl, :], hl, row == first_row, row == first_row - 1)


FFN_TM = 1280


def ffn_up(a_ext, m, w_up, layer, conv_w, conv_b, n_latent):
    d = a_ext.shape[1]
    f = w_up.shape[2] // 2
    tm = FFN_TM
    tn = _pick(f, (512, 256, 128))
    chunk = _pick(tm, (640, 256, 128))
    nb, nj = m // tm, f // tn
    te = tm + 2 * FFN_HALO
    assert n_latent % FFN_HALO == 0 and 0 < n_latent < m and a_ext.shape[0] == nb * te
    fixes = []
    for r in (n_latent - 1, n_latent):
        g = r // FFN_HALO * FFN_HALO
        fixes.append((g // tm, g % tm, n_latent))
    wspec = lambda off: pl.BlockSpec((None, d, tn), lambda j, i: (layer, 0, off + j), pipeline_mode=pl.Buffered(1))
    return pl.pallas_call(
        functools.partial(_ffn_up_kernel, tm=tm, chunk=chunk, fixes=tuple(fixes)),
        grid=(nj, nb),
        in_specs=[pl.BlockSpec((te, d), lambda j, i: (i, 0)),
                  wspec(0), wspec(nj),
                  pl.BlockSpec((3, tn), lambda j, i: (0, j)),
                  pl.BlockSpec((3, tn), lambda j, i: (0, nj + j)),
                  pl.BlockSpec((1, tn), lambda j, i: (0, j)),
                  pl.BlockSpec((1, tn), lambda j, i: (0, nj + j))],
        out_specs=pl.BlockSpec((tm, tn), lambda j, i: (i, j)),
        out_shape=jax.ShapeDtypeStruct((m, f), BF16),
        scratch_shapes=[pltpu.VMEM((d, tn), BF16), pltpu.VMEM((d, tn), BF16)],
        compiler_params=_cparams("parallel", "arbitrary"),
        name="ffn_up",
    )(a_ext, w_up, w_up, conv_w, conv_w, conv_b.reshape(1, -1), conv_b.reshape(1, -1))


def kernel(x, c, ctx, c_ctx, ada_w, ada_b, norm1_g, norm2_g, w_in, a_qn_g, a_kn_g, b_rpb,
           c_gate_w2, c_gate_b, c_norm_g, d_conv_w, d_conv_b, d_ffn_w1, d_ffn_b1, d_ffn_w2,
           d_ffn_b2, d_ffn_w3, d_ffn_b3, d_ffn_w4, d_sin_freq, d_bias, branch_w, w_out,
           ffn_up_w, ffn_conv_w, ffn_conv_b, ffn_down, final_norm_g):
    depth = ada_w.shape[0]
    n_lat, d = x.shape[1], x.shape[2]
    n_ctx = ctx.shape[1]
    h = jnp.concatenate([x[0], ctx[0]], axis=0)
    cvec = jnp.concatenate([c[0:1], c_ctx[None, :], jnp.zeros((14, d), F32)], axis=0)
    cvec = (cvec * jax.nn.sigmoid(cvec)).astype(BF16)
    ctab, stab = rope_tables(n_lat, n_ctx)
    zeros_c = jnp.zeros((n_ctx, MIX_W), BF16)
    wt_in = jnp.swapaxes(w_in, 1, 2)

    for i in range(depth):
        ctx_out = i < depth - 1
        mod = matmul(cvec, ada_w, i, 0, ada_w.shape[2], F32, bias=ada_b[i])[0:2]
        sh1, sc1, g1, sh2, sc2, g2 = jnp.split(mod, 6, axis=-1)
        n1 = mod_norm(h, norm1_g[i], sh1, sc1, n_lat, BF16)
        ca_end = IN_CA + 2 * C_GATE_RANK
        w_ca = jnp.pad(wt_in[i, IN_CA:ca_end, :].T, ((0, 0), (0, 128 - 2 * C_GATE_RANK)))[None]
        p1 = matmul_wt(n1, wt_in, i, 0, PA_WIDTH, F32)
        pb = matmul_wt(n1, wt_in, i, ca_end, PB_WIDTH, F32)
        gates = matmul_wt(n1, wt_in, i, ca_end + PB_WIDTH, N_BRANCH * d, BF16)
        ca = matmul(n1, w_ca, 0, 0, 128, F32)

        q, k, v = a_prep(p1, ctab, stab, a_qn_g[i], a_kn_g[i])
        ya_l = flash_attention(q[:n_lat], k, v, A_KV_HEADS)
        yb_l = neighbourhood_attention(p1, na_bias_table(b_rpb[i]), n_lat, n_ctx)
        yc = gla_mixer(p1, pb, ca, c_gate_w2[i], c_gate_b[i], c_norm_g[i], n_lat, n_ctx)
        x0, z = hyena_pre(pb, d_conv_w[i], d_conv_b[i], n_lat)
        filt = (d_ffn_w1[i], d_ffn_b1[i], d_ffn_w2[i], d_ffn_b2[i], d_ffn_w3[i], d_ffn_b3[i], d_ffn_w4[i],
                d_sin_freq[i])
        hl, sl = hyena_filter(n_lat, *filt)
        yd_l = hyena_long_conv(x0[:n_lat], z[:n_lat], hl, sl, d_bias[i], n_lat)
        if ctx_out:
            ya_c = flash_attention(q[n_lat:], k[n_lat:], v[n_lat:], A_KV_HEADS)
            pc = lax.optimization_barrier(p1[n_lat:])
            yb_c = flash_attention((pc[:, P_BQ:P_BK] * QK_SCALE).astype(BF16), pc[:, P_BK:P_BV].astype(BF16),
                                   with_ones(pc[:, P_BV:P_CQ].astype(BF16), B_HEADS), B_HEADS)
            hc, sc = hyena_filter(n_ctx, *filt)
            yd_c = hyena_long_conv(x0[n_lat:], z[n_lat:], hc, sc, d_bias[i], n_ctx)
        else:
            ya_c = yb_c = yd_c = zeros_c
        ys = [jnp.concatenate([ya_l, ya_c], axis=0), jnp.concatenate([yb_l, yb_c], axis=0), yc,
              jnp.concatenate([yd_l, yd_c], axis=0)]
        merged = merge_branches(ys, gates, 0, branch_w, i)
        h = matmul_residual(merged, w_out, i, h, g1, n_lat)
        n2_ext = mod_norm_halo(h, norm2_g[i], sh2, sc2, n_lat, FFN_TM, FFN_HALO)
        act = ffn_up(n2_ext, h.shape[0], ffn_up_w, i, ffn_conv_w[i], ffn_conv_b[i], n_lat)
        h = matmul_residual(act, ffn_down, i, h, g2, n_lat)

    zero2 = jnp.zeros((2, d), F32)
    out = mod_norm(h, final_norm_g, zero2, zero2, n_lat, F32, n_rows=n_lat)
    return out[None]
```

```python
import functools
import math

import numpy as np
import jax
import jax.numpy as jnp
from jax import lax
from jax.experimental import pallas as pl
from jax.experimental.pallas import tpu as pltpu

F32 = jnp.float32
BF16 = jnp.bfloat16

GRID_W = 64
HEAD_DIM = 128
A_HEADS = 4
A_KV_HEADS = 2
ROPE_THETA = 10000.0
B_HEADS = 4
NA_ROWS = 8
NA_COLS = 16
C_HEADS = 4
C_DK = 64
C_DV = 128
C_GATE_RANK = 16
C_GATE_TAU = 16.0
C_CHUNK = 64
MIX_W = 512
N_BRANCH = 4
HY_EMB = 33
HY_DECAY_MIN = math.log(1e-2) / 1.5
HY_DECAY_MAX = math.log(1e-2) / 0.3
NORM_EPS = 1e-6
NEG_BIG = -1e30

P_AQ, P_AK, P_AV = 0, 512, 768
P_BQ, P_BK, P_BV = 1024, 1536, 2048
P_CQ, P_CK, P_CV = 2560, 2816, 3072
PA_WIDTH = 3584
PB_CG, PB_DU = 0, 512
PB_WIDTH = 2048
IN_CA = PA_WIDTH

VMEM_LIMIT = 56 * 1024 * 1024


def _cparams(*sem):
    return pltpu.CompilerParams(dimension_semantics=sem, vmem_limit_bytes=VMEM_LIMIT)


def _pick(n, candidates):
    for c in candidates:
        if n % c == 0:
            return c
    return n


def _modnorm_rows(x, row0, g_ref, sh_ref, sc_ref, n_latent, m):
    lat = row0 < n_latent
    inside = jnp.logical_and(row0 >= 0, row0 < m)
    gain = jnp.where(inside, g_ref[...] * (1.0 + jnp.where(lat, sc_ref[0], sc_ref[1])), 0.0)
    shift = jnp.where(inside, jnp.where(lat, sh_ref[0], sh_ref[1]), 0.0)
    return x * lax.rsqrt(jnp.mean(x * x, axis=-1, keepdims=True) + NORM_EPS) * gain + shift


def _modnorm_kernel(h_ref, g_ref, sh_ref, sc_ref, o_ref, *, tb, step, n_latent, m):
    for c in range(tb // step):
        rows = slice(c * step, (c + 1) * step)
        y = _modnorm_rows(h_ref[rows, :], pl.program_id(0) * tb + c * step, g_ref, sh_ref, sc_ref, n_latent, m)
        o_ref[rows, :] = y.astype(o_ref.dtype)


def mod_norm(h, g, shift2, scale2, n_latent, out_dtype, n_rows=None):
    d = h.shape[1]
    m = n_rows or h.shape[0]
    tb = _pick(m, (1280, 1024, 512, 256, 128, 64, 8))
    step = _pick(math.gcd(tb, n_latent), (256, 128, 64, 8))
    return pl.pallas_call(
        functools.partial(_modnorm_kernel, tb=tb, step=step, n_latent=n_latent, m=h.shape[0]),
        grid=(m // tb,),
        in_specs=[pl.BlockSpec((tb, d), lambda i: (i, 0)),
                  pl.BlockSpec((1, d), lambda i: (0, 0)),
                  pl.BlockSpec((2, 1, d), lambda i: (0, 0, 0)),
                  pl.BlockSpec((2, 1, d), lambda i: (0, 0, 0))],
        out_specs=pl.BlockSpec((tb, d), lambda i: (i, 0)),
        out_shape=jax.ShapeDtypeStruct((m, d), out_dtype),
        compiler_params=_cparams("parallel"),
        name="mod_norm",
    )(h, g.reshape(1, d), shift2.reshape(2, 1, d), scale2.reshape(2, 1, d))


def _modnorm_halo_kernel(hp_ref, h_ref, hn_ref, g_ref, sh_ref, sc_ref, o_ref, *, tm, halo, n_latent, m):
    i = pl.program_id(0)

    def norm(x, row0):
        return _modnorm_rows(x, row0, g_ref, sh_ref, sc_ref, n_latent, m).astype(o_ref.dtype)

    o_ref[0:halo, :] = norm(hp_ref[...], i * tm - halo)
    step = _pick(math.gcd(tm, n_latent), (256, 128, 64, 16))
    for c in range(tm // step):
        o_ref[halo + c * step:halo + (c + 1) * step, :] = norm(h_ref[c * step:(c + 1) * step, :], i * tm + c * step)
    o_ref[halo + tm:, :] = norm(hn_ref[...], (i + 1) * tm)


def mod_norm_halo(h, g, shift2, scale2, n_latent, tm, halo):
    m, d = h.shape
    nb = m // tm
    hb = tm // halo
    return pl.pallas_call(
        functools.partial(_modnorm_halo_kernel, tm=tm, halo=halo, n_latent=n_latent, m=m),
        grid=(nb,),
        in_specs=[pl.BlockSpec((halo, d), lambda i: (jnp.maximum(i * hb - 1, 0), 0)),
                  pl.BlockSpec((tm, d), lambda i: (i, 0)),
                  pl.BlockSpec((halo, d), lambda i: (jnp.minimum((i + 1) * hb, m // halo - 1), 0)),
                  pl.BlockSpec((1, d), lambda i: (0, 0)),
                  pl.BlockSpec((2, 1, d), lambda i: (0, 0, 0)),
                  pl.BlockSpec((2, 1, d), lambda i: (0, 0, 0))],
        out_specs=pl.BlockSpec((tm + 2 * halo, d), lambda i: (i, 0)),
        out_shape=jax.ShapeDtypeStruct((nb * (tm + 2 * halo), d), BF16),
        compiler_params=_cparams("parallel"),
        name="mod_norm_halo",
    )(h, h, h, g.reshape(1, d), shift2.reshape(2, 1, d), scale2.reshape(2, 1, d))


VMEM_TILE_BUDGET = 46 * 1024 * 1024


def _dense_tiles(m, k, n, col0, extra_tiles, out_bytes):
    for tms in ((1280, 1024, 640), (512, 320, 256, 128, 16)):
        for tn in (1792, 1024, 512, 256, 128):
            if n % tn or col0 % tn:
                continue
            for tm in tms:
                if m % tm:
                    continue
                blocks = 2 * tm * k * 2 + k * tn * (4 + 2) + 2 * tm * tn * (out_bytes + 4 * extra_tiles)
                if blocks <= VMEM_TILE_BUDGET:
                    return tm, tn
    raise ValueError("no dense tiling fits")


def _cast_at_first_row_block(w_ref, wb_ref):
    @pl.when(pl.program_id(1) == 0)
    def _():
        wb_ref[...] = w_ref[...].astype(BF16)


def _mm_kernel(a_ref, w_ref, o_ref, wb_ref):
    _cast_at_first_row_block(w_ref, wb_ref)
    o_ref[...] = jnp.dot(a_ref[...], wb_ref[...], preferred_element_type=F32).astype(o_ref.dtype)


def _mm_bias_kernel(a_ref, w_ref, b_ref, o_ref, wb_ref):
    _cast_at_first_row_block(w_ref, wb_ref)
    o_ref[...] = (jnp.dot(a_ref[...], wb_ref[...], preferred_element_type=F32) + b_ref[...]).astype(o_ref.dtype)


def matmul(a, w3, layer, col0, n, out_dtype, bias=None):
    m, k = a.shape
    tm, tn = _dense_tiles(m, k, n, col0, extra_tiles=0, out_bytes=jnp.dtype(out_dtype).itemsize)
    assert col0 % tn == 0 and w3.shape[1] == k
    c0 = col0 // tn
    in_specs = [pl.BlockSpec((tm, k), lambda j, i: (i, 0)),
                pl.BlockSpec((None, k, tn), lambda j, i: (layer, 0, c0 + j), pipeline_mode=pl.Buffered(1))]
    args = [a, w3]
    kern = _mm_kernel
    if bias is not None:
        in_specs.append(pl.BlockSpec((1, tn), lambda j, i: (0, j)))
        args.append(bias.reshape(1, n))
        kern = _mm_bias_kernel
    return pl.pallas_call(
        kern,
        grid=(n // tn, m // tm),
        in_specs=in_specs,
        out_specs=pl.BlockSpec((tm, tn), lambda j, i: (i, j)),
        out_shape=jax.ShapeDtypeStruct((m, n), out_dtype),
        scratch_shapes=[pltpu.VMEM((k, tn), BF16)],
        compiler_params=_cparams("parallel", "arbitrary"),
        name="matmul",
    )(*args)


WT_PIECE, WT_TAIL = 256, 128


def _mm_wt_kernel(*refs, n_pieces, shift, tn):
    a_ref, w_refs = refs[0], refs[1:1 + n_pieces]
    o_ref, wb_ref = refs[1 + n_pieces:]

    @pl.when(pl.program_id(1) == 0)
    def _():
        x = jnp.concatenate([r[...] for r in w_refs], axis=0)
        wb_ref[...] = x[shift:shift + tn].astype(BF16)

    o_ref[...] = lax.dot_general(a_ref[...], wb_ref[...], (((1,), (1,)), ((), ())),
                                 preferred_element_type=F32).astype(o_ref.dtype)


def matmul_wt(a, wt3, layer, col0, n, out_dtype):
    m, k = a.shape
    shift = col0 % WT_TAIL
    base = col0 - shift
    assert base % WT_PIECE == 0 and shift % 8 == 0 and col0 + n <= wt3.shape[1] and wt3.shape[2] == k
    tm, tn = _dense_tiles(m, k, n, 0, extra_tiles=0, out_bytes=jnp.dtype(out_dtype).itemsize)
    assert tn % WT_PIECE == 0
    per = tn // WT_PIECE
    single = dict(pipeline_mode=pl.Buffered(1))
    w_specs = [pl.BlockSpec((None, WT_PIECE, k), functools.partial(
        lambda j, i, p: (layer, base // WT_PIECE + j * per + p, 0), p=p), **single) for p in range(per)]
    if shift:
        w_specs.append(pl.BlockSpec((None, WT_TAIL, k),
                                    lambda j, i: (layer, base // WT_TAIL + (j + 1) * (tn // WT_TAIL), 0), **single))
    return pl.pallas_call(
        functools.partial(_mm_wt_kernel, n_pieces=len(w_specs), shift=shift, tn=tn),
        grid=(n // tn, m // tm),
        in_specs=[pl.BlockSpec((tm, k), lambda j, i: (i, 0))] + w_specs,
        out_specs=pl.BlockSpec((tm, tn), lambda j, i: (i, j)),
        out_shape=jax.ShapeDtypeStruct((m, n), out_dtype),
        scratch_shapes=[pltpu.VMEM((tn, k), BF16)],
        compiler_params=_cparams("parallel", "arbitrary"),
        name="matmul_wt",
    )(a, *([wt3] * len(w_specs)))


def _mm_res_kernel(a_ref, w_ref, h_ref, g_ref, o_ref, wb_ref, *, n_latent, tm):
    _cast_at_first_row_block(w_ref, wb_ref)
    row = pl.program_id(1) * tm + lax.broadcasted_iota(jnp.int32, (tm, 1), 0)
    gate = jnp.where(row < n_latent, g_ref[0], g_ref[1])
    o_ref[...] = h_ref[...] + gate * jnp.dot(a_ref[...], wb_ref[...], preferred_element_type=F32)


def matmul_residual(a, w3, layer, h, gate2, n_latent):
    m, k = a.shape
    n = w3.shape[2]
    tm, tn = _dense_tiles(m, k, n, 0, extra_tiles=1, out_bytes=4)
    return pl.pallas_call(
        functools.partial(_mm_res_kernel, n_latent=n_latent, tm=tm),
        grid=(n // tn, m // tm),
        in_specs=[pl.BlockSpec((tm, k), lambda j, i: (i, 0)),
                  pl.BlockSpec((None, k, tn), lambda j, i: (layer, 0, j), pipeline_mode=pl.Buffered(1)),
                  pl.BlockSpec((tm, tn), lambda j, i: (i, j)),
                  pl.BlockSpec((2, 1, tn), lambda j, i: (0, 0, j))],
        out_specs=pl.BlockSpec((tm, tn), lambda j, i: (i, j)),
        out_shape=jax.ShapeDtypeStruct((m, n), F32),
        scratch_shapes=[pltpu.VMEM((k, tn), BF16)],
        compiler_params=_cparams("parallel", "arbitrary"),
        name="matmul_residual",
    )(a, w3, h, gate2.reshape(2, 1, n))


def _swap_halves(y):
    lane = lax.broadcasted_iota(jnp.int32, y.shape, 1)
    return jnp.where((lane % 64) < 32, pltpu.roll(y, 96, 1), pltpu.roll(y, 32, 1))


def _aprep_kernel(p_ref, cos_ref, sin_ref, qg_ref, kg_ref, q_ref, k_ref, v_ref):
    c = cos_ref[...]
    s = sin_ref[...]

    def norm_rope(x, g):
        y = x * lax.rsqrt(jnp.mean(x * x, axis=-1, keepdims=True) + NORM_EPS) * g
        return y * c + _swap_halves(y) * s

    for hd in range(A_HEADS):
        x = p_ref[:, hd * HEAD_DIM:(hd + 1) * HEAD_DIM]
        q_ref[:, hd * HEAD_DIM:(hd + 1) * HEAD_DIM] = (
            norm_rope(x, qg_ref[...]) * QK_SCALE).astype(q_ref.dtype)
    for hd in range(A_KV_HEADS):
        x = p_ref[:, P_AK + hd * HEAD_DIM:P_AK + (hd + 1) * HEAD_DIM]
        k_ref[:, hd * HEAD_DIM:(hd + 1) * HEAD_DIM] = norm_rope(x, kg_ref[...]).astype(k_ref.dtype)
        v = p_ref[:, P_AV + hd * HEAD_DIM:P_AV + (hd + 1) * HEAD_DIM]
        v_ref[:, 2 * hd * HEAD_DIM:(2 * hd + 1) * HEAD_DIM] = v.astype(v_ref.dtype)
        v_ref[:, (2 * hd + 1) * HEAD_DIM:(2 * hd + 2) * HEAD_DIM] = jnp.ones_like(v).astype(v_ref.dtype)


def rope_tables(n_latent, n_ctx):
    pos = np.arange(n_latent)
    axes = np.stack([pos // GRID_W, pos % GRID_W], axis=-1).astype(np.float32)
    quarter = HEAD_DIM // 4
    inv_freq = (np.float32(ROPE_THETA) ** (-np.arange(quarter, dtype=np.float32) / quarter)).astype(np.float32)
    ang = (axes[:, :, None] * inv_freq).astype(np.float32)
    cos, sin = np.cos(ang), np.sin(ang)
    ctab = np.concatenate([cos[:, 0], cos[:, 0], cos[:, 1], cos[:, 1]], axis=-1)
    stab = np.concatenate([-sin[:, 0], sin[:, 0], -sin[:, 1], sin[:, 1]], axis=-1)
    ctab = np.concatenate([ctab, np.ones((n_ctx, HEAD_DIM), np.float32)], axis=0)
    stab = np.concatenate([stab, np.zeros((n_ctx, HEAD_DIM), np.float32)], axis=0)
    return jnp.asarray(ctab, F32), jnp.asarray(stab, F32)


def a_prep(p1, ctab, stab, qn_g, kn_g):
    m = p1.shape[0]
    tb = _pick(m, (256, 128, 64, 8))
    wa = 1024
    return pl.pallas_call(
        _aprep_kernel,
        grid=(m // tb,),
        in_specs=[pl.BlockSpec((tb, wa), lambda i: (i, 0)),
                  pl.BlockSpec((tb, HEAD_DIM), lambda i: (i, 0)),
                  pl.BlockSpec((tb, HEAD_DIM), lambda i: (i, 0)),
                  pl.BlockSpec((1, HEAD_DIM), lambda i: (0, 0)),
                  pl.BlockSpec((1, HEAD_DIM), lambda i: (0, 0))],
        out_specs=[pl.BlockSpec((tb, 512), lambda i: (i, 0)),
                   pl.BlockSpec((tb, 256), lambda i: (i, 0)),
                   pl.BlockSpec((tb, 512), lambda i: (i, 0))],
        out_shape=[jax.ShapeDtypeStruct((m, 512), BF16),
                   jax.ShapeDtypeStruct((m, 256), BF16),
                   jax.ShapeDtypeStruct((m, 512), BF16)],
        compiler_params=_cparams("parallel"),
        name="a_prep",
    )(p1, ctab, stab, qn_g.reshape(1, HEAD_DIM), kn_g.reshape(1, HEAD_DIM))


LOG2E = 1.4426950408889634
QK_SCALE = HEAD_DIM ** -0.5 * LOG2E


def _flash_kernel(q_ref, k_ref, v_ref, o_ref, m_ref, acc_ref, sa_ref, sb_ref, *, group, tk, nk):
    m_ref[...] = jnp.full_like(m_ref, NEG_BIG)
    acc_ref[...] = jnp.zeros_like(acc_ref)

    def scores(j, s_ref):
        k = k_ref[pl.ds(pl.multiple_of(j * tk, tk), tk), :]
        for g in range(group):
            q = q_ref[:, g * HEAD_DIM:(g + 1) * HEAD_DIM]
            s_ref[g] = lax.dot_general(q, k, (((1,), (1,)), ((), ())), preferred_element_type=F32)

    def update(j, s_ref):
        v = v_ref[pl.ds(pl.multiple_of(j * tk, tk), tk), :]
        for g in range(group):
            s = s_ref[g]
            m_prev = m_ref[g]
            m_new = jnp.maximum(m_prev, jnp.max(s, axis=-1, keepdims=True))
            p = jnp.exp2(s - m_new).astype(BF16)
            acc_ref[g] = jnp.exp2(m_prev - m_new) * acc_ref[g] + jnp.dot(p, v, preferred_element_type=F32)
            m_ref[g] = m_new

    scores(0, sa_ref)
    pairs = (nk - 1) // 2

    def body(jj, carry):
        j = 2 * jj
        scores(j + 1, sb_ref)
        update(j, sa_ref)
        scores(j + 2, sa_ref)
        update(j + 1, sb_ref)
        return carry

    lax.fori_loop(0, pairs, body, 0)
    if (nk - 1) % 2 == 1:
        scores(nk - 1, sb_ref)
        update(nk - 2, sa_ref)
        update(nk - 1, sb_ref)
    else:
        update(nk - 1, sa_ref)
    for g in range(group):
        acc = acc_ref[g]
        o_ref[:, g * HEAD_DIM:(g + 1) * HEAD_DIM] = (acc[:, :HEAD_DIM] / acc[:, HEAD_DIM:]).astype(o_ref.dtype)


def with_ones(v, heads):
    n = v.shape[0]
    v3 = v.reshape(n, heads, HEAD_DIM)
    return jnp.concatenate([v3, jnp.ones_like(v3)], axis=-1).reshape(n, heads * 2 * HEAD_DIM)


def flash_attention(q, k, v1, kv_heads, tq=None, tk=None, n_rows=None):
    lq, hq = n_rows or q.shape[0], q.shape[1]
    lk = k.shape[0]
    group = hq // (kv_heads * HEAD_DIM)
    tq = tq or _pick(lq, (1024, 512, 256, 128))
    tk = tk or _pick(lk, (1280, 1024, 512, 256, 128))
    nk = lk // tk
    gw = group * HEAD_DIM
    return pl.pallas_call(
        functools.partial(_flash_kernel, group=group, tk=tk, nk=nk),
        grid=(kv_heads, lq // tq),
        in_specs=[pl.BlockSpec((tq, gw), lambda h, i: (i, h)),
                  pl.BlockSpec((lk, HEAD_DIM), lambda h, i: (0, h), pipeline_mode=pl.Buffered(1)),
                  pl.BlockSpec((lk, 2 * HEAD_DIM), lambda h, i: (0, h), pipeline_mode=pl.Buffered(1))],
        out_specs=pl.BlockSpec((tq, gw), lambda h, i: (i, h)),
        out_shape=jax.ShapeDtypeStruct((lq, hq), BF16),
        scratch_shapes=[pltpu.VMEM((group, tq, 1), F32),
                        pltpu.VMEM((group, tq, 2 * HEAD_DIM), F32),
                        pltpu.VMEM((group, tq, tk), F32),
                        pltpu.VMEM((group, tq, tk), F32)],
        compiler_params=_cparams("parallel", "parallel"),
        name="flash_attention",
    )(q, k, v1)


NA_GROUP = 4


def na_bias_table(rpb):
    col = np.arange(GRID_W)
    c_start = np.clip(col - NA_COLS // 2, 0, GRID_W - NA_COLS)
    kc = np.arange(GRID_W)
    col_ok = (kc[None, :] >= c_start[:, None]) & (kc[None, :] < c_start[:, None] + NA_COLS)
    dc = np.clip(kc[None, :] - col[:, None] + (NA_COLS - 1), 0, 2 * NA_COLS - 2)
    i = np.arange(NA_GROUP)
    j = np.arange(3 * NA_GROUP)
    dr = np.clip(j[None, :] - i[:, None] + NA_ROWS // 2 - 1, 0, 2 * NA_ROWS - 2)
    first = np.broadcast_to((j >= NA_GROUP)[None, :], dr.shape)
    inner = (j[None, :] >= i[:, None]) & (j[None, :] < i[:, None] + NA_ROWS)
    last = np.broadcast_to((j < NA_ROWS)[None, :], dr.shape)
    pick_r = jnp.asarray(np.eye(2 * NA_ROWS - 1, dtype=np.float32)[dr])
    pick_c = jnp.asarray(np.eye(2 * NA_COLS - 1, dtype=np.float32)[dc])
    bias = jnp.einsum("ijr,hrs,cks->hijck", pick_r, rpb.astype(F32), pick_c, precision=lax.Precision.HIGHEST)
    tabs = []
    for row_ok in (first, inner, last):
        ok = row_ok[:, :, None, None] & col_ok[None, None, :, :]
        t = jnp.where(jnp.asarray(ok)[None], bias * LOG2E, NEG_BIG)
        tabs.append(t.transpose(0, 1, 3, 2, 4).reshape(rpb.shape[0], NA_GROUP * GRID_W, 3 * NA_GROUP * GRID_W))
    return jnp.stack(tabs)


def _na_kernel(q_ref, k0_ref, k1_ref, k2_ref, v0_ref, v1_ref, v2_ref, kc_ref, vc_ref, b_ref, o_ref):
    k_u = jnp.concatenate([r[...].astype(BF16) for r in (k0_ref, k1_ref, k2_ref)], axis=0)
    v_u = jnp.concatenate([r[...].astype(BF16) for r in (v0_ref, v1_ref, v2_ref)], axis=0)
    k_c = kc_ref[...].astype(BF16)
    v_c = vc_ref[...].astype(BF16)
    nt = (((1,), (1,)), ((), ()))
    for hd in range(B_HEADS):
        sl = slice(hd * HEAD_DIM, (hd + 1) * HEAD_DIM)
        q = (q_ref[:, sl] * QK_SCALE).astype(BF16)
        s_loc = lax.dot_general(q, k_u[:, sl], nt, preferred_element_type=F32) + b_ref[hd]
        s_ctx = lax.dot_general(q, k_c[:, sl], nt, preferred_element_type=F32)
        mx = jnp.maximum(jnp.max(s_loc, axis=-1, keepdims=True), jnp.max(s_ctx, axis=-1, keepdims=True))
        p_loc = jnp.exp2(s_loc - mx)
        p_ctx = jnp.exp2(s_ctx - mx)
        den = jnp.sum(p_loc, axis=-1, keepdims=True) + jnp.sum(p_ctx, axis=-1, keepdims=True)
        o = (jnp.dot(p_loc.astype(BF16), v_u[:, sl], preferred_element_type=F32)
             + jnp.dot(p_ctx.astype(BF16), v_c[:, sl], preferred_element_type=F32))
        o_ref[:, sl] = (o / den).astype(o_ref.dtype)


def neighbourhood_attention(p1, bias_tab, n_latent, n_ctx):
    rows = n_latent // GRID_W
    ng = rows // NA_GROUP
    assert rows % NA_GROUP == 0 and rows >= 2 * NA_ROWS and NA_ROWS == 2 * NA_GROUP
    w = B_HEADS * HEAD_DIM
    t = NA_GROUP * GRID_W
    cq, ck, cv = P_BQ // w, P_BK // w, P_BV // w

    def kv_specs(col):
        return [pl.BlockSpec((t, w), functools.partial(lambda g, d, col: (jnp.clip(g + d, 0, ng - 1), col), d=d, col=col))
                for d in (-1, 0, 1)]

    cblk = n_latent // n_ctx
    return pl.pallas_call(
        _na_kernel,
        grid=(ng,),
        in_specs=[pl.BlockSpec((t, w), lambda g: (g, cq))] + kv_specs(ck) + kv_specs(cv) + [
            pl.BlockSpec((n_ctx, w), lambda g: (cblk, ck)),
            pl.BlockSpec((n_ctx, w), lambda g: (cblk, cv)),
            pl.BlockSpec((None,) + bias_tab.shape[1:],
                         lambda g: (jnp.where(g == 0, 0, jnp.where(g == ng - 1, 2, 1)), 0, 0, 0))],
        out_specs=pl.BlockSpec((t, w), lambda g: (g, 0)),
        out_shape=jax.ShapeDtypeStruct((n_latent, w), BF16),
        compiler_params=_cparams("parallel"),
        name="neighbourhood_attention",
    )(*([p1] * 9), bias_tab)


GLA_BLOCK = 4 * C_CHUNK


def _gla_block(q_ref, k_ref, v_ref, a_ref, w2_ref, gb_ref, st_ref, reverse, emit):
    c = C_CHUNK
    nch = GLA_BLOCK // c
    x = jnp.dot(a_ref[...].astype(BF16), w2_ref[...], preferred_element_type=F32) + gb_ref[...]
    la = (jnp.minimum(x, 0.0) - jnp.log(1.0 + jnp.exp(-jnp.abs(x)))) * (1.0 / C_GATE_TAU)
    pos = lax.broadcasted_iota(jnp.int32, la.shape, 0) % c
    b = la
    sh = 1
    while sh < c:
        if reverse:
            b = b + jnp.where(pos < c - sh, pltpu.roll(b, GLA_BLOCK - sh, 0), 0.0)
        else:
            b = b + jnp.where(pos >= sh, pltpu.roll(b, sh, 0), 0.0)
        sh *= 2
    qs = q_ref[...] * (C_DK ** -0.5)
    kk = k_ref[...]
    q_in = (qs * jnp.exp(b)).astype(BF16)
    ii = lax.broadcasted_iota(jnp.int32, (c, c), 0)
    jj = lax.broadcasted_iota(jnp.int32, (c, c), 1)
    mask = (jj >= ii) if reverse else (jj <= ii)
    nt = (((1,), (1,)), ((), ()))
    tn = (((0,), (0,)), ((), ()))
    states = [st_ref[hd] for hd in range(C_HEADS)]
    for ch in (range(nch - 1, -1, -1) if reverse else range(nch)):
        rows = slice(ch * c, (ch + 1) * c)
        bc = b[rows]
        b_mid = bc[c // 2:c // 2 + 1, :]
        b_last = bc[0:1, :] if reverse else bc[c - 1:c, :]
        q_t = (qs[rows] * jnp.exp(bc - b_mid)).astype(BF16)
        k_t = (kk[rows] * jnp.exp(b_mid - bc)).astype(BF16)
        k_d = (kk[rows] * jnp.exp(b_last - bc)).astype(BF16)
        e_last = jnp.exp(b_last)
        for hd in range(C_HEADS):
            ks = slice(hd * C_DK, (hd + 1) * C_DK)
            v = v_ref[rows, hd * C_DV:(hd + 1) * C_DV].astype(BF16)
            att = lax.dot_general(q_t[:, ks], k_t[:, ks], nt, preferred_element_type=F32)
            att = jnp.where(mask, att, 0.0).astype(BF16)
            o = (lax.dot_general(q_in[rows, ks], states[hd].astype(BF16), nt, preferred_element_type=F32)
                 + jnp.dot(att, v, preferred_element_type=F32))
            states[hd] = e_last[:, ks] * states[hd] + lax.dot_general(v, k_d[:, ks], tn, preferred_element_type=F32)
            emit(rows, hd, o)
    for hd in range(C_HEADS):
        st_ref[hd] = states[hd]


def _gla_scan_kernel(qf_ref, kf_ref, vf_ref, af_ref, qr_ref, kr_ref, vr_ref, ar_ref, w2_ref, gb_ref,
                     of_ref, or_ref, stf_ref, str_ref):
    @pl.when(pl.program_id(0) == 0)
    def _():
        stf_ref[...] = jnp.zeros_like(stf_ref)
        str_ref[...] = jnp.zeros_like(str_ref)

    def emit_to(o_ref):
        def emit(rows, hd, o):
            o_ref[rows, hd * C_DV:(hd + 1) * C_DV] = o
        return emit

    _gla_block(qf_ref, kf_ref, vf_ref, af_ref, w2_ref.at[0], gb_ref.at[0], stf_ref, False, emit_to(of_ref))
    _gla_block(qr_ref, kr_ref, vr_ref, ar_ref, w2_ref.at[1], gb_ref.at[1], str_ref, True, emit_to(or_ref))


def _gla_out_kernel(of_ref, or_ref, g_ref, ng_ref, y_ref):
    g = g_ref[...]
    silu = g * _sigmoid(g)
    for hd in range(C_HEADS):
        vs = slice(hd * C_DV, (hd + 1) * C_DV)
        x = of_ref[:, vs] + or_ref[:, vs]
        y = x * lax.rsqrt(jnp.mean(x * x, axis=-1, keepdims=True) + NORM_EPS) * ng_ref[...]
        y_ref[:, vs] = (y * silu[:, vs]).astype(y_ref.dtype)


def gla_mixer(p1, pb, ca, gate_w2, gate_b, norm_g, n_latent, n_ctx):
    m = p1.shape[0]
    c = GLA_BLOCK
    assert n_latent % c == 0 and n_ctx % c == 0
    nl, nc = n_latent // c, n_ctx // c
    kw = C_HEADS * C_DK
    vw = C_HEADS * C_DV

    def blk_f(s):
        return jnp.where(s < nc, nl + s, s - nc)

    def blk_r(s):
        return nl + nc - 1 - s

    w2 = jnp.zeros((2, 128, kw), F32)
    w2 = w2.at[0, 0:C_GATE_RANK].set(gate_w2[0]).at[1, C_GATE_RANK:2 * C_GATE_RANK].set(gate_w2[1]).astype(BF16)
    gb = gate_b.reshape(2, 1, kw).astype(F32)

    def common_specs(blk):
        return [pl.BlockSpec((c, kw), lambda s: (blk(s), P_CQ // kw)),
                pl.BlockSpec((c, kw), lambda s: (blk(s), P_CK // kw)),
                pl.BlockSpec((c, vw), lambda s: (blk(s), P_CV // vw)),
                pl.BlockSpec((c, 128), lambda s: (blk(s), 0))]

    state = pltpu.VMEM((C_HEADS, C_DV, C_DK), F32)
    o_f, o_r = pl.pallas_call(
        _gla_scan_kernel,
        grid=(nl + nc,),
        in_specs=common_specs(blk_f) + common_specs(blk_r) + [pl.BlockSpec((2, 128, kw), lambda s: (0, 0, 0)),
                                                              pl.BlockSpec((2, 1, kw), lambda s: (0, 0, 0))],
        out_specs=[pl.BlockSpec((c, vw), lambda s: (blk_f(s), 0)),
                   pl.BlockSpec((c, vw), lambda s: (blk_r(s), 0))],
        out_shape=[jax.ShapeDtypeStruct((m, vw), F32)] * 2,
        scratch_shapes=[state, state],
        compiler_params=_cparams("arbitrary"),
        name="gla_scan",
    )(p1, p1, p1, ca, p1, p1, p1, ca, w2, gb)
    tb = _pick(m, (1280, 1024, 512, 256))
    return pl.pallas_call(
        _gla_out_kernel,
        grid=(m // tb,),
        in_specs=[pl.BlockSpec((tb, vw), lambda i: (i, 0)),
                  pl.BlockSpec((tb, vw), lambda i: (i, 0)),
                  pl.BlockSpec((tb, vw), lambda i: (i, PB_CG // vw)),
                  pl.BlockSpec((1, C_DV), lambda i: (0, 0))],
        out_specs=pl.BlockSpec((tb, vw), lambda i: (i, 0)),
        out_shape=jax.ShapeDtypeStruct((m, vw), BF16),
        compiler_params=_cparams("parallel"),
        name="gla_out",
    )(o_f, o_r, pb, norm_g.reshape(1, C_DV))


def _conv3_rows(x, prev_row, next_row, w_ref, b_ref):
    t = x.shape[0]
    row = lax.broadcasted_iota(jnp.int32, x.shape, 0)
    up = jnp.where(row == 0, prev_row, pltpu.roll(x, 1, 0))
    dn = jnp.where(row == t - 1, next_row, pltpu.roll(x, t - 1, 0))
    return up * w_ref[0:1, :] + x * w_ref[1:2, :] + dn * w_ref[2:3, :] + b_ref[...]


def _hy_pre_kernel(*refs, tb, seq_starts, seq_ends):
    mains, prevs, nexts = refs[0:3], refs[3:6], refs[6:9]
    w_refs, b_refs = refs[9:12], refs[12:15]
    x0_ref, z_ref = refs[15:17]
    r0 = pl.program_id(0) * tb
    pvalid = jnp.where(functools.reduce(jnp.logical_or, [r0 == s for s in seq_starts]), 0.0, 1.0)
    nvalid = jnp.where(functools.reduce(jnp.logical_or, [r0 + tb == e for e in seq_ends]), 0.0, 1.0)
    res = []
    for t in range(3):
        res.append(_conv3_rows(mains[t][...], prevs[t][7:8, :] * pvalid, nexts[t][0:1, :] * nvalid,
                               w_refs[t], b_refs[t]))
    x0_ref[...] = res[0]
    z_ref[...] = res[2] * res[1]


def hyena_pre(p1, conv_w, conv_b, n_latent):
    m = p1.shape[0]
    w = MIX_W
    tb = _pick(math.gcd(n_latent, m), (256, 128, 64, 8))
    nb8 = m // 8
    c0 = PB_DU // w
    mains = [pl.BlockSpec((tb, w), functools.partial(lambda i, t: (i, c0 + t), t=t)) for t in range(3)]
    prevs = [pl.BlockSpec((8, w), functools.partial(lambda i, t: (jnp.maximum(i * (tb // 8) - 1, 0), c0 + t), t=t))
             for t in range(3)]
    nexts = [pl.BlockSpec((8, w), functools.partial(lambda i, t: (jnp.minimum((i + 1) * (tb // 8), nb8 - 1), c0 + t), t=t))
             for t in range(3)]
    wspecs = [pl.BlockSpec((3, w), functools.partial(lambda i, t: (0, t), t=t)) for t in range(3)]
    bspecs = [pl.BlockSpec((1, w), functools.partial(lambda i, t: (0, t), t=t)) for t in range(3)]
    return pl.pallas_call(
        functools.partial(_hy_pre_kernel, tb=tb, seq_starts=(0, n_latent), seq_ends=(n_latent, m)),
        grid=(m // tb,),
        in_specs=mains + prevs + nexts + wspecs + bspecs,
        out_specs=[pl.BlockSpec((tb, w), lambda i: (i, 0))] * 2,
        out_shape=[jax.ShapeDtypeStruct((m, w), F32)] * 2,
        compiler_params=_cparams("parallel"),
        name="hyena_pre",
    )(*([p1] * 9), *([conv_w] * 3), *([conv_b.reshape(1, -1)] * 3))


def _hy_filter_kernel(z_ref, w1_ref, b1_ref, w2_ref, b2_ref, w3_ref, b3_ref, w4_ref, fr_ref, dl_ref,
                      h_ref, s_ref):
    def dot3(x, w):
        xh, wh = x.astype(BF16), w.astype(BF16)
        xl, wl = (x - xh.astype(F32)).astype(BF16), (w - wh.astype(F32)).astype(BF16)
        return (jnp.dot(xh, wh, preferred_element_type=F32) + jnp.dot(xh, wl, preferred_element_type=F32)
                + jnp.dot(xl, wh, preferred_element_type=F32))

    z = z_ref[...]
    fr = fr_ref[...]
    hid = jnp.sin(fr * (dot3(z, w1_ref[...]) + b1_ref[...]))
    hid = jnp.sin(fr * (dot3(hid, w2_ref[...]) + b2_ref[...]))
    hid = jnp.sin(fr * (dot3(hid, w3_ref[...]) + b3_ref[...]))
    h = dot3(hid, w4_ref[...])
    h = h * jnp.exp(-z[:, 0:1] * dl_ref[...])
    row = pl.program_id(0) * h.shape[0] + lax.broadcasted_iota(jnp.int32, h.shape, 0)
    col = lax.broadcasted_iota(jnp.int32, h.shape, 1)
    h_ref[...] = jnp.where((row == 0) & (col >= MIX_W), 0.0, h)

    @pl.when(pl.program_id(0) == 0)
    def _():
        s_ref[...] = jnp.zeros_like(s_ref)

    s_ref[...] += jnp.sum(jnp.abs(h), axis=0, keepdims=True)


def hyena_filter(length, w1, b1, w2, b2, w3, b3, w4, freq):
    bands = (HY_EMB - 1) // 2
    t = np.linspace(0.0, 1.0, length, dtype=np.float32)[:, None]
    wv = (np.float32(2.0 * math.pi) * np.arange(length, dtype=np.float32)[:, None] / np.float32(length)).astype(np.float32)
    f = np.linspace(1e-4, bands - 1, bands, dtype=np.float32)
    zt = np.concatenate([t, np.cos(f * wv), -np.sin(f * wv)], axis=-1).astype(np.float32)
    z = np.zeros((length, 128), np.float32)
    z[:, :HY_EMB] = zt
    deltas = np.abs(np.linspace(HY_DECAY_MIN, HY_DECAY_MAX, MIX_W, dtype=np.float32))
    dl = np.concatenate([deltas, deltas])[None, :].astype(np.float32)

    def pad2(a, r, c):
        return jnp.zeros((r, c), F32).at[:a.shape[0], :a.shape[1]].set(a.astype(F32))

    hf = w1.shape[1]
    tb = _pick(length, (512, 256, 128, 8))
    full = lambda i: (0, 0)
    return pl.pallas_call(
        _hy_filter_kernel,
        grid=(length // tb,),
        in_specs=[pl.BlockSpec((tb, 128), lambda i: (i, 0)),
                  pl.BlockSpec((128, 128), full), pl.BlockSpec((1, 128), full),
                  pl.BlockSpec((128, 128), full), pl.BlockSpec((1, 128), full),
                  pl.BlockSpec((128, 128), full), pl.BlockSpec((1, 128), full),
                  pl.BlockSpec((128, 2 * MIX_W), full), pl.BlockSpec((1, 128), full),
                  pl.BlockSpec((1, 2 * MIX_W), full)],
        out_specs=[pl.BlockSpec((tb, 2 * MIX_W), lambda i: (i, 0)),
                   pl.BlockSpec((1, 2 * MIX_W), full)],
        out_shape=[jax.ShapeDtypeStruct((length, 2 * MIX_W), F32),
                   jax.ShapeDtypeStruct((1, 2 * MIX_W), F32)],
        compiler_params=_cparams("arbitrary"),
        name="hyena_filter",
    )(jnp.asarray(z), pad2(w1, 128, 128), pad2(b1[None], 1, 128), pad2(w2, 128, 128), pad2(b2[None], 1, 128),
      pad2(w3, 128, 128), pad2(b3[None], 1, 128), pad2(w4, 128, 2 * MIX_W), pad2(freq[None], 1, 128),
      jnp.asarray(dl))


def _split_hi_lo(m):
    m = np.asarray(m, np.float32)
    hi = jnp.asarray(m, F32).astype(BF16)
    lo = (jnp.asarray(m, F32) - hi.astype(F32)).astype(BF16)
    return hi, lo


def _dft_rows(n1):
    return -(-(n1 // 2 + 1) // 8) * 8


def _dft_consts(n1, n2):
    n = n1 * n2
    hf = n1 // 2
    nh = _dft_rows(n1)
    k = np.arange(nh, dtype=np.int64)
    kept = (k <= hf)[:, None]
    ang1 = 2.0 * np.pi * ((k[:, None] * np.arange(hf, dtype=np.int64)[None, :]) % n1) / n1
    c1, s1 = np.where(kept, np.cos(ang1), 0.0), np.where(kept, np.sin(ang1), 0.0)
    fwd_half = np.concatenate([c1, -s1], axis=0)
    wgt = np.where((k == 0) | (k == hf), 1.0, 2.0)[:, None]
    inv_half = np.concatenate([(wgt * c1).T, -(wgt * s1).T], axis=1)
    k2 = np.arange(n2, dtype=np.int64)
    ang2 = 2.0 * np.pi * ((k2[:, None] * k2[None, :]) % n2) / n2
    c2, s2 = np.cos(ang2), np.sin(ang2)
    m2 = np.block([[c2, s2], [-s2, c2]])
    angt = 2.0 * np.pi * ((k[:, None] * k2[None, :]) % n) / n
    twr = np.cos(angt).astype(np.float32)[:, :, None]
    twi = (-np.sin(angt)).astype(np.float32)[:, :, None]
    return dict(fwd_half=_split_hi_lo(fwd_half), inv_half=_split_hi_lo(inv_half),
                m2=_split_hi_lo(m2), m2inv=_split_hi_lo(m2.T), twr=jnp.asarray(twr), twi=jnp.asarray(twi))


def _dot3(mh, ml, x):
    xh = x.astype(BF16)
    xl = (x - xh.astype(F32)).astype(BF16)
    return (jnp.dot(mh, xh, preferred_element_type=F32) + jnp.dot(ml, xh, preferred_element_type=F32)
            + jnp.dot(mh, xl, preferred_element_type=F32))


def _dft_left_kernel(mh_ref, ml_ref, x_ref, o_ref):
    o_ref[...] = _dot3(mh_ref[...], ml_ref[...], x_ref[...])


def _filter_spectrum(fr, fi, c):
    return fr[:, :c] + fr[:, c:], fi[:, :c] - fi[:, c:]


def _dft_left_mul_kernel(mh_ref, ml_ref, x_ref, f_ref, o_ref):
    a = _dot3(mh_ref[...], ml_ref[...], x_ref[...])
    r = a.shape[0] // 2
    ar, ai = a[:r], a[r:]
    kr, ki = _filter_spectrum(f_ref[0:r, :], f_ref[r:, :], a.shape[1])
    o_ref[0:r, :] = ar * kr - ai * ki
    o_ref[r:, :] = ar * ki + ai * kr


def dft_left(mats, x, filt_f=None):
    mh, ml = mats
    r, k = mh.shape
    w = x.shape[1]
    tc = _pick(w, (2048, 1024, 512))
    in_specs = [pl.BlockSpec((r, k), lambda j: (0, 0)), pl.BlockSpec((r, k), lambda j: (0, 0)),
                pl.BlockSpec((k, tc), lambda j: (0, j))]
    args = [mh, ml, x]
    kern = _dft_left_kernel
    if filt_f is not None:
        assert w == tc and filt_f.shape == (r, 2 * w)
        in_specs.append(pl.BlockSpec((r, 2 * w), lambda j: (0, 0)))
        args.append(filt_f)
        kern = _dft_left_mul_kernel
    return pl.pallas_call(
        kern,
        grid=(w // tc,),
        in_specs=in_specs,
        out_specs=pl.BlockSpec((r, tc), lambda j: (0, j)),
        out_shape=jax.ShapeDtypeStruct((r, w), F32),
        compiler_params=_cparams("parallel"),
        name="dft_left",
    )(*args)


def _dft_inv_post_kernel(mh_ref, ml_ref, b_ref, x0_ref, z_ref, bias_ref, scale_ref, o_ref):
    conv = _dot3(mh_ref[...], ml_ref[...], b_ref[...]) * scale_ref[...]
    o_ref[...] = (x0_ref[...] * (conv + z_ref[...] * bias_ref[...])).astype(o_ref.dtype)


def dft_inv_post(mats, b, x0, z, bias_t, scale_t):
    mh, ml = mats
    r, k = mh.shape
    w = b.shape[1]
    tc = _pick(w, (2048, 1024, 512))
    return pl.pallas_call(
        _dft_inv_post_kernel,
        grid=(w // tc,),
        in_specs=[pl.BlockSpec((r, k), lambda j: (0, 0)), pl.BlockSpec((r, k), lambda j: (0, 0)),
                  pl.BlockSpec((k, tc), lambda j: (0, j)),
                  pl.BlockSpec((r, tc), lambda j: (0, j)), pl.BlockSpec((r, tc), lambda j: (0, j)),
                  pl.BlockSpec((1, tc), lambda j: (0, j)), pl.BlockSpec((1, tc), lambda j: (0, j))],
        out_specs=pl.BlockSpec((r, tc), lambda j: (0, j)),
        out_shape=jax.ShapeDtypeStruct((r, w), BF16),
        compiler_params=_cparams("parallel"),
        name="dft_inv_post",
    )(mh, ml, b, x0, z, bias_t, scale_t)


def _fft_mid_kernel(*refs, n2, with_filter):
    if with_filter:
        a_ref, twr_ref, twi_ref, mh_ref, ml_ref, ih_ref, il_ref, k_ref, o_ref = refs
    else:
        a_ref, twr_ref, twi_ref, mh_ref, ml_ref, o_ref = refs
    for s in range(a_ref.shape[1]):
        twr, twi = twr_ref[s], twi_ref[s]
        ar, ai = a_ref[0, s], a_ref[1, s]
        t = jnp.concatenate([twr * ar - twi * ai, twr * ai + twi * ar], axis=0)
        x = _dot3(mh_ref[...], ml_ref[...], t)
        xr, xi = x[:n2], x[n2:]
        if not with_filter:
            o_ref[0, s], o_ref[1, s] = _filter_spectrum(xr, xi, x.shape[1] // 2)
            continue
        kr, ki = k_ref[0, s], k_ref[1, s]
        y = jnp.concatenate([xr * kr - xi * ki, xr * ki + xi * kr], axis=0)
        b = _dot3(ih_ref[...], il_ref[...], y)
        br, bi = b[:n2], b[n2:]
        o_ref[0, s] = twr * br + twi * bi
        o_ref[1, s] = twr * bi - twi * br


def fft_mid(consts, a, spectrum=None):
    _, n1, n2, c = a.shape
    c_out = c if spectrum is not None else c // 2
    kb = _pick(n1, (4, 2, 1))
    slab = pl.BlockSpec((2, kb, n2, c), lambda k: (0, k, 0, 0))
    out_slab = pl.BlockSpec((2, kb, n2, c_out), lambda k: (0, k, 0, 0))
    tw = pl.BlockSpec((kb, n2, 1), lambda k: (k, 0, 0))
    mat = pl.BlockSpec((2 * n2, 2 * n2), lambda k: (0, 0))
    in_specs = [slab, tw, tw, mat, mat]
    args = [a, consts["twr"], consts["twi"], *consts["m2"]]
    if spectrum is not None:
        in_specs += [mat, mat, slab]
        args += [*consts["m2inv"], spectrum]
    return pl.pallas_call(
        functools.partial(_fft_mid_kernel, n2=n2, with_filter=spectrum is not None),
        grid=(n1 // kb,),
        in_specs=in_specs,
        out_specs=out_slab,
        out_shape=jax.ShapeDtypeStruct((2, n1, n2, c_out), F32),
        compiler_params=_cparams("parallel"),
        name="fft_mid",
    )(*args)


def hyena_long_conv(x0, z, h, abs_sum, bias, length):
    c = MIX_W
    n = 2 * length
    n2 = 128 if length >= 1024 else 1
    n1 = n // n2
    consts = _dft_consts(n1, n2)
    nh = _dft_rows(n1)
    inv_scale = 1.0 / ((abs_sum[:, :c] + abs_sum[:, c:]) * n)
    filt_f = dft_left(consts["fwd_half"], h.reshape(n1 // 2, n2 * 2 * c))
    xin = z.reshape(n1 // 2, n2 * c)
    if n2 > 1:
        spec = fft_mid(consts, filt_f.reshape(2, nh, n2, 2 * c))
        a = dft_left(consts["fwd_half"], xin)
        b = fft_mid(consts, a.reshape(2, nh, n2, c), spec).reshape(2 * nh, n2 * c)
    else:
        b = dft_left(consts["fwd_half"], xin, filt_f)
    y = dft_inv_post(consts["inv_half"], b, x0.reshape(n1 // 2, n2 * c), xin,
                     jnp.tile(bias.reshape(1, c), (1, n2)), jnp.tile(inv_scale, (1, n2)))
    return y.reshape(length, c)


def _merge_kernel(*refs):
    ys, gs, ws = refs[0:N_BRANCH], refs[N_BRANCH:2 * N_BRANCH], refs[2 * N_BRANCH:3 * N_BRANCH]
    o_ref, wb_ref = refs[3 * N_BRANCH:]

    @pl.when(pl.program_id(1) == 0)
    def _():
        for i in range(N_BRANCH):
            wb_ref[i] = ws[i][...].astype(BF16)

    acc = None
    for i in range(N_BRANCH):
        t = jnp.dot(ys[i][...], wb_ref[i], preferred_element_type=F32) * _sigmoid(gs[i][...].astype(F32))
        acc = t if acc is None else acc + t
    o_ref[...] = acc.astype(o_ref.dtype)


def merge_branches(ys, gates, gate_col0, branch_w, layer):
    m = ys[0].shape[0]
    d = branch_w.shape[3]
    tm = _pick(m, (640, 512, 256, 128))
    tn = _pick(d, (1024, 512, 256, 128))
    nj = d // tn
    assert gate_col0 % tn == 0
    g0 = gate_col0 // tn
    y_specs = [pl.BlockSpec((tm, MIX_W), lambda j, i: (i, 0))] * N_BRANCH
    g_specs = [pl.BlockSpec((tm, tn), functools.partial(lambda j, i, b: (i, g0 + b * nj + j), b=b)) for b in range(N_BRANCH)]
    w_specs = [pl.BlockSpec((None, None, MIX_W, tn), functools.partial(lambda j, i, b: (layer, b, 0, j), b=b),
                            pipeline_mode=pl.Buffered(1)) for b in range(N_BRANCH)]
    return pl.pallas_call(
        _merge_kernel,
        grid=(nj, m // tm),
        in_specs=y_specs + g_specs + w_specs,
        out_specs=pl.BlockSpec((tm, tn), lambda j, i: (i, j)),
        out_shape=jax.ShapeDtypeStruct((m, d), BF16),
        scratch_shapes=[pltpu.VMEM((N_BRANCH, MIX_W, tn), BF16)],
        compiler_params=_cparams("parallel", "arbitrary"),
        name="merge_branches",
    )(*ys, *([gates] * N_BRANCH), *([branch_w] * N_BRANCH))


FFN_HALO = 16


def _sigmoid(x):
    return 0.5 + 0.5 * jnp.tanh(0.5 * x)


def _ffn_up_kernel(a_ref, wa32_ref, wg32_ref, cwa_ref, cwg_ref, cba_ref, cbg_ref, o_ref, wa_ref, wg_ref,
                   *, tm, chunk, fixes):
    hl = FFN_HALO

    @pl.when(pl.program_id(1) == 0)
    def _():
        wa_ref[...] = wa32_ref[...].astype(BF16)
        wg_ref[...] = wg32_ref[...].astype(BF16)

    def conv(a, rows, w_ref, cw_ref, cb_ref, no_prev=None, no_next=None):
        u = jnp.dot(a, w_ref[...], preferred_element_type=F32)
        up, dn = u[hl - 1:hl - 1 + rows], u[hl + 1:hl + 1 + rows]
        if no_prev is not None:
            up = jnp.where(no_prev, 0.0, up)
            dn = jnp.where(no_next, 0.0, dn)
        return up * cw_ref[0:1, :] + u[hl:hl + rows] * cw_ref[1:2, :] + dn * cw_ref[2:3, :] + cb_ref[...]

    def silu_gate(a, rows, *masks):
        g = conv(a, rows, wg_ref, cwg_ref, cbg_ref, *masks)
        return g * _sigmoid(g)

    def gated(a, rows, *masks):
        return (silu_gate(a, rows, *masks) * conv(a, rows, wa_ref, cwa_ref, cba_ref, *masks)).astype(o_ref.dtype)

    pieces = [a_ref[c * chunk:(c + 1) * chunk + 2 * hl, :] for c in range(tm // chunk)]
    gates = [silu_gate(a, chunk) for a in pieces]
    for c, a in enumerate(pieces):
        o_ref[c * chunk:(c + 1) * chunk, :] = (gates[c] * conv(a, chunk, wa_ref, cwa_ref, cba_ref)).astype(o_ref.dtype)

    for blk, g0, first_row in fixes:
        @pl.when(pl.program_id(1) == blk)
        def _(g0=g0, first_row=first_row):
            row = blk * tm + g0 + lax.broadcasted_iota(jnp.int32, (hl, 1), 0)
            o_ref[g0:g0 + hl, :] = gated(a_ref[g0:g0 + 3 * hl, :], hl, row == first_row, row == first_row - 1)


FFN_TM = 1280


def ffn_up(a_ext, m, w_up, layer, conv_w, conv_b, n_latent):
    d = a_ext.shape[1]
    f = w_up.shape[2] // 2
    tm = FFN_TM
    tn = _pick(f, (512, 256, 128))
    chunk = _pick(tm, (640, 256, 128))
    nb, nj = m // tm, f // tn
    te = tm + 2 * FFN_HALO
    assert n_latent % FFN_HALO == 0 and 0 < n_latent < m and a_ext.shape[0] == nb * te
    fixes = []
    for r in (n_latent - 1, n_latent):
        g = r // FFN_HALO * FFN_HALO
        fixes.append((g // tm, g % tm, n_latent))
    wspec = lambda off: pl.BlockSpec((None, d, tn), lambda j, i: (layer, 0, off + j), pipeline_mode=pl.Buffered(1))
    return pl.pallas_call(
        functools.partial(_ffn_up_kernel, tm=tm, chunk=chunk, fixes=tuple(fixes)),
        grid=(nj, nb),
        in_specs=[pl.BlockSpec((te, d), lambda j, i: (i, 0)),
                  wspec(0), wspec(nj),
                  pl.BlockSpec((3, tn), lambda j, i: (0, j)),
                  pl.BlockSpec((3, tn), lambda j, i: (0, nj + j)),
                  pl.BlockSpec((1, tn), lambda j, i: (0, j)),
                  pl.BlockSpec((1, tn), lambda j, i: (0, nj + j))],
        out_specs=pl.BlockSpec((tm, tn), lambda j, i: (i, j)),
        out_shape=jax.ShapeDtypeStruct((m, f), BF16),
        scratch_shapes=[pltpu.VMEM((d, tn), BF16), pltpu.VMEM((d, tn), BF16)],
        compiler_params=_cparams("parallel", "arbitrary"),
        name="ffn_up",
    )(a_ext, w_up, w_up, conv_w, conv_w, conv_b.reshape(1, -1), conv_b.reshape(1, -1))


def kernel(x, c, ctx, c_ctx, ada_w, ada_b, norm1_g, norm2_g, w_in, a_qn_g, a_kn_g, b_rpb,
           c_gate_w2, c_gate_b, c_norm_g, d_conv_w, d_conv_b, d_ffn_w1, d_ffn_b1, d_ffn_w2,
           d_ffn_b2, d_ffn_w3, d_ffn_b3, d_ffn_w4, d_sin_freq, d_bias, branch_w, w_out,
           ffn_up_w, ffn_conv_w, ffn_conv_b, ffn_down, final_norm_g):
    depth = ada_w.shape[0]
    n_lat, d = x.shape[1], x.shape[2]
    n_ctx = ctx.shape[1]
    h = jnp.concatenate([x[0], ctx[0]], axis=0)
    cvec = jnp.concatenate([c[0:1], c_ctx[None, :], jnp.zeros((14, d), F32)], axis=0)
    cvec = (cvec * jax.nn.sigmoid(cvec)).astype(BF16)
    ctab, stab = rope_tables(n_lat, n_ctx)
    zeros_c = jnp.zeros((n_ctx, MIX_W), BF16)
    wt_in = jnp.swapaxes(w_in, 1, 2)

    for i in range(depth):
        ctx_out = i < depth - 1
        mod = matmul(cvec, ada_w, i, 0, ada_w.shape[2], F32, bias=ada_b[i])[0:2]
        sh1, sc1, g1, sh2, sc2, g2 = jnp.split(mod, 6, axis=-1)
        n1 = mod_norm(h, norm1_g[i], sh1, sc1, n_lat, BF16)
        ca_end = IN_CA + 2 * C_GATE_RANK
        w_ca = jnp.pad(wt_in[i, IN_CA:ca_end, :].T, ((0, 0), (0, 128 - 2 * C_GATE_RANK)))[None]
        p1 = matmul_wt(n1, wt_in, i, 0, PA_WIDTH, F32)
        pb = matmul_wt(n1, wt_in, i, ca_end, PB_WIDTH, F32)
        gates = matmul_wt(n1, wt_in, i, ca_end + PB_WIDTH, N_BRANCH * d, BF16)
        ca = matmul(n1, w_ca, 0, 0, 128, F32)

        q, k, v = a_prep(p1, ctab, stab, a_qn_g[i], a_kn_g[i])
        ya_l = flash_attention(q, k, v, A_KV_HEADS, n_rows=n_lat)
        yb_l = neighbourhood_attention(p1, na_bias_table(b_rpb[i]), n_lat, n_ctx)
        yc = gla_mixer(p1, pb, ca, c_gate_w2[i], c_gate_b[i], c_norm_g[i], n_lat, n_ctx)
        x0, z = hyena_pre(pb, d_conv_w[i], d_conv_b[i], n_lat)
        filt = (d_ffn_w1[i], d_ffn_b1[i], d_ffn_w2[i], d_ffn_b2[i], d_ffn_w3[i], d_ffn_b3[i], d_ffn_w4[i],
                d_sin_freq[i])
        hl, sl = hyena_filter(n_lat, *filt)
        yd_l = hyena_long_conv(x0[:n_lat], z[:n_lat], hl, sl, d_bias[i], n_lat)
        if ctx_out:
            ya_c = flash_attention(q[n_lat:], k[n_lat:], v[n_lat:], A_KV_HEADS)
            pc = lax.optimization_barrier(p1[n_lat:])
            yb_c = flash_attention((pc[:, P_BQ:P_BK] * QK_SCALE).astype(BF16), pc[:, P_BK:P_BV].astype(BF16),
                                   with_ones(pc[:, P_BV:P_CQ].astype(BF16), B_HEADS), B_HEADS)
            hc, sc = hyena_filter(n_ctx, *filt)
            yd_c = hyena_long_conv(x0[n_lat:], z[n_lat:], hc, sc, d_bias[i], n_ctx)
        else:
            ya_c = yb_c = yd_c = zeros_c
        ys = [jnp.concatenate([ya_l, ya_c], axis=0), jnp.concatenate([yb_l, yb_c], axis=0), yc,
              jnp.concatenate([yd_l, yd_c], axis=0)]
        merged = merge_branches(ys, gates, 0, branch_w, i)
        h = matmul_residual(merged, w_out, i, h, g1, n_lat)
        n2_ext = mod_norm_halo(h, norm2_g[i], sh2, sc2, n_lat, FFN_TM, FFN_HALO)
        act = ffn_up(n2_ext, h.shape[0], ffn_up_w, i, ffn_conv_w[i], ffn_conv_b[i], n_lat)
        h = matmul_residual(act, ffn_down, i, h, g2, n_lat)

    zero2 = jnp.zeros((2, d), F32)
    out = mod_norm(h, final_norm_g, zero2, zero2, n_lat, F32, n_rows=n_lat)
    return out[None]
```

```python
import functools
import math

import numpy as np
import jax
import jax.numpy as jnp
from jax import lax
from jax.experimental import pallas as pl
from jax.experimental.pallas import tpu as pltpu

F32 = jnp.float32
BF16 = jnp.bfloat16

GRID_W = 64
HEAD_DIM = 128
A_HEADS = 4
A_KV_HEADS = 2
ROPE_THETA = 10000.0
B_HEADS = 4
NA_ROWS = 8
NA_COLS = 16
C_HEADS = 4
C_DK = 64
C_DV = 128
C_GATE_RANK = 16
C_GATE_TAU = 16.0
C_CHUNK = 64
MIX_W = 512
N_BRANCH = 4
HY_EMB = 33
HY_DECAY_MIN = math.log(1e-2) / 1.5
HY_DECAY_MAX = math.log(1e-2) / 0.3
NORM_EPS = 1e-6
NEG_BIG = -1e30

P_AQ, P_AK, P_AV = 0, 512, 768
P_BQ, P_BK, P_BV = 1024, 1536, 2048
P_CQ, P_CK, P_CV = 2560, 2816, 3072
PA_WIDTH = 3584
PB_CG, PB_DU = 0, 512
PB_WIDTH = 2048
IN_CA = PA_WIDTH

VMEM_LIMIT = 56 * 1024 * 1024


def _cparams(*sem):
    return pltpu.CompilerParams(dimension_semantics=sem, vmem_limit_bytes=VMEM_LIMIT)


def _pick(n, candidates):
    for c in candidates:
        if n % c == 0:
            return c
    return n


def _modnorm_rows(x, row0, g_ref, sh_ref, sc_ref, n_latent, m):
    lat = row0 < n_latent
    inside = jnp.logical_and(row0 >= 0, row0 < m)
    gain = jnp.where(inside, g_ref[...] * (1.0 + jnp.where(lat, sc_ref[0], sc_ref[1])), 0.0)
    shift = jnp.where(inside, jnp.where(lat, sh_ref[0], sh_ref[1]), 0.0)
    return x * lax.rsqrt(jnp.mean(x * x, axis=-1, keepdims=True) + NORM_EPS) * gain + shift


def _modnorm_kernel(h_ref, g_ref, sh_ref, sc_ref, o_ref, *, tb, step, n_latent, m):
    for c in range(tb // step):
        rows = slice(c * step, (c + 1) * step)
        y = _modnorm_rows(h_ref[rows, :], pl.program_id(0) * tb + c * step, g_ref, sh_ref, sc_ref, n_latent, m)
        o_ref[rows, :] = y.astype(o_ref.dtype)


def mod_norm(h, g, shift2, scale2, n_latent, out_dtype, n_rows=None):
    d = h.shape[1]
    m = n_rows or h.shape[0]
    tb = _pick(m, (1280, 1024, 512, 256, 128, 64, 8))
    step = _pick(math.gcd(tb, n_latent), (256, 128, 64, 8))
    return pl.pallas_call(
        functools.partial(_modnorm_kernel, tb=tb, step=step, n_latent=n_latent, m=h.shape[0]),
        grid=(m // tb,),
        in_specs=[pl.BlockSpec((tb, d), lambda i: (i, 0)),
                  pl.BlockSpec((1, d), lambda i: (0, 0)),
                  pl.BlockSpec((2, 1, d), lambda i: (0, 0, 0)),
                  pl.BlockSpec((2, 1, d), lambda i: (0, 0, 0))],
        out_specs=pl.BlockSpec((tb, d), lambda i: (i, 0)),
        out_shape=jax.ShapeDtypeStruct((m, d), out_dtype),
        compiler_params=_cparams("parallel"),
        name="mod_norm",
    )(h, g.reshape(1, d), shift2.reshape(2, 1, d), scale2.reshape(2, 1, d))


def _modnorm_halo_kernel(hp_ref, h_ref, hn_ref, g_ref, sh_ref, sc_ref, o_ref, *, tm, halo, n_latent, m):
    i = pl.program_id(0)

    def norm(x, row0):
        return _modnorm_rows(x, row0, g_ref, sh_ref, sc_ref, n_latent, m).astype(o_ref.dtype)

    o_ref[0:halo, :] = norm(hp_ref[...], i * tm - halo)
    step = _pick(math.gcd(tm, n_latent), (256, 128, 64, 16))
    for c in range(tm // step):
        o_ref[halo + c * step:halo + (c + 1) * step, :] = norm(h_ref[c * step:(c + 1) * step, :], i * tm + c * step)
    o_ref[halo + tm:, :] = norm(hn_ref[...], (i + 1) * tm)


def mod_norm_halo(h, g, shift2, scale2, n_latent, tm, halo):
    m, d = h.shape
    nb = m // tm
    hb = tm // halo
    return pl.pallas_call(
        functools.partial(_modnorm_halo_kernel, tm=tm, halo=halo, n_latent=n_latent, m=m),
        grid=(nb,),
        in_specs=[pl.BlockSpec((halo, d), lambda i: (jnp.maximum(i * hb - 1, 0), 0)),
                  pl.BlockSpec((tm, d), lambda i: (i, 0)),
                  pl.BlockSpec((halo, d), lambda i: (jnp.minimum((i + 1) * hb, m // halo - 1), 0)),
                  pl.BlockSpec((1, d), lambda i: (0, 0)),
                  pl.BlockSpec((2, 1, d), lambda i: (0, 0, 0)),
                  pl.BlockSpec((2, 1, d), lambda i: (0, 0, 0))],
        out_specs=pl.BlockSpec((tm + 2 * halo, d), lambda i: (i, 0)),
        out_shape=jax.ShapeDtypeStruct((nb * (tm + 2 * halo), d), BF16),
        compiler_params=_cparams("parallel"),
        name="mod_norm_halo",
    )(h, h, h, g.reshape(1, d), shift2.reshape(2, 1, d), scale2.reshape(2, 1, d))


VMEM_TILE_BUDGET = 46 * 1024 * 1024


def _dense_tiles(m, k, n, col0, extra_tiles, out_bytes):
    for tms in ((1280, 1024, 640), (512, 320, 256, 128, 16)):
        for tn in (1792, 1024, 512, 256, 128):
            if n % tn or col0 % tn:
                continue
            for tm in tms:
                if m % tm:
                    continue
                blocks = 2 * tm * k * 2 + k * tn * (4 + 2) + 2 * tm * tn * (out_bytes + 4 * extra_tiles)
                if blocks <= VMEM_TILE_BUDGET:
                    return tm, tn
    raise ValueError("no dense tiling fits")


def _cast_at_first_row_block(w_ref, wb_ref):
    @pl.when(pl.program_id(1) == 0)
    def _():
        wb_ref[...] = w_ref[...].astype(BF16)


def _mm_kernel(a_ref, w_ref, o_ref, wb_ref):
    _cast_at_first_row_block(w_ref, wb_ref)
    o_ref[...] = jnp.dot(a_ref[...], wb_ref[...], preferred_element_type=F32).astype(o_ref.dtype)


def _mm_bias_kernel(a_ref, w_ref, b_ref, o_ref, wb_ref):
    _cast_at_first_row_block(w_ref, wb_ref)
    o_ref[...] = (jnp.dot(a_ref[...], wb_ref[...], preferred_element_type=F32) + b_ref[...]).astype(o_ref.dtype)


def matmul(a, w3, layer, col0, n, out_dtype, bias=None):
    m, k = a.shape
    tm, tn = _dense_tiles(m, k, n, col0, extra_tiles=0, out_bytes=jnp.dtype(out_dtype).itemsize)
    assert col0 % tn == 0 and w3.shape[1] == k
    c0 = col0 // tn
    in_specs = [pl.BlockSpec((tm, k), lambda j, i: (i, 0)),
                pl.BlockSpec((None, k, tn), lambda j, i: (layer, 0, c0 + j),
                             pipeline_mode=pl.Buffered(1 if m // tm > 1 else 2))]
    args = [a, w3]
    kern = _mm_kernel
    if bias is not None:
        in_specs.append(pl.BlockSpec((1, tn), lambda j, i: (0, j)))
        args.append(bias.reshape(1, n))
        kern = _mm_bias_kernel
    return pl.pallas_call(
        kern,
        grid=(n // tn, m // tm),
        in_specs=in_specs,
        out_specs=pl.BlockSpec((tm, tn), lambda j, i: (i, j)),
        out_shape=jax.ShapeDtypeStruct((m, n), out_dtype),
        scratch_shapes=[pltpu.VMEM((k, tn), BF16)],
        compiler_params=_cparams("parallel", "arbitrary"),
        name="matmul",
    )(*args)


WT_PIECE, WT_TAIL = 256, 128


def _mm_wt_kernel(*refs, n_pieces, shift, tn):
    a_ref, w_refs = refs[0], refs[1:1 + n_pieces]
    o_ref, wb_ref = refs[1 + n_pieces:]

    @pl.when(pl.program_id(1) == 0)
    def _():
        x = jnp.concatenate([r[...] for r in w_refs], axis=0)
        wb_ref[...] = x[shift:shift + tn].astype(BF16)

    o_ref[...] = lax.dot_general(a_ref[...], wb_ref[...], (((1,), (1,)), ((), ())),
                                 preferred_element_type=F32).astype(o_ref.dtype)


def matmul_wt(a, wt3, layer, col0, n, out_dtype):
    m, k = a.shape
    shift = col0 % WT_TAIL
    base = col0 - shift
    assert base % WT_PIECE == 0 and shift % 8 == 0 and col0 + n <= wt3.shape[1] and wt3.shape[2] == k
    tm, tn = _dense_tiles(m, k, n, 0, extra_tiles=0, out_bytes=jnp.dtype(out_dtype).itemsize)
    assert tn % WT_PIECE == 0
    per = tn // WT_PIECE
    single = dict(pipeline_mode=pl.Buffered(1))
    w_specs = [pl.BlockSpec((None, WT_PIECE, k), functools.partial(
        lambda j, i, p: (layer, base // WT_PIECE + j * per + p, 0), p=p), **single) for p in range(per)]
    if shift:
        w_specs.append(pl.BlockSpec((None, WT_TAIL, k),
                                    lambda j, i: (layer, base // WT_TAIL + (j + 1) * (tn // WT_TAIL), 0), **single))
    return pl.pallas_call(
        functools.partial(_mm_wt_kernel, n_pieces=len(w_specs), shift=shift, tn=tn),
        grid=(n // tn, m // tm),
        in_specs=[pl.BlockSpec((tm, k), lambda j, i: (i, 0))] + w_specs,
        out_specs=pl.BlockSpec((tm, tn), lambda j, i: (i, j)),
        out_shape=jax.ShapeDtypeStruct((m, n), out_dtype),
        scratch_shapes=[pltpu.VMEM((tn, k), BF16)],
        compiler_params=_cparams("parallel", "arbitrary"),
        name="matmul_wt",
    )(a, *([wt3] * len(w_specs)))


def _mm_res_kernel(a_ref, w_ref, h_ref, g_ref, o_ref, wb_ref, *, n_latent, tm):
    _cast_at_first_row_block(w_ref, wb_ref)
    row = pl.program_id(1) * tm + lax.broadcasted_iota(jnp.int32, (tm, 1), 0)
    gate = jnp.where(row < n_latent, g_ref[0], g_ref[1])
    o_ref[...] = h_ref[...] + gate * jnp.dot(a_ref[...], wb_ref[...], preferred_element_type=F32)


def matmul_residual(a, w3, layer, h, gate2, n_latent):
    m, k = a.shape
    n = w3.shape[2]
    tm, tn = _dense_tiles(m, k, n, 0, extra_tiles=1, out_bytes=4)
    return pl.pallas_call(
        functools.partial(_mm_res_kernel, n_latent=n_latent, tm=tm),
        grid=(n // tn, m // tm),
        in_specs=[pl.BlockSpec((tm, k), lambda j, i: (i, 0)),
                  pl.BlockSpec((None, k, tn), lambda j, i: (layer, 0, j), pipeline_mode=pl.Buffered(1)),
                  pl.BlockSpec((tm, tn), lambda j, i: (i, j)),
                  pl.BlockSpec((2, 1, tn), lambda j, i: (0, 0, j))],
        out_specs=pl.BlockSpec((tm, tn), lambda j, i: (i, j)),
        out_shape=jax.ShapeDtypeStruct((m, n), F32),
        scratch_shapes=[pltpu.VMEM((k, tn), BF16)],
        compiler_params=_cparams("parallel", "arbitrary"),
        name="matmul_residual",
    )(a, w3, h, gate2.reshape(2, 1, n))


def _swap_halves(y):
    lane = lax.broadcasted_iota(jnp.int32, y.shape, 1)
    return jnp.where((lane % 64) < 32, pltpu.roll(y, 96, 1), pltpu.roll(y, 32, 1))


def _aprep_kernel(p_ref, cos_ref, sin_ref, qg_ref, kg_ref, q_ref, k_ref, v_ref):
    c = cos_ref[...]
    s = sin_ref[...]

    def norm_rope(x, g):
        y = x * lax.rsqrt(jnp.mean(x * x, axis=-1, keepdims=True) + NORM_EPS) * g
        return y * c + _swap_halves(y) * s

    for hd in range(A_HEADS):
        x = p_ref[:, hd * HEAD_DIM:(hd + 1) * HEAD_DIM]
        q_ref[:, hd * HEAD_DIM:(hd + 1) * HEAD_DIM] = (
            norm_rope(x, qg_ref[...]) * QK_SCALE).astype(q_ref.dtype)
    for hd in range(A_KV_HEADS):
        x = p_ref[:, P_AK + hd * HEAD_DIM:P_AK + (hd + 1) * HEAD_DIM]
        k_ref[:, hd * HEAD_DIM:(hd + 1) * HEAD_DIM] = norm_rope(x, kg_ref[...]).astype(k_ref.dtype)
        v = p_ref[:, P_AV + hd * HEAD_DIM:P_AV + (hd + 1) * HEAD_DIM]
        v_ref[:, 2 * hd * HEAD_DIM:(2 * hd + 1) * HEAD_DIM] = v.astype(v_ref.dtype)
        v_ref[:, (2 * hd + 1) * HEAD_DIM:(2 * hd + 2) * HEAD_DIM] = jnp.ones_like(v).astype(v_ref.dtype)


def rope_tables(n_latent, n_ctx):
    pos = np.arange(n_latent)
    axes = np.stack([pos // GRID_W, pos % GRID_W], axis=-1).astype(np.float32)
    quarter = HEAD_DIM // 4
    inv_freq = (np.float32(ROPE_THETA) ** (-np.arange(quarter, dtype=np.float32) / quarter)).astype(np.float32)
    ang = (axes[:, :, None] * inv_freq).astype(np.float32)
    cos, sin = np.cos(ang), np.sin(ang)
    ctab = np.concatenate([cos[:, 0], cos[:, 0], cos[:, 1], cos[:, 1]], axis=-1)
    stab = np.concatenate([-sin[:, 0], sin[:, 0], -sin[:, 1], sin[:, 1]], axis=-1)
    ctab = np.concatenate([ctab, np.ones((n_ctx, HEAD_DIM), np.float32)], axis=0)
    stab = np.concatenate([stab, np.zeros((n_ctx, HEAD_DIM), np.float32)], axis=0)
    return jnp.asarray(ctab, F32), jnp.asarray(stab, F32)


def a_prep(p1, ctab, stab, qn_g, kn_g):
    m = p1.shape[0]
    tb = _pick(m, (640, 256, 128, 64, 8))
    wa = 1024
    return pl.pallas_call(
        _aprep_kernel,
        grid=(m // tb,),
        in_specs=[pl.BlockSpec((tb, wa), lambda i: (i, 0)),
                  pl.BlockSpec((tb, HEAD_DIM), lambda i: (i, 0)),
                  pl.BlockSpec((tb, HEAD_DIM), lambda i: (i, 0)),
                  pl.BlockSpec((1, HEAD_DIM), lambda i: (0, 0)),
                  pl.BlockSpec((1, HEAD_DIM), lambda i: (0, 0))],
        out_specs=[pl.BlockSpec((tb, 512), lambda i: (i, 0)),
                   pl.BlockSpec((tb, 256), lambda i: (i, 0)),
                   pl.BlockSpec((tb, 512), lambda i: (i, 0))],
        out_shape=[jax.ShapeDtypeStruct((m, 512), BF16),
                   jax.ShapeDtypeStruct((m, 256), BF16),
                   jax.ShapeDtypeStruct((m, 512), BF16)],
        compiler_params=_cparams("parallel"),
        name="a_prep",
    )(p1, ctab, stab, qn_g.reshape(1, HEAD_DIM), kn_g.reshape(1, HEAD_DIM))


LOG2E = 1.4426950408889634
QK_SCALE = HEAD_DIM ** -0.5 * LOG2E


def _flash_kernel(q_ref, k_ref, v_ref, o_ref, m_ref, acc_ref, sa_ref, sb_ref, *, group, tk, nk):
    m_ref[...] = jnp.full_like(m_ref, NEG_BIG)
    acc_ref[...] = jnp.zeros_like(acc_ref)

    def scores(j, s_ref):
        k = k_ref[pl.ds(pl.multiple_of(j * tk, tk), tk), :]
        for g in range(group):
            q = q_ref[:, g * HEAD_DIM:(g + 1) * HEAD_DIM]
            s_ref[g] = lax.dot_general(q, k, (((1,), (1,)), ((), ())), preferred_element_type=F32)

    def update(j, s_ref):
        v = v_ref[pl.ds(pl.multiple_of(j * tk, tk), tk), :]
        for g in range(group):
            s = s_ref[g]
            m_prev = m_ref[g]
            m_new = jnp.maximum(m_prev, jnp.max(s, axis=-1, keepdims=True))
            p = jnp.exp2(s - m_new).astype(BF16)
            acc_ref[g] = jnp.exp2(m_prev - m_new) * acc_ref[g] + jnp.dot(p, v, preferred_element_type=F32)
            m_ref[g] = m_new

    scores(0, sa_ref)
    pairs = (nk - 1) // 2

    def body(jj, carry):
        j = 2 * jj
        scores(j + 1, sb_ref)
        update(j, sa_ref)
        scores(j + 2, sa_ref)
        update(j + 1, sb_ref)
        return carry

    lax.fori_loop(0, pairs, body, 0)
    if (nk - 1) % 2 == 1:
        scores(nk - 1, sb_ref)
        update(nk - 2, sa_ref)
        update(nk - 1, sb_ref)
    else:
        update(nk - 1, sa_ref)
    for g in range(group):
        acc = acc_ref[g]
        o_ref[:, g * HEAD_DIM:(g + 1) * HEAD_DIM] = (acc[:, :HEAD_DIM] / acc[:, HEAD_DIM:]).astype(o_ref.dtype)


def with_ones(v, heads):
    n = v.shape[0]
    v3 = v.reshape(n, heads, HEAD_DIM)
    return jnp.concatenate([v3, jnp.ones_like(v3)], axis=-1).reshape(n, heads * 2 * HEAD_DIM)


def flash_attention(q, k, v1, kv_heads, tq=None, tk=None, n_rows=None):
    lq, hq = n_rows or q.shape[0], q.shape[1]
    lk = k.shape[0]
    group = hq // (kv_heads * HEAD_DIM)
    tq = tq or _pick(lq, (1024, 512, 256, 128))
    tk = tk or _pick(lk, (1280, 1024, 512, 256, 128))
    nk = lk // tk
    gw = group * HEAD_DIM
    return pl.pallas_call(
        functools.partial(_flash_kernel, group=group, tk=tk, nk=nk),
        grid=(kv_heads, lq // tq),
        in_specs=[pl.BlockSpec((tq, gw), lambda h, i: (i, h)),
                  pl.BlockSpec((lk, HEAD_DIM), lambda h, i: (0, h), pipeline_mode=pl.Buffered(1)),
                  pl.BlockSpec((lk, 2 * HEAD_DIM), lambda h, i: (0, h), pipeline_mode=pl.Buffered(1))],
        out_specs=pl.BlockSpec((tq, gw), lambda h, i: (i, h)),
        out_shape=jax.ShapeDtypeStruct((lq, hq), BF16),
        scratch_shapes=[pltpu.VMEM((group, tq, 1), F32),
                        pltpu.VMEM((group, tq, 2 * HEAD_DIM), F32),
                        pltpu.VMEM((group, tq, tk), F32),
                        pltpu.VMEM((group, tq, tk), F32)],
        compiler_params=_cparams("parallel", "parallel"),
        name="flash_attention",
    )(q, k, v1)


NA_GROUP = 4


def na_bias_table(rpb):
    col = np.arange(GRID_W)
    c_start = np.clip(col - NA_COLS // 2, 0, GRID_W - NA_COLS)
    kc = np.arange(GRID_W)
    col_ok = (kc[None, :] >= c_start[:, None]) & (kc[None, :] < c_start[:, None] + NA_COLS)
    dc = np.clip(kc[None, :] - col[:, None] + (NA_COLS - 1), 0, 2 * NA_COLS - 2)
    i = np.arange(NA_GROUP)
    j = np.arange(3 * NA_GROUP)
    dr = np.clip(j[None, :] - i[:, None] + NA_ROWS // 2 - 1, 0, 2 * NA_ROWS - 2)
    first = np.broadcast_to((j >= NA_GROUP)[None, :], dr.shape)
    inner = (j[None, :] >= i[:, None]) & (j[None, :] < i[:, None] + NA_ROWS)
    last = np.broadcast_to((j < NA_ROWS)[None, :], dr.shape)
    pick_r = jnp.asarray(np.eye(2 * NA_ROWS - 1, dtype=np.float32)[dr])
    pick_c = jnp.asarray(np.eye(2 * NA_COLS - 1, dtype=np.float32)[dc])
    bias = jnp.einsum("ijr,hrs,cks->hijck", pick_r, rpb.astype(F32), pick_c, precision=lax.Precision.HIGHEST)
    tabs = []
    for row_ok in (first, inner, last):
        ok = row_ok[:, :, None, None] & col_ok[None, None, :, :]
        t = jnp.where(jnp.asarray(ok)[None], bias * LOG2E, NEG_BIG)
        tabs.append(t.transpose(0, 1, 3, 2, 4).reshape(rpb.shape[0], NA_GROUP * GRID_W, 3 * NA_GROUP * GRID_W))
    return jnp.stack(tabs)


def _na_kernel(q_ref, k0_ref, k1_ref, k2_ref, v0_ref, v1_ref, v2_ref, kc_ref, vc_ref, b_ref, o_ref):
    k_u = jnp.concatenate([r[...].astype(BF16) for r in (k0_ref, k1_ref, k2_ref)], axis=0)
    v_u = jnp.concatenate([r[...].astype(BF16) for r in (v0_ref, v1_ref, v2_ref)], axis=0)
    k_c = kc_ref[...].astype(BF16)
    v_c = vc_ref[...].astype(BF16)
    nt = (((1,), (1,)), ((), ()))
    for hd in range(B_HEADS):
        sl = slice(hd * HEAD_DIM, (hd + 1) * HEAD_DIM)
        q = (q_ref[:, sl] * QK_SCALE).astype(BF16)
        s_loc = lax.dot_general(q, k_u[:, sl], nt, preferred_element_type=F32) + b_ref[hd]
        s_ctx = lax.dot_general(q, k_c[:, sl], nt, preferred_element_type=F32)
        mx = jnp.maximum(jnp.max(s_loc, axis=-1, keepdims=True), jnp.max(s_ctx, axis=-1, keepdims=True))
        p_loc = jnp.exp2(s_loc - mx)
        p_ctx = jnp.exp2(s_ctx - mx)
        den = jnp.sum(p_loc, axis=-1, keepdims=True) + jnp.sum(p_ctx, axis=-1, keepdims=True)
        o = (jnp.dot(p_loc.astype(BF16), v_u[:, sl], preferred_element_type=F32)
             + jnp.dot(p_ctx.astype(BF16), v_c[:, sl], preferred_element_type=F32))
        o_ref[:, sl] = (o / den).astype(o_ref.dtype)


def neighbourhood_attention(p1, bias_tab, n_latent, n_ctx):
    rows = n_latent // GRID_W
    ng = rows // NA_GROUP
    assert rows % NA_GROUP == 0 and rows >= 2 * NA_ROWS and NA_ROWS == 2 * NA_GROUP
    w = B_HEADS * HEAD_DIM
    t = NA_GROUP * GRID_W
    cq, ck, cv = P_BQ // w, P_BK // w, P_BV // w

    def kv_specs(col):
        return [pl.BlockSpec((t, w), functools.partial(lambda g, d, col: (jnp.clip(g + d, 0, ng - 1), col), d=d, col=col))
                for d in (-1, 0, 1)]

    cblk = n_latent // n_ctx
    return pl.pallas_call(
        _na_kernel,
        grid=(ng,),
        in_specs=[pl.BlockSpec((t, w), lambda g: (g, cq))] + kv_specs(ck) + kv_specs(cv) + [
            pl.BlockSpec((n_ctx, w), lambda g: (cblk, ck)),
            pl.BlockSpec((n_ctx, w), lambda g: (cblk, cv)),
            pl.BlockSpec((None,) + bias_tab.shape[1:],
                         lambda g: (jnp.where(g == 0, 0, jnp.where(g == ng - 1, 2, 1)), 0, 0, 0))],
        out_specs=pl.BlockSpec((t, w), lambda g: (g, 0)),
        out_shape=jax.ShapeDtypeStruct((n_latent, w), BF16),
        compiler_params=_cparams("parallel"),
        name="neighbourhood_attention",
    )(*([p1] * 9), bias_tab)


GLA_BLOCK = 4 * C_CHUNK


def _gla_block(q_ref, k_ref, v_ref, a_ref, w2_ref, gb_ref, st_ref, reverse, emit):
    c = C_CHUNK
    nch = GLA_BLOCK // c
    x = jnp.dot(a_ref[...].astype(BF16), w2_ref[...], preferred_element_type=F32) + gb_ref[...]
    la = (jnp.minimum(x, 0.0) - jnp.log(1.0 + jnp.exp(-jnp.abs(x)))) * (1.0 / C_GATE_TAU)
    pos = lax.broadcasted_iota(jnp.int32, la.shape, 0) % c
    b = la
    sh = 1
    while sh < c:
        if reverse:
            b = b + jnp.where(pos < c - sh, pltpu.roll(b, GLA_BLOCK - sh, 0), 0.0)
        else:
            b = b + jnp.where(pos >= sh, pltpu.roll(b, sh, 0), 0.0)
        sh *= 2
    qs = q_ref[...] * (C_DK ** -0.5)
    kk = k_ref[...]
    q_in = (qs * jnp.exp(b)).astype(BF16)
    ii = lax.broadcasted_iota(jnp.int32, (c, c), 0)
    jj = lax.broadcasted_iota(jnp.int32, (c, c), 1)
    mask = (jj >= ii) if reverse else (jj <= ii)
    nt = (((1,), (1,)), ((), ()))
    tn = (((0,), (0,)), ((), ()))
    states = [st_ref[hd] for hd in range(C_HEADS)]
    for ch in (range(nch - 1, -1, -1) if reverse else range(nch)):
        rows = slice(ch * c, (ch + 1) * c)
        bc = b[rows]
        b_mid = bc[c // 2:c // 2 + 1, :]
        b_last = bc[0:1, :] if reverse else bc[c - 1:c, :]
        q_t = (qs[rows] * jnp.exp(bc - b_mid)).astype(BF16)
        k_t = (kk[rows] * jnp.exp(b_mid - bc)).astype(BF16)
        k_d = (kk[rows] * jnp.exp(b_last - bc)).astype(BF16)
        e_last = jnp.exp(b_last)
        for hd in range(C_HEADS):
            ks = slice(hd * C_DK, (hd + 1) * C_DK)
            v = v_ref[rows, hd * C_DV:(hd + 1) * C_DV].astype(BF16)
            att = lax.dot_general(q_t[:, ks], k_t[:, ks], nt, preferred_element_type=F32)
            att = jnp.where(mask, att, 0.0).astype(BF16)
            o = (lax.dot_general(q_in[rows, ks], states[hd].astype(BF16), nt, preferred_element_type=F32)
                 + jnp.dot(att, v, preferred_element_type=F32))
            states[hd] = e_last[:, ks] * states[hd] + lax.dot_general(v, k_d[:, ks], tn, preferred_element_type=F32)
            emit(rows, hd, o)
    for hd in range(C_HEADS):
        st_ref[hd] = states[hd]


def _gla_scan_kernel(qf_ref, kf_ref, vf_ref, af_ref, qr_ref, kr_ref, vr_ref, ar_ref, w2_ref, gb_ref,
                     of_ref, or_ref, stf_ref, str_ref):
    @pl.when(pl.program_id(0) == 0)
    def _():
        stf_ref[...] = jnp.zeros_like(stf_ref)
        str_ref[...] = jnp.zeros_like(str_ref)

    def emit_to(o_ref):
        def emit(rows, hd, o):
            o_ref[rows, hd * C_DV:(hd + 1) * C_DV] = o
        return emit

    _gla_block(qf_ref, kf_ref, vf_ref, af_ref, w2_ref.at[0], gb_ref.at[0], stf_ref, False, emit_to(of_ref))
    _gla_block(qr_ref, kr_ref, vr_ref, ar_ref, w2_ref.at[1], gb_ref.at[1], str_ref, True, emit_to(or_ref))


def _gla_out_kernel(of_ref, or_ref, g_ref, ng_ref, y_ref):
    g = g_ref[...]
    silu = g * _sigmoid(g)
    for hd in range(C_HEADS):
        vs = slice(hd * C_DV, (hd + 1) * C_DV)
        x = of_ref[:, vs] + or_ref[:, vs]
        y = x * lax.rsqrt(jnp.mean(x * x, axis=-1, keepdims=True) + NORM_EPS) * ng_ref[...]
        y_ref[:, vs] = (y * silu[:, vs]).astype(y_ref.dtype)


def gla_mixer(p1, pb, ca, gate_w2, gate_b, norm_g, n_latent, n_ctx):
    m = p1.shape[0]
    c = GLA_BLOCK
    assert n_latent % c == 0 and n_ctx % c == 0
    nl, nc = n_latent // c, n_ctx // c
    kw = C_HEADS * C_DK
    vw = C_HEADS * C_DV

    def blk_f(s):
        return jnp.where(s < nc, nl + s, s - nc)

    def blk_r(s):
        return nl + nc - 1 - s

    w2 = jnp.zeros((2, 128, kw), F32)
    w2 = w2.at[0, 0:C_GATE_RANK].set(gate_w2[0]).at[1, C_GATE_RANK:2 * C_GATE_RANK].set(gate_w2[1]).astype(BF16)
    gb = gate_b.reshape(2, 1, kw).astype(F32)

    def common_specs(blk):
        return [pl.BlockSpec((c, kw), lambda s: (blk(s), P_CQ // kw)),
                pl.BlockSpec((c, kw), lambda s: (blk(s), P_CK // kw)),
                pl.BlockSpec((c, vw), lambda s: (blk(s), P_CV // vw)),
                pl.BlockSpec((c, 128), lambda s: (blk(s), 0))]

    state = pltpu.VMEM((C_HEADS, C_DV, C_DK), F32)
    o_f, o_r = pl.pallas_call(
        _gla_scan_kernel,
        grid=(nl + nc,),
        in_specs=common_specs(blk_f) + common_specs(blk_r) + [pl.BlockSpec((2, 128, kw), lambda s: (0, 0, 0)),
                                                              pl.BlockSpec((2, 1, kw), lambda s: (0, 0, 0))],
        out_specs=[pl.BlockSpec((c, vw), lambda s: (blk_f(s), 0)),
                   pl.BlockSpec((c, vw), lambda s: (blk_r(s), 0))],
        out_shape=[jax.ShapeDtypeStruct((m, vw), F32)] * 2,
        scratch_shapes=[state, state],
        compiler_params=_cparams("arbitrary"),
        name="gla_scan",
    )(p1, p1, p1, ca, p1, p1, p1, ca, w2, gb)
    tb = _pick(m, (1280, 1024, 512, 256))
    return pl.pallas_call(
        _gla_out_kernel,
        grid=(m // tb,),
        in_specs=[pl.BlockSpec((tb, vw), lambda i: (i, 0)),
                  pl.BlockSpec((tb, vw), lambda i: (i, 0)),
                  pl.BlockSpec((tb, vw), lambda i: (i, PB_CG // vw)),
                  pl.BlockSpec((1, C_DV), lambda i: (0, 0))],
        out_specs=pl.BlockSpec((tb, vw), lambda i: (i, 0)),
        out_shape=jax.ShapeDtypeStruct((m, vw), BF16),
        compiler_params=_cparams("parallel"),
        name="gla_out",
    )(o_f, o_r, pb, norm_g.reshape(1, C_DV))


def _conv3_rows(x, prev_row, next_row, w_ref, b_ref):
    t = x.shape[0]
    row = lax.broadcasted_iota(jnp.int32, x.shape, 0)
    up = jnp.where(row == 0, prev_row, pltpu.roll(x, 1, 0))
    dn = jnp.where(row == t - 1, next_row, pltpu.roll(x, t - 1, 0))
    return up * w_ref[0:1, :] + x * w_ref[1:2, :] + dn * w_ref[2:3, :] + b_ref[...]


def _hy_pre_kernel(*refs, tb, seq_starts, seq_ends):
    mains, prevs, nexts = refs[0:3], refs[3:6], refs[6:9]
    w_refs, b_refs = refs[9:12], refs[12:15]
    x0_ref, z_ref = refs[15:17]
    r0 = pl.program_id(0) * tb
    pvalid = jnp.where(functools.reduce(jnp.logical_or, [r0 == s for s in seq_starts]), 0.0, 1.0)
    nvalid = jnp.where(functools.reduce(jnp.logical_or, [r0 + tb == e for e in seq_ends]), 0.0, 1.0)
    res = []
    for t in range(3):
        res.append(_conv3_rows(mains[t][...], prevs[t][7:8, :] * pvalid, nexts[t][0:1, :] * nvalid,
                               w_refs[t], b_refs[t]))
    x0_ref[...] = res[0]
    z_ref[...] = res[2] * res[1]


def hyena_pre(p1, conv_w, conv_b, n_latent):
    m = p1.shape[0]
    w = MIX_W
    tb = _pick(math.gcd(n_latent, m), (256, 128, 64, 8))
    nb8 = m // 8
    c0 = PB_DU // w
    mains = [pl.BlockSpec((tb, w), functools.partial(lambda i, t: (i, c0 + t), t=t)) for t in range(3)]
    prevs = [pl.BlockSpec((8, w), functools.partial(lambda i, t: (jnp.maximum(i * (tb // 8) - 1, 0), c0 + t), t=t))
             for t in range(3)]
    nexts = [pl.BlockSpec((8, w), functools.partial(lambda i, t: (jnp.minimum((i + 1) * (tb // 8), nb8 - 1), c0 + t), t=t))
             for t in range(3)]
    wspecs = [pl.BlockSpec((3, w), functools.partial(lambda i, t: (0, t), t=t)) for t in range(3)]
    bspecs = [pl.BlockSpec((1, w), functools.partial(lambda i, t: (0, t), t=t)) for t in range(3)]
    return pl.pallas_call(
        functools.partial(_hy_pre_kernel, tb=tb, seq_starts=(0, n_latent), seq_ends=(n_latent, m)),
        grid=(m // tb,),
        in_specs=mains + prevs + nexts + wspecs + bspecs,
        out_specs=[pl.BlockSpec((tb, w), lambda i: (i, 0))] * 2,
        out_shape=[jax.ShapeDtypeStruct((m, w), F32)] * 2,
        compiler_params=_cparams("parallel"),
        name="hyena_pre",
    )(*([p1] * 9), *([conv_w] * 3), *([conv_b.reshape(1, -1)] * 3))


def _hy_filter_kernel(z_ref, w1_ref, b1_ref, w2_ref, b2_ref, w3_ref, b3_ref, w4_ref, fr_ref, dl_ref,
                      h_ref, s_ref):
    def dot3(x, w):
        xh, wh = x.astype(BF16), w.astype(BF16)
        xl, wl = (x - xh.astype(F32)).astype(BF16), (w - wh.astype(F32)).astype(BF16)
        return (jnp.dot(xh, wh, preferred_element_type=F32) + jnp.dot(xh, wl, preferred_element_type=F32)
                + jnp.dot(xl, wh, preferred_element_type=F32))

    z = z_ref[...]
    fr = fr_ref[...]
    hid = jnp.sin(fr * (dot3(z, w1_ref[...]) + b1_ref[...]))
    hid = jnp.sin(fr * (dot3(hid, w2_ref[...]) + b2_ref[...]))
    hid = jnp.sin(fr * (dot3(hid, w3_ref[...]) + b3_ref[...]))
    h = dot3(hid, w4_ref[...])
    h = h * jnp.exp(-z[:, 0:1] * dl_ref[...])
    row = pl.program_id(0) * h.shape[0] + lax.broadcasted_iota(jnp.int32, h.shape, 0)
    col = lax.broadcasted_iota(jnp.int32, h.shape, 1)
    h_ref[...] = jnp.where((row == 0) & (col >= MIX_W), 0.0, h)

    @pl.when(pl.program_id(0) == 0)
    def _():
        s_ref[...] = jnp.zeros_like(s_ref)

    s_ref[...] += jnp.sum(jnp.abs(h), axis=0, keepdims=True)


def hyena_filter(length, w1, b1, w2, b2, w3, b3, w4, freq):
    bands = (HY_EMB - 1) // 2
    t = np.linspace(0.0, 1.0, length, dtype=np.float32)[:, None]
    wv = (np.float32(2.0 * math.pi) * np.arange(length, dtype=np.float32)[:, None] / np.float32(length)).astype(np.float32)
    f = np.linspace(1e-4, bands - 1, bands, dtype=np.float32)
    zt = np.concatenate([t, np.cos(f * wv), -np.sin(f * wv)], axis=-1).astype(np.float32)
    z = np.zeros((length, 128), np.float32)
    z[:, :HY_EMB] = zt
    deltas = np.abs(np.linspace(HY_DECAY_MIN, HY_DECAY_MAX, MIX_W, dtype=np.float32))
    dl = np.concatenate([deltas, deltas])[None, :].astype(np.float32)

    def pad2(a, r, c):
        return jnp.zeros((r, c), F32).at[:a.shape[0], :a.shape[1]].set(a.astype(F32))

    hf = w1.shape[1]
    tb = _pick(length, (512, 256, 128, 8))
    full = lambda i: (0, 0)
    return pl.pallas_call(
        _hy_filter_kernel,
        grid=(length // tb,),
        in_specs=[pl.BlockSpec((tb, 128), lambda i: (i, 0)),
                  pl.BlockSpec((128, 128), full), pl.BlockSpec((1, 128), full),
                  pl.BlockSpec((128, 128), full), pl.BlockSpec((1, 128), full),
                  pl.BlockSpec((128, 128), full), pl.BlockSpec((1, 128), full),
                  pl.BlockSpec((128, 2 * MIX_W), full), pl.BlockSpec((1, 128), full),
                  pl.BlockSpec((1, 2 * MIX_W), full)],
        out_specs=[pl.BlockSpec((tb, 2 * MIX_W), lambda i: (i, 0)),
                   pl.BlockSpec((1, 2 * MIX_W), full)],
        out_shape=[jax.ShapeDtypeStruct((length, 2 * MIX_W), F32),
                   jax.ShapeDtypeStruct((1, 2 * MIX_W), F32)],
        compiler_params=_cparams("arbitrary"),
        name="hyena_filter",
    )(jnp.asarray(z), pad2(w1, 128, 128), pad2(b1[None], 1, 128), pad2(w2, 128, 128), pad2(b2[None], 1, 128),
      pad2(w3, 128, 128), pad2(b3[None], 1, 128), pad2(w4, 128, 2 * MIX_W), pad2(freq[None], 1, 128),
      jnp.asarray(dl))


def _split_hi_lo(m):
    m = np.asarray(m, np.float32)
    hi = jnp.asarray(m, F32).astype(BF16)
    lo = (jnp.asarray(m, F32) - hi.astype(F32)).astype(BF16)
    return hi, lo


def _dft_rows(n1):
    return -(-(n1 // 2 + 1) // 8) * 8


def _dft_consts(n1, n2):
    n = n1 * n2
    hf = n1 // 2
    nh = _dft_rows(n1)
    k = np.arange(nh, dtype=np.int64)
    kept = (k <= hf)[:, None]
    ang1 = 2.0 * np.pi * ((k[:, None] * np.arange(hf, dtype=np.int64)[None, :]) % n1) / n1
    c1, s1 = np.where(kept, np.cos(ang1), 0.0), np.where(kept, np.sin(ang1), 0.0)
    fwd_half = np.concatenate([c1, -s1], axis=0)
    wgt = np.where((k == 0) | (k == hf), 1.0, 2.0)[:, None]
    inv_half = np.concatenate([(wgt * c1).T, -(wgt * s1).T], axis=1)
    k2 = np.arange(n2, dtype=np.int64)
    ang2 = 2.0 * np.pi * ((k2[:, None] * k2[None, :]) % n2) / n2
    c2, s2 = np.cos(ang2), np.sin(ang2)
    m2 = np.block([[c2, s2], [-s2, c2]])
    angt = 2.0 * np.pi * ((k[:, None] * k2[None, :]) % n) / n
    twr = np.cos(angt).astype(np.float32)[:, :, None]
    twi = (-np.sin(angt)).astype(np.float32)[:, :, None]
    return dict(fwd_half=_split_hi_lo(fwd_half), inv_half=_split_hi_lo(inv_half),
                m2=_split_hi_lo(m2), m2inv=_split_hi_lo(m2.T), twr=jnp.asarray(twr), twi=jnp.asarray(twi))


def _dot3(mh, ml, x):
    xh = x.astype(BF16)
    xl = (x - xh.astype(F32)).astype(BF16)
    return (jnp.dot(mh, xh, preferred_element_type=F32) + jnp.dot(ml, xh, preferred_element_type=F32)
            + jnp.dot(mh, xl, preferred_element_type=F32))


def _dft_left_kernel(mh_ref, ml_ref, x_ref, o_ref):
    o_ref[...] = _dot3(mh_ref[...], ml_ref[...], x_ref[...])


def _filter_spectrum(fr, fi, c):
    return fr[:, :c] + fr[:, c:], fi[:, :c] - fi[:, c:]


def _dft_left_mul_kernel(mh_ref, ml_ref, x_ref, f_ref, o_ref):
    a = _dot3(mh_ref[...], ml_ref[...], x_ref[...])
    r = a.shape[0] // 2
    ar, ai = a[:r], a[r:]
    kr, ki = _filter_spectrum(f_ref[0:r, :], f_ref[r:, :], a.shape[1])
    o_ref[0:r, :] = ar * kr - ai * ki
    o_ref[r:, :] = ar * ki + ai * kr


def dft_left(mats, x, filt_f=None):
    mh, ml = mats
    r, k = mh.shape
    w = x.shape[1]
    tc = _pick(w, (2048, 1024, 512))
    in_specs = [pl.BlockSpec((r, k), lambda j: (0, 0)), pl.BlockSpec((r, k), lambda j: (0, 0)),
                pl.BlockSpec((k, tc), lambda j: (0, j))]
    args = [mh, ml, x]
    kern = _dft_left_kernel
    if filt_f is not None:
        assert w == tc and filt_f.shape == (r, 2 * w)
        in_specs.append(pl.BlockSpec((r, 2 * w), lambda j: (0, 0)))
        args.append(filt_f)
        kern = _dft_left_mul_kernel
    return pl.pallas_call(
        kern,
        grid=(w // tc,),
        in_specs=in_specs,
        out_specs=pl.BlockSpec((r, tc), lambda j: (0, j)),
        out_shape=jax.ShapeDtypeStruct((r, w), F32),
        compiler_params=_cparams("parallel"),
        name="dft_left",
    )(*args)


def _dft_inv_post_kernel(mh_ref, ml_ref, b_ref, x0_ref, z_ref, bias_ref, scale_ref, o_ref):
    conv = _dot3(mh_ref[...], ml_ref[...], b_ref[...]) * scale_ref[...]
    o_ref[...] = (x0_ref[...] * (conv + z_ref[...] * bias_ref[...])).astype(o_ref.dtype)


def dft_inv_post(mats, b, x0, z, bias_t, scale_t):
    mh, ml = mats
    r, k = mh.shape
    w = b.shape[1]
    tc = _pick(w, (2048, 1024, 512))
    return pl.pallas_call(
        _dft_inv_post_kernel,
        grid=(w // tc,),
        in_specs=[pl.BlockSpec((r, k), lambda j: (0, 0)), pl.BlockSpec((r, k), lambda j: (0, 0)),
                  pl.BlockSpec((k, tc), lambda j: (0, j)),
                  pl.BlockSpec((r, tc), lambda j: (0, j)), pl.BlockSpec((r, tc), lambda j: (0, j)),
                  pl.BlockSpec((1, tc), lambda j: (0, j)), pl.BlockSpec((1, tc), lambda j: (0, j))],
        out_specs=pl.BlockSpec((r, tc), lambda j: (0, j)),
        out_shape=jax.ShapeDtypeStruct((r, w), BF16),
        compiler_params=_cparams("parallel"),
        name="dft_inv_post",
    )(mh, ml, b, x0, z, bias_t, scale_t)


def _fft_mid_kernel(*refs, n2, with_filter):
    if with_filter:
        a_ref, twr_ref, twi_ref, mh_ref, ml_ref, ih_ref, il_ref, k_ref, o_ref = refs
    else:
        a_ref, twr_ref, twi_ref, mh_ref, ml_ref, o_ref = refs
    for s in range(a_ref.shape[1]):
        twr, twi = twr_ref[s], twi_ref[s]
        ar, ai = a_ref[0, s], a_ref[1, s]
        t = jnp.concatenate([twr * ar - twi * ai, twr * ai + twi * ar], axis=0)
        x = _dot3(mh_ref[...], ml_ref[...], t)
        xr, xi = x[:n2], x[n2:]
        if not with_filter:
            o_ref[0, s], o_ref[1, s] = _filter_spectrum(xr, xi, x.shape[1] // 2)
            continue
        kr, ki = k_ref[0, s], k_ref[1, s]
        y = jnp.concatenate([xr * kr - xi * ki, xr * ki + xi * kr], axis=0)
        b = _dot3(ih_ref[...], il_ref[...], y)
        br, bi = b[:n2], b[n2:]
        o_ref[0, s] = twr * br + twi * bi
        o_ref[1, s] = twr * bi - twi * br


def fft_mid(consts, a, spectrum=None):
    _, n1, n2, c = a.shape
    c_out = c if spectrum is not None else c // 2
    kb = _pick(n1, (8, 4, 2, 1))
    slab = pl.BlockSpec((2, kb, n2, c), lambda k: (0, k, 0, 0))
    out_slab = pl.BlockSpec((2, kb, n2, c_out), lambda k: (0, k, 0, 0))
    tw = pl.BlockSpec((kb, n2, 1), lambda k: (k, 0, 0))
    mat = pl.BlockSpec((2 * n2, 2 * n2), lambda k: (0, 0))
    in_specs = [slab, tw, tw, mat, mat]
    args = [a, consts["twr"], consts["twi"], *consts["m2"]]
    if spectrum is not None:
        in_specs += [mat, mat, slab]
        args += [*consts["m2inv"], spectrum]
    return pl.pallas_call(
        functools.partial(_fft_mid_kernel, n2=n2, with_filter=spectrum is not None),
        grid=(n1 // kb,),
        in_specs=in_specs,
        out_specs=out_slab,
        out_shape=jax.ShapeDtypeStruct((2, n1, n2, c_out), F32),
        compiler_params=_cparams("parallel"),
        name="fft_mid",
    )(*args)


def hyena_long_conv(x0, z, h, abs_sum, bias, length):
    c = MIX_W
    n = 2 * length
    n2 = 128 if length >= 1024 else 1
    n1 = n // n2
    consts = _dft_consts(n1, n2)
    nh = _dft_rows(n1)
    inv_scale = 1.0 / ((abs_sum[:, :c] + abs_sum[:, c:]) * n)
    filt_f = dft_left(consts["fwd_half"], h.reshape(n1 // 2, n2 * 2 * c))
    xin = z.reshape(n1 // 2, n2 * c)
    if n2 > 1:
        spec = fft_mid(consts, filt_f.reshape(2, nh, n2, 2 * c))
        a = dft_left(consts["fwd_half"], xin)
        b = fft_mid(consts, a.reshape(2, nh, n2, c), spec).reshape(2 * nh, n2 * c)
    else:
        b = dft_left(consts["fwd_half"], xin, filt_f)
    y = dft_inv_post(consts["inv_half"], b, x0.reshape(n1 // 2, n2 * c), xin,
                     jnp.tile(bias.reshape(1, c), (1, n2)), jnp.tile(inv_scale, (1, n2)))
    return y.reshape(length, c)


def _merge_kernel(*refs):
    ys, gs, ws = refs[0:N_BRANCH], refs[N_BRANCH:2 * N_BRANCH], refs[2 * N_BRANCH:3 * N_BRANCH]
    o_ref, wb_ref = refs[3 * N_BRANCH:]

    @pl.when(pl.program_id(1) == 0)
    def _():
        for i in range(N_BRANCH):
            wb_ref[i] = ws[i][...].astype(BF16)

    acc = None
    for i in range(N_BRANCH):
        t = jnp.dot(ys[i][...], wb_ref[i], preferred_element_type=F32) * _sigmoid(gs[i][...].astype(F32))
        acc = t if acc is None else acc + t
    o_ref[...] = acc.astype(o_ref.dtype)


def merge_branches(ys, gates, gate_col0, branch_w, layer):
    m = ys[0].shape[0]
    d = branch_w.shape[3]
    tm = _pick(m, (640, 512, 256, 128))
    tn = _pick(d, (1024, 512, 256, 128))
    nj = d // tn
    assert gate_col0 % tn == 0
    g0 = gate_col0 // tn
    y_specs = [pl.BlockSpec((tm, MIX_W), lambda j, i: (i, 0))] * N_BRANCH
    g_specs = [pl.BlockSpec((tm, tn), functools.partial(lambda j, i, b: (i, g0 + b * nj + j), b=b)) for b in range(N_BRANCH)]
    w_specs = [pl.BlockSpec((None, None, MIX_W, tn), functools.partial(lambda j, i, b: (layer, b, 0, j), b=b),
                            pipeline_mode=pl.Buffered(1)) for b in range(N_BRANCH)]
    return pl.pallas_call(
        _merge_kernel,
        grid=(nj, m // tm),
        in_specs=y_specs + g_specs + w_specs,
        out_specs=pl.BlockSpec((tm, tn), lambda j, i: (i, j)),
        out_shape=jax.ShapeDtypeStruct((m, d), BF16),
        scratch_shapes=[pltpu.VMEM((N_BRANCH, MIX_W, tn), BF16)],
        compiler_params=_cparams("parallel", "arbitrary"),
        name="merge_branches",
    )(*ys, *([gates] * N_BRANCH), *([branch_w] * N_BRANCH))


FFN_HALO = 16


def _sigmoid(x):
    return 0.5 + 0.5 * jnp.tanh(0.5 * x)


def _ffn_up_kernel(a_ref, wa32_ref, wg32_ref, cwa_ref, cwg_ref, cba_ref, cbg_ref, o_ref, wa_ref, wg_ref,
                   *, tm, chunk, fixes):
    hl = FFN_HALO

    @pl.when(pl.program_id(1) == 0)
    def _():
        wa_ref[...] = wa32_ref[...].astype(BF16)
        wg_ref[...] = wg32_ref[...].astype(BF16)

    def conv(a, rows, w_ref, cw_ref, cb_ref, no_prev=None, no_next=None):
        u = jnp.dot(a, w_ref[...], preferred_element_type=F32)
        up, dn = u[hl - 1:hl - 1 + rows], u[hl + 1:hl + 1 + rows]
        if no_prev is not None:
            up = jnp.where(no_prev, 0.0, up)
            dn = jnp.where(no_next, 0.0, dn)
        return up * cw_ref[0:1, :] + u[hl:hl + rows] * cw_ref[1:2, :] + dn * cw_ref[2:3, :] + cb_ref[...]

    def silu_gate(a, rows, *masks):
        g = conv(a, rows, wg_ref, cwg_ref, cbg_ref, *masks)
        return g * _sigmoid(g)

    def gated(a, rows, *masks):
        return (silu_gate(a, rows, *masks) * conv(a, rows, wa_ref, cwa_ref, cba_ref, *masks)).astype(o_ref.dtype)

    pieces = [a_ref[c * chunk:(c + 1) * chunk + 2 * hl, :] for c in range(tm // chunk)]
    gates = [silu_gate(a, chunk) for a in pieces]
    for c, a in enumerate(pieces):
        o_ref[c * chunk:(c + 1) * chunk, :] = (gates[c] * conv(a, chunk, wa_ref, cwa_ref, cba_ref)).astype(o_ref.dtype)

    for blk, g0, first_row in fixes:
        @pl.when(pl.program_id(1) == blk)
        def _(g0=g0, first_row=first_row):
            row = blk * tm + g0 + lax.broadcasted_iota(jnp.int32, (hl, 1), 0)
            o_ref[g0:g0 + hl, :] = gated(a_ref[g0:g0 + 3 * hl, :], hl, row == first_row, row == first_row - 1)


FFN_TM = 1280


def ffn_up(a_ext, m, w_up, layer, conv_w, conv_b, n_latent):
    d = a_ext.shape[1]
    f = w_up.shape[2] // 2
    tm = FFN_TM
    tn = _pick(f, (512, 256, 128))
    chunk = _pick(tm, (640, 256, 128))
    nb, nj = m // tm, f // tn
    te = tm + 2 * FFN_HALO
    assert n_latent % FFN_HALO == 0 and 0 < n_latent < m and a_ext.shape[0] == nb * te
    fixes = []
    for r in (n_latent - 1, n_latent):
        g = r // FFN_HALO * FFN_HALO
        fixes.append((g // tm, g % tm, n_latent))
    wspec = lambda off: pl.BlockSpec((None, d, tn), lambda j, i: (layer, 0, off + j), pipeline_mode=pl.Buffered(1))
    return pl.pallas_call(
        functools.partial(_ffn_up_kernel, tm=tm, chunk=chunk, fixes=tuple(fixes)),
        grid=(nj, nb),
        in_specs=[pl.BlockSpec((te, d), lambda j, i: (i, 0)),
                  wspec(0), wspec(nj),
                  pl.BlockSpec((3, tn), lambda j, i: (0, j)),
                  pl.BlockSpec((3, tn), lambda j, i: (0, nj + j)),
                  pl.BlockSpec((1, tn), lambda j, i: (0, j)),
                  pl.BlockSpec((1, tn), lambda j, i: (0, nj + j))],
        out_specs=pl.BlockSpec((tm, tn), lambda j, i: (i, j)),
        out_shape=jax.ShapeDtypeStruct((m, f), BF16),
        scratch_shapes=[pltpu.VMEM((d, tn), BF16), pltpu.VMEM((d, tn), BF16)],
        compiler_params=_cparams("parallel", "arbitrary"),
        name="ffn_up",
    )(a_ext, w_up, w_up, conv_w, conv_w, conv_b.reshape(1, -1), conv_b.reshape(1, -1))


def kernel(x, c, ctx, c_ctx, ada_w, ada_b, norm1_g, norm2_g, w_in, a_qn_g, a_kn_g, b_rpb,
           c_gate_w2, c_gate_b, c_norm_g, d_conv_w, d_conv_b, d_ffn_w1, d_ffn_b1, d_ffn_w2,
           d_ffn_b2, d_ffn_w3, d_ffn_b3, d_ffn_w4, d_sin_freq, d_bias, branch_w, w_out,
           ffn_up_w, ffn_conv_w, ffn_conv_b, ffn_down, final_norm_g):
    depth = ada_w.shape[0]
    n_lat, d = x.shape[1], x.shape[2]
    n_ctx = ctx.shape[1]
    h = jnp.concatenate([x[0], ctx[0]], axis=0)
    cvec = jnp.concatenate([c[0:1], c_ctx[None, :], jnp.zeros((14, d), F32)], axis=0)
    cvec = (cvec * jax.nn.sigmoid(cvec)).astype(BF16)
    ctab, stab = rope_tables(n_lat, n_ctx)
    zeros_c = jnp.zeros((n_ctx, MIX_W), BF16)
    wt_in = jnp.swapaxes(w_in, 1, 2)

    for i in range(depth):
        ctx_out = i < depth - 1
        mod = matmul(cvec, ada_w, i, 0, ada_w.shape[2], F32, bias=ada_b[i])[0:2]
        sh1, sc1, g1, sh2, sc2, g2 = jnp.split(mod, 6, axis=-1)
        n1 = mod_norm(h, norm1_g[i], sh1, sc1, n_lat, BF16)
        ca_end = IN_CA + 2 * C_GATE_RANK
        w_ca = jnp.pad(wt_in[i, IN_CA:ca_end, :].T, ((0, 0), (0, 128 - 2 * C_GATE_RANK)))[None]
        p1 = matmul_wt(n1, wt_in, i, 0, PA_WIDTH, F32)
        pb = matmul_wt(n1, wt_in, i, ca_end, PB_WIDTH, F32)
        gates = matmul_wt(n1, wt_in, i, ca_end + PB_WIDTH, N_BRANCH * d, BF16)
        ca = matmul(n1, w_ca, 0, 0, 128, F32)

        q, k, v = a_prep(p1, ctab, stab, a_qn_g[i], a_kn_g[i])
        ya_l = flash_attention(q, k, v, A_KV_HEADS, n_rows=n_lat)
        yb_l = neighbourhood_attention(p1, na_bias_table(b_rpb[i]), n_lat, n_ctx)
        yc = gla_mixer(p1, pb, ca, c_gate_w2[i], c_gate_b[i], c_norm_g[i], n_lat, n_ctx)
        x0, z = hyena_pre(pb, d_conv_w[i], d_conv_b[i], n_lat)
        filt = (d_ffn_w1[i], d_ffn_b1[i], d_ffn_w2[i], d_ffn_b2[i], d_ffn_w3[i], d_ffn_b3[i], d_ffn_w4[i],
                d_sin_freq[i])
        hl, sl = hyena_filter(n_lat, *filt)
        yd_l = hyena_long_conv(x0[:n_lat], z[:n_lat], hl, sl, d_bias[i], n_lat)
        if ctx_out:
            ya_c = flash_attention(q[n_lat:], k[n_lat:], v[n_lat:], A_KV_HEADS)
            pc = lax.optimization_barrier(p1[n_lat:])
            yb_c = flash_attention((pc[:, P_BQ:P_BK] * QK_SCALE).astype(BF16), pc[:, P_BK:P_BV].astype(BF16),
                                   with_ones(pc[:, P_BV:P_CQ].astype(BF16), B_HEADS), B_HEADS)
            hc, sc = hyena_filter(n_ctx, *filt)
            yd_c = hyena_long_conv(x0[n_lat:], z[n_lat:], hc, sc, d_bias[i], n_ctx)
        else:
            ya_c = yb_c = yd_c = zeros_c
        ys = [jnp.concatenate([ya_l, ya_c], axis=0), jnp.concatenate([yb_l, yb_c], axis=0), yc,
              jnp.concatenate([yd_l, yd_c], axis=0)]
        merged = merge_branches(ys, gates, 0, branch_w, i)
        h = matmul_residual(merged, w_out, i, h, g1, n_lat)
        n2_ext = mod_norm_halo(h, norm2_g[i], sh2, sc2, n_lat, FFN_TM, FFN_HALO)
        act = ffn_up(n2_ext, h.shape[0], ffn_up_w, i, ffn_conv_w[i], ffn_conv_b[i], n_lat)
        h = matmul_residual(act, ffn_down, i, h, g2, n_lat)

    zero2 = jnp.zeros((2, d), F32)
    out = mod_norm(h, final_norm_g, zero2, zero2, n_lat, F32, n_rows=n_lat)
    return out[None]
```
